```python
import numpy as np
import jax
import jax.numpy as jnp
from jax import lax

D_MODEL = 2048
BATCH = 8
SEQ = 2048
DEPTH = 1
DEC_BATCH = 128
DEC_SEQ = 8
PAST_LEN = 2048
PAGE_SIZE = 128

HEAD_DIM = 128
NSA_HEADS = 8
NSA_KV_HEADS = 2
NSA_GROUP = NSA_HEADS // NSA_KV_HEADS
CMP_LEN = 32
CMP_STRIDE = 16
SEL_BLOCK = 64
SEL_TOP_N = 16
WINDOW = 512
FORCE_BONUS = 1.0e4
MOBA_HEADS = 4
MOBA_BLOCK = 256
MOBA_TOP_K = 3
MEM_HEADS = 4
N_MEM = 256
NSA_Q_BLOCK = 64
MOBA_Q_BLOCK = 32
NSA_WIDTH = NSA_HEADS * HEAD_DIM
NSA_KV_WIDTH = NSA_KV_HEADS * HEAD_DIM
MOBA_WIDTH = MOBA_HEADS * HEAD_DIM
MEM_WIDTH = MEM_HEADS * HEAD_DIM
N_BRANCH = 3
IN_SPLITS = (NSA_WIDTH, 2 * NSA_KV_WIDTH, 2 * NSA_KV_WIDTH, 2 * NSA_KV_WIDTH, 3 * NSA_HEADS, NSA_WIDTH,
             MOBA_WIDTH, 2 * MOBA_WIDTH, MOBA_WIDTH, MEM_WIDTH, MEM_WIDTH, N_BRANCH * D_MODEL)
IN_WIDTH = sum(IN_SPLITS)
RMS_EPS = 1e-6
NEG_INF = -1e30

kernel_name = 'hybrid_nsa_moba_memory_step'


def alibi_slopes(n_heads):
    return jnp.asarray(2.0 ** (-8.0 * np.arange(1, n_heads + 1) / n_heads), dtype=jnp.float32)


def rmsnorm(x, g):
    x32 = x.astype(jnp.float32)
    inv = lax.rsqrt(jnp.mean(x32 * x32, axis=-1, keepdims=True) + RMS_EPS)
    return (x32 * inv * g.astype(jnp.float32)).astype(x.dtype)


def masked_softmax(s, mask):
    s = jnp.where(mask, s, NEG_INF)
    e = jnp.where(mask, jnp.exp(s - jnp.max(s, axis=-1, keepdims=True)), 0.0)
    d = jnp.sum(e, axis=-1, keepdims=True)
    return e / jnp.where(d > 0, d, 1.0)


def in_projection(x, norm_g, w_in):
    h = rmsnorm(x, norm_g) @ w_in
    return jnp.split(h, [int(c) for c in np.cumsum(IN_SPLITS)[:-1]], axis=-1)


def gather_pages(pool, page_table):
    rows = pool[page_table]
    return rows.reshape((page_table.shape[0], -1) + pool.shape[2:])


def nsa_heads(a):
    b, t = a.shape[:2]
    return a.reshape(b, t, NSA_KV_HEADS, NSA_GROUP, -1).transpose(0, 2, 3, 1, 4)


def nsa_rows(o):
    b, g, r, t, d = o.shape
    return o.transpose(0, 3, 1, 2, 4).reshape(b, t, g * r * d)


def split_heads(a, n_heads):
    b, t = a.shape[:2]
    return a.reshape(b, t, n_heads, HEAD_DIM).transpose(0, 2, 1, 3)


def head_rows(o):
    b, h, t, d = o.shape
    return o.transpose(0, 2, 1, 3).reshape(b, t, h * d)


def compress_tokens(kv, phi_pos, phi_w1, phi_w2):
    b, seq_len = kv.shape[:2]
    n_sub = CMP_LEN // CMP_STRIDE
    n_half = seq_len // CMP_STRIDE
    n_cmp = n_half - n_sub + 1
    sub = kv[:, :n_half * CMP_STRIDE].reshape(b, n_half, CMP_STRIDE, 2, NSA_KV_HEADS, HEAD_DIM)
    pos = phi_pos.reshape(2, n_sub, CMP_STRIDE, HEAD_DIM)
    w1 = phi_w1.reshape(2, n_sub, CMP_STRIDE, HEAD_DIM, HEAD_DIM)
    hid = None
    for m in range(n_sub):
        part = jnp.einsum('bnscgd,csdh->bncgh', sub + pos[:, m].transpose(1, 0, 2)[:, :, None, :], w1[:, m])
        part = part[:, m:m + n_cmp]
        hid = part if hid is None else hid + part
    out = jnp.einsum('bncgh,che->bncge', jax.nn.silu(hid), phi_w2)
    ends = jnp.asarray(np.arange(n_cmp) * CMP_STRIDE + CMP_LEN - 1, dtype=jnp.int32)
    return out[:, :, 0].transpose(0, 2, 1, 3), out[:, :, 1].transpose(0, 2, 1, 3), ends


def to_blocks(kv, block):
    b, seq_len, _, h, _ = kv.shape
    nb = -(-seq_len // block)
    kv = jnp.pad(kv, ((0, 0), (0, nb * block - seq_len), (0, 0), (0, 0), (0, 0)))
    kv = kv.reshape(b, nb, block, 2, h, HEAD_DIM).transpose(3, 0, 4, 1, 2, 5)
    return kv[0], kv[1]


def block_means(kb):
    return jnp.mean(kb.astype(jnp.float32), axis=3).astype(kb.dtype)


def nsa_attend(q, gates, q_pos, kc, vc, kc_end, ks, vs, kw, vw, kw_pos, slopes):
    scale = HEAD_DIM ** -0.5
    b, g, r, tq, _ = q.shape
    s = jnp.einsum('bgrtd,bgnd->bgrtn', q, kc).astype(jnp.float32) * scale
    p_cmp = masked_softmax(s, kc_end[None, :] <= q_pos[:, None])
    o_cmp = jnp.einsum('bgrtn,bgnd->bgrtd', p_cmp.astype(vc.dtype), vc)
    n_cmp, n_sel = kc.shape[2], ks.shape[2]
    c_start = np.arange(n_cmp)[:, None] * CMP_STRIDE
    s_start = np.arange(n_sel)[None, :] * SEL_BLOCK
    cover = jnp.asarray(((c_start < s_start + SEL_BLOCK) & (c_start + CMP_LEN > s_start)).astype(np.float32))
    imp = jnp.einsum('bgrtn,ns->bgts', p_cmp, cover)
    blk = jnp.arange(n_sel, dtype=jnp.int32)[None, :]
    cur = (q_pos // SEL_BLOCK)[:, None]
    forced = (blk == 0) | (blk == cur) | (blk == cur - 1)
    allowed = blk * SEL_BLOCK <= q_pos[:, None]
    score = jnp.where(allowed, imp + FORCE_BONUS * forced, NEG_INF)
    n_top = min(SEL_TOP_N, n_sel)
    _, idx = lax.top_k(score, n_top)
    bi = jnp.arange(b)[:, None, None, None]
    gi = jnp.arange(g)[None, :, None, None]
    m_keys = n_top * SEL_BLOCK
    k_sel = ks[bi, gi, idx].reshape(b, g, tq, m_keys, HEAD_DIM)
    v_sel = vs[bi, gi, idx].reshape(b, g, tq, m_keys, HEAD_DIM)
    dist = q_pos[:, None, None] - (idx[..., None] * SEL_BLOCK + jnp.arange(SEL_BLOCK, dtype=jnp.int32))
    dist = dist.reshape(b, g, 1, tq, m_keys)
    s = jnp.einsum('bgrtd,bgtmd->bgrtm', q, k_sel).astype(jnp.float32) * scale
    s = s - slopes[None, :, :, None, None] * dist
    p = masked_softmax(s, dist >= 0)
    o_sel = jnp.einsum('bgrtm,bgtmd->bgrtd', p.astype(v_sel.dtype), v_sel)
    dist_w = q_pos[:, None] - kw_pos[None, :]
    s = jnp.einsum('bgrtd,bgld->bgrtl', q, kw).astype(jnp.float32) * scale
    s = s - slopes[None, :, :, None, None] * dist_w
    p = masked_softmax(s, (dist_w >= 0) & (dist_w < WINDOW) & (kw_pos[None, :] >= 0))
    o_win = jnp.einsum('bgrtl,bgld->bgrtd', p.astype(vw.dtype), vw)
    return gates[..., 0:1] * o_cmp + gates[..., 1:2] * o_sel + gates[..., 2:3] * o_win


def nsa_prompt(q, gates, kc, vc, kc_end, ks, vs, kw, vw, slopes):
    b, g, r, t, dh = q.shape
    nq = t // NSA_Q_BLOCK
    span = WINDOW + NSA_Q_BLOCK

    def to_chunks(a):
        return jnp.moveaxis(a.reshape(b, g, r, nq, NSA_Q_BLOCK, a.shape[-1]), 3, 0)

    kw_pad = jnp.pad(kw, ((0, 0), (0, 0), (WINDOW, 0), (0, 0)))
    vw_pad = jnp.pad(vw, ((0, 0), (0, 0), (WINDOW, 0), (0, 0)))
    starts = jnp.arange(nq, dtype=jnp.int32) * NSA_Q_BLOCK

    def one(args):
        q_c, g_c, s0 = args
        q_pos = s0 + jnp.arange(NSA_Q_BLOCK, dtype=jnp.int32)
        kw_c = lax.dynamic_slice_in_dim(kw_pad, s0, span, axis=2)
        vw_c = lax.dynamic_slice_in_dim(vw_pad, s0, span, axis=2)
        kw_pos = s0 - WINDOW + jnp.arange(span, dtype=jnp.int32)
        return nsa_attend(q_c, g_c, q_pos, kc, vc, kc_end, ks, vs, kw_c, vw_c, kw_pos, slopes)

    out = lax.map(one, (to_chunks(q), to_chunks(gates), starts))
    return jnp.moveaxis(out, 0, 3).reshape(b, g, r, t, dh)


def moba_attend(q, q_pos, kb, vb, kmean, slopes):
    scale = HEAD_DIM ** -0.5
    b, h, tq, _ = q.shape
    nb = kb.shape[2]
    cur = q_pos // MOBA_BLOCK
    gate = jnp.einsum('bhtd,bhnd->bhtn', q, kmean).astype(jnp.float32)
    past = jnp.arange(nb, dtype=jnp.int32)[None, :] < cur[:, None]
    gate = jnp.where(past, gate, NEG_INF)
    own = jnp.broadcast_to(cur, (b, h, tq))[..., None]
    n_top = min(MOBA_TOP_K, nb - 1)
    if n_top > 0:
        top_val, top_idx = lax.top_k(gate, n_top)
        idx = jnp.concatenate([top_idx.astype(jnp.int32), own], axis=-1)
        slot_ok = jnp.concatenate([top_val > 0.5 * NEG_INF, jnp.ones(own.shape, dtype=bool)], axis=-1)
    else:
        idx = own
        slot_ok = jnp.ones(own.shape, dtype=bool)
    n_slot = idx.shape[-1]
    bi = jnp.arange(b)[:, None, None, None]
    hi = jnp.arange(h)[None, :, None, None]
    m_keys = n_slot * MOBA_BLOCK
    k_g = kb[bi, hi, idx].reshape(b, h, tq, m_keys, HEAD_DIM)
    v_g = vb[bi, hi, idx].reshape(b, h, tq, m_keys, HEAD_DIM)
    dist = q_pos[:, None, None] - (idx[..., None] * MOBA_BLOCK + jnp.arange(MOBA_BLOCK, dtype=jnp.int32))
    mask = (slot_ok[..., None] & (dist >= 0)).reshape(b, h, tq, m_keys)
    dist = dist.reshape(b, h, tq, m_keys)
    s = jnp.einsum('bhtd,bhtmd->bhtm', q, k_g).astype(jnp.float32) * scale
    s = s - slopes[None, :, None, None] * dist
    p = masked_softmax(s, mask)
    return jnp.einsum('bhtm,bhtmd->bhtd', p.astype(v_g.dtype), v_g)


def moba_prompt(q, kb, vb, kmean, slopes):
    b, h, t, dh = q.shape
    nq = t // MOBA_Q_BLOCK
    q_c = jnp.moveaxis(q.reshape(b, h, nq, MOBA_Q_BLOCK, dh), 2, 0)
    starts = jnp.arange(nq, dtype=jnp.int32) * MOBA_Q_BLOCK

    def one(args):
        qc, s0 = args
        return moba_attend(qc, s0 + jnp.arange(MOBA_Q_BLOCK, dtype=jnp.int32), kb, vb, kmean, slopes)

    out = lax.map(one, (q_c, starts))
    return jnp.moveaxis(out, 0, 2).reshape(b, h, t, dh)


def cross_attend(q, k, v):
    s = jnp.einsum('bhtd,bhmd->bhtm', q, k).astype(jnp.float32) * (HEAD_DIM ** -0.5)
    p = jax.nn.softmax(s, axis=-1)
    return jnp.einsum('bhtm,bhmd->bhtd', p.astype(v.dtype), v)


def merge_branches(x, o_nsa, z_nsa, o_moba, z_moba, o_mem, z_mem, gate_logits, w_br_nsa, w_br_moba, w_br_mem, w_out):
    branches = jnp.stack([(o_nsa * jax.nn.silu(z_nsa)) @ w_br_nsa,
                          (o_moba * jax.nn.silu(z_moba)) @ w_br_moba,
                          (o_mem * jax.nn.silu(z_mem)) @ w_br_mem], axis=-2)
    gates = jax.nn.sigmoid(gate_logits.reshape(gate_logits.shape[:-1] + (N_BRANCH, D_MODEL)))
    return x + jnp.sum(gates * branches, axis=-2) @ w_out


def prompt_layer(x, mem, norm_g, w_in, phi_pos, phi_w1, phi_w2, mem_norm_g, w_mem_kv,
                 w_br_nsa, w_br_moba, w_br_mem, w_out):
    b, t, _ = x.shape
    (q_nsa, cmp_kv, sel_kv, win_kv, g_nsa, z_nsa, q_moba, moba_kv, z_moba,
     q_mem, z_mem, g_merge) = in_projection(x, norm_g, w_in)
    cmp_kv = cmp_kv.reshape(b, t, 2, NSA_KV_HEADS, HEAD_DIM)
    sel_kv = sel_kv.reshape(b, t, 2, NSA_KV_HEADS, HEAD_DIM)
    win_kv = win_kv.reshape(b, t, 2, NSA_KV_HEADS, HEAD_DIM)
    moba_kv = moba_kv.reshape(b, t, 2, MOBA_HEADS, HEAD_DIM)
    nsa_slopes = alibi_slopes(NSA_HEADS).reshape(NSA_KV_HEADS, NSA_GROUP)
    kc, vc, kc_end = compress_tokens(cmp_kv, phi_pos, phi_w1, phi_w2)
    ks, vs = to_blocks(sel_kv, SEL_BLOCK)
    o_nsa = nsa_prompt(nsa_heads(q_nsa), jax.nn.sigmoid(nsa_heads(g_nsa)), kc, vc, kc_end, ks, vs,
                       win_kv[:, :, 0].transpose(0, 2, 1, 3), win_kv[:, :, 1].transpose(0, 2, 1, 3), nsa_slopes)
    kb, vb = to_blocks(moba_kv, MOBA_BLOCK)
    o_moba = moba_prompt(split_heads(q_moba, MOBA_HEADS), kb, vb, block_means(kb), alibi_slopes(MOBA_HEADS))
    mem_kv = (rmsnorm(mem, mem_norm_g) @ w_mem_kv).reshape(b, mem.shape[1], 2, MEM_HEADS, HEAD_DIM)
    o_mem = cross_attend(split_heads(q_mem, MEM_HEADS), mem_kv[:, :, 0].transpose(0, 2, 1, 3),
                         mem_kv[:, :, 1].transpose(0, 2, 1, 3))
    y = merge_branches(x, nsa_rows(o_nsa), z_nsa, head_rows(o_moba), z_moba, head_rows(o_mem), z_mem, g_merge,
                       w_br_nsa, w_br_moba, w_br_mem, w_out)
    keep = min(WINDOW, t)
    return y, (cmp_kv, sel_kv, moba_kv, win_kv[:, t - keep:], mem_kv)


def sample_layer(x, cache_cmp, cache_sel, cache_moba, cache_win, cache_mem, page_table, norm_g, w_in,
                 phi_pos, phi_w1, phi_w2, w_br_nsa, w_br_moba, w_br_mem, w_out):
    b, t, _ = x.shape
    past = page_table.shape[1] * cache_cmp.shape[1]
    (q_nsa, cmp_kv, sel_kv, win_kv, g_nsa, z_nsa, q_moba, moba_kv, z_moba,
     q_mem, z_mem, g_merge) = in_projection(x, norm_g, w_in)
    cmp_kv = cmp_kv.reshape(b, t, 2, NSA_KV_HEADS, HEAD_DIM)
    sel_kv = sel_kv.reshape(b, t, 2, NSA_KV_HEADS, HEAD_DIM)
    win_kv = win_kv.reshape(b, t, 2, NSA_KV_HEADS, HEAD_DIM)
    moba_kv = moba_kv.reshape(b, t, 2, MOBA_HEADS, HEAD_DIM)
    nsa_slopes = alibi_slopes(NSA_HEADS).reshape(NSA_KV_HEADS, NSA_GROUP)
    q_pos = past + jnp.arange(t, dtype=jnp.int32)
    kc, vc, kc_end = compress_tokens(jnp.concatenate([gather_pages(cache_cmp, page_table), cmp_kv], axis=1),
                                     phi_pos, phi_w1, phi_w2)
    ks, vs = to_blocks(jnp.concatenate([gather_pages(cache_sel, page_table), sel_kv], axis=1), SEL_BLOCK)
    wb = cache_win.shape[1]
    win_all = jnp.concatenate([cache_win, win_kv], axis=1)
    kw_pos = past - wb + jnp.arange(wb + t, dtype=jnp.int32)
    o_nsa = nsa_attend(nsa_heads(q_nsa), jax.nn.sigmoid(nsa_heads(g_nsa)), q_pos, kc, vc, kc_end, ks, vs,
                       win_all[:, :, 0].transpose(0, 2, 1, 3), win_all[:, :, 1].transpose(0, 2, 1, 3),
                       kw_pos, nsa_slopes)
    kb, vb = to_blocks(jnp.concatenate([gather_pages(cache_moba, page_table), moba_kv], axis=1), MOBA_BLOCK)
    o_moba = moba_attend(split_heads(q_moba, MOBA_HEADS), q_pos, kb, vb, block_means(kb), alibi_slopes(MOBA_HEADS))
    o_mem = cross_attend(split_heads(q_mem, MEM_HEADS), cache_mem[:, :, 0].transpose(0, 2, 1, 3),
                         cache_mem[:, :, 1].transpose(0, 2, 1, 3))
    y = merge_branches(x, nsa_rows(o_nsa), z_nsa, head_rows(o_moba), z_moba, head_rows(o_mem), z_mem, g_merge,
                       w_br_nsa, w_br_moba, w_br_mem, w_out)
    keep = min(WINDOW, wb + t)
    return y, (cmp_kv, sel_kv, moba_kv, win_all[:, wb + t - keep:])


def setup_inputs(seed: int = 0) -> dict:
    key = jax.random.key(seed)
    ks = jax.random.split(key, 24)
    f32 = jnp.float32
    n_pages = PAST_LEN // PAGE_SIZE
    n_pool = (5 * DEC_BATCH * n_pages + 3) // 4
    win_rows = min(WINDOW, PAST_LEN)

    def nrm(k, shape, scale):
        return jax.random.normal(k, shape, f32) * scale

    page_table = jax.random.permutation(ks[0], n_pool)[:DEC_BATCH * n_pages].reshape(DEC_BATCH, n_pages).astype(jnp.int32)
    return {
        'x_prompt': nrm(ks[1], (BATCH, SEQ, D_MODEL), 1.0),
        'x_sample': nrm(ks[2], (DEC_BATCH, DEC_SEQ, D_MODEL), 1.0),
        'cache_cmp_kv': nrm(ks[3], (DEPTH, n_pool, PAGE_SIZE, 2, NSA_KV_HEADS, HEAD_DIM), 1.0),
        'cache_sel_kv': nrm(ks[4], (DEPTH, n_pool, PAGE_SIZE, 2, NSA_KV_HEADS, HEAD_DIM), 1.0),
        'cache_moba_kv': nrm(ks[5], (DEPTH, n_pool, PAGE_SIZE, 2, MOBA_HEADS, HEAD_DIM), 1.0),
        'cache_win_kv': nrm(ks[6], (DEPTH, DEC_BATCH, win_rows, 2, NSA_KV_HEADS, HEAD_DIM), 1.0),
        'cache_mem_kv': nrm(ks[7], (DEPTH, DEC_BATCH, N_MEM, 2, MEM_HEADS, HEAD_DIM), 1.0),
        'page_table': page_table,
        'mem_prompt': nrm(ks[8], (BATCH, N_MEM, D_MODEL), 1.0),
        'norm_g': 1.0 + nrm(ks[9], (DEPTH, D_MODEL), 0.1),
        'w_in': nrm(ks[10], (DEPTH, D_MODEL, IN_WIDTH), D_MODEL ** -0.5),
        'phi_pos': nrm(ks[11], (DEPTH, 2, CMP_LEN, HEAD_DIM), 0.1),
        'phi_w1': nrm(ks[12], (DEPTH, 2, CMP_LEN, HEAD_DIM, HEAD_DIM), (CMP_LEN * HEAD_DIM) ** -0.5),
        'phi_w2': nrm(ks[13], (DEPTH, 2, HEAD_DIM, HEAD_DIM), HEAD_DIM ** -0.5),
        'mem_norm_g': 1.0 + nrm(ks[14], (DEPTH, D_MODEL), 0.1),
        'w_mem_kv': nrm(ks[15], (DEPTH, D_MODEL, 2 * MEM_WIDTH), D_MODEL ** -0.5),
        'w_br_nsa': nrm(ks[16], (DEPTH, NSA_WIDTH, D_MODEL), NSA_WIDTH ** -0.5),
        'w_br_moba': nrm(ks[17], (DEPTH, MOBA_WIDTH, D_MODEL), MOBA_WIDTH ** -0.5),
        'w_br_mem': nrm(ks[18], (DEPTH, MEM_WIDTH, D_MODEL), MEM_WIDTH ** -0.5),
        'w_out': nrm(ks[19], (DEPTH, D_MODEL, D_MODEL), D_MODEL ** -0.5),
        'final_norm_g': 1.0 + nrm(ks[20], (D_MODEL,), 0.1),
    }


def reference(x_prompt, x_sample, cache_cmp_kv, cache_sel_kv, cache_moba_kv, cache_win_kv, cache_mem_kv,
              page_table, mem_prompt, norm_g, w_in, phi_pos, phi_w1, phi_w2, mem_norm_g, w_mem_kv,
              w_br_nsa, w_br_moba, w_br_mem, w_out, final_norm_g):
    hp = x_prompt
    hs = x_sample
    ps, ss = [], []
    for l in range(DEPTH):
        hp, st_p = prompt_layer(hp, mem_prompt, norm_g[l], w_in[l], phi_pos[l], phi_w1[l], phi_w2[l],
                                mem_norm_g[l], w_mem_kv[l], w_br_nsa[l], w_br_moba[l], w_br_mem[l], w_out[l])
        hs, st_s = sample_layer(hs, cache_cmp_kv[l], cache_sel_kv[l], cache_moba_kv[l], cache_win_kv[l],
                                cache_mem_kv[l], page_table, norm_g[l], w_in[l], phi_pos[l], phi_w1[l], phi_w2[l],
                                w_br_nsa[l], w_br_moba[l], w_br_mem[l], w_out[l])
        ps.append(st_p)
        ss.append(st_s)

    def stack(states, i):
        return jnp.stack([s[i] for s in states], axis=0)

    y_prompt = rmsnorm(hp, final_norm_g)
    y_sample = rmsnorm(hs, final_norm_g)
    return (y_prompt, y_sample, stack(ps, 0), stack(ps, 1), stack(ps, 2), stack(ps, 3), stack(ps, 4),
            stack(ss, 0), stack(ss, 1), stack(ss, 2), stack(ss, 3))
```

```python
import functools

import numpy as np
import jax
import jax.numpy as jnp
from jax import lax
from jax.experimental import pallas as pl
from jax.experimental.pallas import tpu as pltpu

HEAD_DIM = 128
NSA_HEADS = 8
NSA_KV_HEADS = 2
NSA_GROUP = NSA_HEADS // NSA_KV_HEADS
CMP_LEN = 32
CMP_STRIDE = 16
SEL_BLOCK = 64
SEL_TOP_N = 16
WINDOW = 512
FORCE_BONUS = 1.0e4
MOBA_HEADS = 4
MOBA_BLOCK = 256
MOBA_TOP_K = 3
MEM_HEADS = 4
N_BRANCH = 3
RMS_EPS = 1e-6
NEG_INF = -1e30
SCALE = HEAD_DIM ** -0.5

LANES = 128
VMEM_LIMIT = 56 * 1024 * 1024

F32 = jnp.float32
BF16 = jnp.bfloat16


def _params(n_axes, vmem=None):
    return pltpu.CompilerParams(dimension_semantics=("parallel",) * n_axes, vmem_limit_bytes=vmem)


def _dot(a, b):
    return jnp.dot(a, b, preferred_element_type=F32)


def _dot_nt(a, b):
    return lax.dot_general(a, b, (((1,), (1,)), ((), ())), preferred_element_type=F32)


def _sigmoid(x):
    return 1.0 / (1.0 + jnp.exp(-x))


def _rms_kernel(x_ref, g_ref, o_ref):
    x = x_ref[...]
    inv = lax.rsqrt(jnp.mean(x * x, axis=-1, keepdims=True) + RMS_EPS)
    o_ref[...] = (x * inv * g_ref[...]).astype(o_ref.dtype)


def rmsnorm_bf16(x, g, tm):
    n, d = x.shape
    return pl.pallas_call(
        _rms_kernel,
        grid=(n // tm,),
        in_specs=[pl.BlockSpec((tm, d), lambda i: (i, 0)), pl.BlockSpec((1, d), lambda i: (0, 0))],
        out_specs=pl.BlockSpec((tm, d), lambda i: (i, 0)),
        out_shape=jax.ShapeDtypeStruct((n, d), BF16),
        compiler_params=_params(1, VMEM_LIMIT),
        name="rmsnorm",
    )(x, g.reshape(1, d))


def _mm_kernel(x_ref, w_ref, o_ref):
    o_ref[...] = _dot(x_ref[...], w_ref[...]).astype(o_ref.dtype)


def matmul(x, w, out_dtype, tm, tn, name):
    m, k = x.shape
    n = w.shape[1]
    tm, tn = min(tm, m), min(tn, n)
    return pl.pallas_call(
        _mm_kernel,
        grid=(m // tm, n // tn),
        in_specs=[pl.BlockSpec((tm, k), lambda i, j: (i, 0)), pl.BlockSpec((k, tn), lambda i, j: (0, j))],
        out_specs=pl.BlockSpec((tm, tn), lambda i, j: (i, j)),
        out_shape=jax.ShapeDtypeStruct((m, n), out_dtype),
        compiler_params=_params(2, VMEM_LIMIT),
        name=name,
    )(x, w)


def _mm_rows_kernel(x_ref, w_ref, o_ref):
    tm = x_ref.shape[0]
    n_slab = w_ref.shape[1] // LANES
    res = _dot(x_ref[...], w_ref[...])
    for j in range(n_slab):
        o_ref[pl.ds(j, tm, stride=n_slab), :] = res[:, j * LANES:(j + 1) * LANES]


def matmul_rows(x, w, tm, name):
    m, k = x.shape
    n = w.shape[1]
    n_slab = n // LANES
    tm = min(tm, m)
    return pl.pallas_call(
        _mm_rows_kernel,
        grid=(m // tm,),
        in_specs=[pl.BlockSpec((tm, k), lambda i: (i, 0)), pl.BlockSpec((k, n), lambda i: (0, 0))],
        out_specs=pl.BlockSpec((tm * n_slab, LANES), lambda i: (i, 0)),
        out_shape=jax.ShapeDtypeStruct((m * n_slab, LANES), F32),
        compiler_params=_params(1, VMEM_LIMIT),
        name=name,
    )(x, w)


def _compress_kernel(pt_ref, *refs, n_pages, page):
    page_refs = refs[:n_pages]
    w1_ref, pos_ref, w2_ref, out_ref, lhs_scr = refs[n_pages:]
    sub_per_page = page // CMP_STRIDE
    n_sub = n_pages * sub_per_page
    for p in range(n_pages):
        for s in range(CMP_STRIDE):
            for cg in range(4):
                lhs_scr[cg, p * sub_per_page:(p + 1) * sub_per_page, s * HEAD_DIM:(s + 1) * HEAD_DIM] = (
                    page_refs[p][pl.ds(4 * s + cg, sub_per_page, stride=4 * CMP_STRIDE), :])
    row = lax.broadcasted_iota(jnp.int32, (n_sub, HEAD_DIM), 0)
    for c in range(2):
        w1 = w1_ref[c]
        bias = _dot(pos_ref[c], w1)
        b0 = bias[0:1, :HEAD_DIM]
        b1 = bias[1:2, HEAD_DIM:]
        for g in range(2):
            a = _dot(lhs_scr[c * 2 + g].astype(BF16), w1)
            a0 = a[:, :HEAD_DIM] + b0
            a1 = a[:, HEAD_DIM:] + b1
            hid = a0 + pltpu.roll(a1, shift=n_sub - 1, axis=0)
            act = hid * _sigmoid(hid)
            o = _dot(act.astype(BF16), w2_ref[c])
            out_ref[0, c, g] = jnp.where(row < n_sub - 1, o, 0.0)


def _page_specs(n_pages, rows):
    return [pl.BlockSpec((rows, LANES), functools.partial(lambda b, pt, p: (pt[b * n_pages + p], 0), p=p))
            for p in range(n_pages)]


def compress_tokens(pool, page_table, page, w1cat, pos8, w2):
    nb, n_pages = page_table.shape
    n_sub = n_pages * page // CMP_STRIDE
    const = lambda shape: pl.BlockSpec(shape, lambda b, pt: (0,) * len(shape))
    grid_spec = pltpu.PrefetchScalarGridSpec(
        num_scalar_prefetch=1,
        grid=(nb,),
        in_specs=_page_specs(n_pages, 4 * page) + [const(w1cat.shape), const(pos8.shape), const(w2.shape)],
        out_specs=pl.BlockSpec((1, 2, 2, n_sub, HEAD_DIM), lambda b, pt: (b, 0, 0, 0, 0)),
        scratch_shapes=[pltpu.VMEM((4, n_sub, CMP_STRIDE * HEAD_DIM), F32)],
    )
    return pl.pallas_call(
        functools.partial(_compress_kernel, n_pages=n_pages, page=page),
        grid_spec=grid_spec,
        out_shape=jax.ShapeDtypeStruct((nb, 2, 2, n_sub, HEAD_DIM), F32),
        compiler_params=_params(1, VMEM_LIMIT),
        name="compress",
    )(page_table.reshape(-1), *([pool] * n_pages), w1cat, pos8, w2)


def compress_weights(phi_pos, phi_w1, phi_w2):
    n_half = CMP_LEN // CMP_STRIDE
    w1 = phi_w1.reshape(2, n_half, CMP_STRIDE, HEAD_DIM, HEAD_DIM).transpose(0, 2, 3, 1, 4)
    w1cat = w1.reshape(2, CMP_STRIDE * HEAD_DIM, n_half * HEAD_DIM).astype(BF16)
    pos = phi_pos.reshape(2, n_half, CMP_STRIDE * HEAD_DIM)
    pos8 = jnp.pad(pos, ((0, 0), (0, 8 - n_half), (0, 0))).astype(BF16)
    return w1cat, pos8, phi_w2.astype(BF16)


def _online_update(idx, s, mask, v, m_scr, l_scr, acc_scr):
    s = jnp.where(mask, s, NEG_INF)
    m_prev = m_scr[idx]
    m_new = jnp.maximum(m_prev, jnp.max(s, axis=-1, keepdims=True))
    alpha = jnp.exp(m_prev - m_new)
    e = jnp.where(mask, jnp.exp(s - m_new), 0.0)
    l_scr[idx] = alpha * l_scr[idx] + jnp.sum(e, axis=-1, keepdims=True)
    acc_scr[idx] = alpha * acc_scr[idx] + _dot(e.astype(BF16), v)
    m_scr[idx] = m_new


def _online_result(idx, l_scr, acc_scr):
    l = l_scr[idx]
    return acc_scr[idx] / jnp.where(l > 0, l, 1.0)


def _top_rank(score, n_blocks):
    blk = lax.broadcasted_iota(jnp.int32, score.shape, 1)
    rank = jnp.zeros(score.shape, F32)
    for n2 in range(n_blocks):
        col = score[:, n2:n2 + 1]
        rank = rank + jnp.where(col > score, 1.0, 0.0) + jnp.where((col == score) & (blk > n2), 1.0, 0.0)
    return rank


def _scalar_pick(idx, values):
    out = jnp.float32(values[-1])
    for i in range(len(values) - 2, -1, -1):
        out = jnp.where(idx == i, jnp.float32(values[i]), out)
    return out


def _alibi_slopes(n_heads):
    return [float(2.0 ** (-8.0 * (h + 1) / n_heads)) for h in range(n_heads)]


def _cmp_branch(q, kc, vc, cover, valid):
    imp = None
    outs = []
    for r in range(NSA_GROUP):
        s = _dot_nt(q[:, r * HEAD_DIM:(r + 1) * HEAD_DIM], kc) * SCALE
        s = jnp.where(valid, s, NEG_INF)
        e = jnp.where(valid, jnp.exp(s - jnp.max(s, axis=-1, keepdims=True)), 0.0)
        d = jnp.sum(e, axis=-1, keepdims=True)
        p = e / jnp.where(d > 0, d, 1.0)
        p_hi = p.astype(BF16)
        p_lo = (p - p_hi.astype(F32)).astype(BF16)
        outs.append(_dot(p_hi, vc))
        part = _dot(p_hi, cover) + _dot(p_lo, cover)
        imp = part if imp is None else imp + part
    return outs, imp


def _select_blocks(imp, tpos, n_sel):
    blk = lax.broadcasted_iota(jnp.int32, imp.shape, 1)
    cur = tpos // SEL_BLOCK
    forced = (blk == 0) | (blk == cur) | (blk == cur - 1)
    allowed = (blk * SEL_BLOCK <= tpos) & (blk < n_sel)
    score = jnp.where(allowed, imp + FORCE_BONUS * jnp.where(forced, 1.0, 0.0), NEG_INF)
    rank = _top_rank(score, n_sel)
    return jnp.where((rank < min(SEL_TOP_N, n_sel)) & (blk < n_sel), 1.0, 0.0)


def _kv_rows(ref, token0, n_tokens, slab, n_slab):
    return ref[pl.ds(token0 * n_slab + slab, n_tokens, stride=n_slab), :]


def _nsa_prompt_kernel(q_ref, gate_ref, kcvc_ref, sel_ref, win_ref, cover_ref, expand_ref, o_ref,
                       m_scr, l_scr, acc_scr, *, tq, seq):
    g = pl.program_id(1)
    qi = pl.program_id(2)
    nq = seq // tq
    n_sel = seq // SEL_BLOCK
    n_sub = seq // CMP_STRIDE
    n_win = WINDOW // tq
    tpos = qi * tq + lax.broadcasted_iota(jnp.int32, (tq, 1), 0)
    q = q_ref[...]

    n_idx = lax.broadcasted_iota(jnp.int32, (1, n_sub), 1)
    valid = (n_idx * CMP_STRIDE + (CMP_LEN - 1) <= tpos) & (n_idx < n_sub - 1)
    o_cmp, imp = _cmp_branch(q, kcvc_ref[0, 0, 0].astype(BF16), kcvc_ref[0, 1, 0].astype(BF16), cover_ref[...], valid)
    selected = _select_blocks(imp, tpos, n_sel).astype(BF16)

    m_scr[...] = jnp.full(m_scr.shape, NEG_INF, F32)
    l_scr[...] = jnp.zeros(l_scr.shape, F32)
    acc_scr[...] = jnp.zeros(acc_scr.shape, F32)
    slopes = _alibi_slopes(NSA_HEADS)
    slope = [jnp.where(g == 0, jnp.float32(slopes[r]), jnp.float32(slopes[NSA_GROUP + r])) for r in range(NSA_GROUP)]

    for j in range(nq):
        @pl.when(j <= qi)
        def _(j=j):
            kpos = j * tq + lax.broadcasted_iota(jnp.int32, (1, tq), 1)
            dist = tpos - kpos
            distf = dist.astype(F32)
            causal = dist >= 0
            sel_mask = (_dot(selected, expand_ref[:, j * tq:(j + 1) * tq]) > 0.5) & causal
            k = _kv_rows(sel_ref, j * tq, tq, g, 4).astype(BF16)
            v = _kv_rows(sel_ref, j * tq, tq, 2 + g, 4).astype(BF16)
            for r in range(NSA_GROUP):
                s = _dot_nt(q[:, r * HEAD_DIM:(r + 1) * HEAD_DIM], k) * SCALE - slope[r] * distf
                _online_update(r, s, sel_mask, v, m_scr, l_scr, acc_scr)

            @pl.when(j >= qi - n_win)
            def _():
                win_mask = causal & (dist < WINDOW)
                kw = _kv_rows(win_ref, j * tq, tq, g, 4).astype(BF16)
                vw = _kv_rows(win_ref, j * tq, tq, 2 + g, 4).astype(BF16)
                for r in range(NSA_GROUP):
                    s = _dot_nt(q[:, r * HEAD_DIM:(r + 1) * HEAD_DIM], kw) * SCALE - slope[r] * distf
                    _online_update(NSA_GROUP + r, s, win_mask, vw, m_scr, l_scr, acc_scr)

    gates = _sigmoid(gate_ref[...])
    for r in range(NSA_GROUP):
        o_sel = _online_result(r, l_scr, acc_scr)
        o_win = _online_result(NSA_GROUP + r, l_scr, acc_scr)
        o_ref[:, r * HEAD_DIM:(r + 1) * HEAD_DIM] = (gates[:, 3 * r:3 * r + 1] * o_cmp[r]
                                                      + gates[:, 3 * r + 1:3 * r + 2] * o_sel
                                                      + gates[:, 3 * r + 2:3 * r + 3] * o_win)


def nsa_constants(n_sub, n_keys):
    c_start = np.arange(n_sub)[:, None] * CMP_STRIDE
    s_start = np.arange(LANES)[None, :] * SEL_BLOCK
    cover = ((c_start < s_start + SEL_BLOCK) & (c_start + CMP_LEN > s_start)).astype(np.float32)
    cover[n_sub - 1:] = 0.0
    expand = (np.arange(n_keys)[None, :] // SEL_BLOCK == np.arange(LANES)[:, None]).astype(np.float32)
    return jnp.asarray(cover, BF16), jnp.asarray(expand, BF16)


def nsa_prompt(q_all, gates, kcvc, sel_kv, win_kv, batch, seq, tq=256):
    nq = seq // tq
    n_sub = seq // CMP_STRIDE
    cover, expand = nsa_constants(n_sub, seq)
    gw = NSA_GROUP * HEAD_DIM
    kv_spec = pl.BlockSpec((seq * 4, LANES), lambda b, g, i: (b, 0))
    return pl.pallas_call(
        functools.partial(_nsa_prompt_kernel, tq=tq, seq=seq),
        grid=(batch, NSA_KV_HEADS, nq),
        in_specs=[
            pl.BlockSpec((tq, gw), lambda b, g, i: (b * nq + i, g)),
            pl.BlockSpec((tq, LANES), lambda b, g, i: (b * nq + i, g)),
            pl.BlockSpec((1, 2, 1, n_sub, HEAD_DIM), lambda b, g, i: (b, 0, g, 0, 0)),
            kv_spec, kv_spec,
            pl.BlockSpec(cover.shape, lambda b, g, i: (0, 0)),
            pl.BlockSpec(expand.shape, lambda b, g, i: (0, 0)),
        ],
        out_specs=pl.BlockSpec((tq, gw), lambda b, g, i: (b * nq + i, g)),
        out_shape=jax.ShapeDtypeStruct((batch * seq, NSA_HEADS * HEAD_DIM), F32),
        scratch_shapes=[pltpu.VMEM((2 * NSA_GROUP, tq, 1), F32), pltpu.VMEM((2 * NSA_GROUP, tq, 1), F32),
                        pltpu.VMEM((2 * NSA_GROUP, tq, HEAD_DIM), F32)],
        compiler_params=_params(3, VMEM_LIMIT),
        name="nsa_prompt",
    )(q_all, gates, kcvc, sel_kv, win_kv, cover, expand)


def _moba_select(gate, n_past, n_blocks):
    blk = lax.broadcasted_iota(jnp.int32, gate.shape, 1)
    past = blk < n_past
    score = jnp.where(past, gate, NEG_INF)
    rank = _top_rank(score, n_blocks)
    return jnp.where(past & (rank < min(MOBA_TOP_K, n_blocks)), 1.0, 0.0)


def _moba_prompt_kernel(q_ref, kv_ref, o_ref, kmean_scr, m_scr, l_scr, acc_scr, *, seq):
    tq = MOBA_BLOCK
    n_slab = 2 * MOBA_HEADS
    h = pl.program_id(1)
    qi = pl.program_id(2)
    nb = seq // tq
    q = q_ref[...]
    tpos = qi * tq + lax.broadcasted_iota(jnp.int32, (tq, 1), 0)
    slope = _scalar_pick(h, _alibi_slopes(MOBA_HEADS))

    kmean_scr[...] = jnp.zeros(kmean_scr.shape, F32)
    for n in range(nb):
        kmean_scr[n:n + 1, :] = jnp.mean(_kv_rows(kv_ref, n * tq, tq, h, n_slab), axis=0, keepdims=True)
    gate = _dot_nt(q, kmean_scr[...].astype(BF16))
    blk = lax.broadcasted_iota(jnp.int32, gate.shape, 1)
    selected = jnp.where(blk == qi, 1.0, _moba_select(gate, qi, nb - 1))

    m_scr[...] = jnp.full(m_scr.shape, NEG_INF, F32)
    l_scr[...] = jnp.zeros(l_scr.shape, F32)
    acc_scr[...] = jnp.zeros(acc_scr.shape, F32)
    for j in range(nb):
        @pl.when(j <= qi)
        def _(j=j):
            kpos = j * tq + lax.broadcasted_iota(jnp.int32, (1, tq), 1)
            dist = tpos - kpos
            mask = (selected[:, j:j + 1] > 0.5) & (dist >= 0)
            k = _kv_rows(kv_ref, j * tq, tq, h, n_slab).astype(BF16)
            v = _kv_rows(kv_ref, j * tq, tq, MOBA_HEADS + h, n_slab).astype(BF16)
            s = _dot_nt(q, k) * SCALE - slope * dist.astype(F32)
            _online_update(0, s, mask, v, m_scr, l_scr, acc_scr)

    o_ref[...] = _online_result(0, l_scr, acc_scr)


def moba_prompt(q_all, q_col, moba_kv, batch, seq):
    tq = MOBA_BLOCK
    nq = seq // tq
    return pl.pallas_call(
        functools.partial(_moba_prompt_kernel, seq=seq),
        grid=(batch, MOBA_HEADS, nq),
        in_specs=[
            pl.BlockSpec((tq, HEAD_DIM), lambda b, h, i: (b * nq + i, q_col + h)),
            pl.BlockSpec((seq * 2 * MOBA_HEADS, LANES), lambda b, h, i: (b, 0)),
        ],
        out_specs=pl.BlockSpec((tq, HEAD_DIM), lambda b, h, i: (b * nq + i, h)),
        out_shape=jax.ShapeDtypeStruct((batch * seq, MOBA_HEADS * HEAD_DIM), F32),
        scratch_shapes=[pltpu.VMEM((LANES, HEAD_DIM), F32), pltpu.VMEM((1, tq, 1), F32), pltpu.VMEM((1, tq, 1), F32),
                        pltpu.VMEM((1, tq, HEAD_DIM), F32)],
        compiler_params=_params(3, VMEM_LIMIT),
        name="moba_prompt",
    )(q_all, moba_kv)


def _masked_attend(q, k, v, mask, bias):
    s = jnp.where(mask, _dot_nt(q, k) * SCALE - bias, NEG_INF)
    e = jnp.where(mask, jnp.exp(s - jnp.max(s, axis=-1, keepdims=True)), 0.0)
    d = jnp.sum(e, axis=-1, keepdims=True)
    return _dot(e.astype(BF16), v) / jnp.where(d > 0, d, 1.0)


def _padded_tile(rows, n):
    return jnp.concatenate([rows, jnp.zeros((n - rows.shape[0], rows.shape[1]), F32)], axis=0).astype(BF16)


def _nsa_sample_kernel(pt_ref, *refs, n_pages, page, t_new, n_win):
    page_refs = refs[:n_pages]
    (q_ref, gate_ref, kcvc_ref, new_sel_ref, cwin_ref, new_win_ref, cover_ref, expand_ref, o_ref,
     ks_scr, vs_scr, kw_scr, vw_scr) = refs[n_pages:]
    past = n_pages * page
    n_sub = past // CMP_STRIDE
    n_sel = -(-(past + t_new) // SEL_BLOCK)
    rows = NSA_GROUP * t_new
    slopes = _alibi_slopes(NSA_HEADS)
    row_t = lax.broadcasted_iota(jnp.int32, (rows, 1), 0)
    tpos = past + row_t % t_new
    tpos_q = past + lax.broadcasted_iota(jnp.int32, (t_new, 1), 0)
    n_idx = lax.broadcasted_iota(jnp.int32, (1, n_sub), 1)
    valid = (n_idx * CMP_STRIDE + (CMP_LEN - 1) <= tpos) & (n_idx < n_sub - 1)
    kpos = lax.broadcasted_iota(jnp.int32, (1, past + LANES), 1)
    dist = tpos - kpos
    kw_pos = past - n_win + lax.broadcasted_iota(jnp.int32, (1, n_win + LANES), 1)
    dist_w = tpos - kw_pos
    win_mask = (dist_w >= 0) & (dist_w < WINDOW) & (kw_pos >= 0)
    gates = _sigmoid(gate_ref[...])

    for g in range(NSA_KV_HEADS):
        q = jnp.concatenate([q_ref[:, (g * NSA_GROUP + r) * HEAD_DIM:(g * NSA_GROUP + r + 1) * HEAD_DIM]
                             for r in range(NSA_GROUP)], axis=0).astype(BF16)
        slope = jnp.float32(slopes[g * NSA_GROUP + NSA_GROUP - 1])
        for r in range(NSA_GROUP - 2, -1, -1):
            slope = jnp.where(row_t < (r + 1) * t_new, jnp.float32(slopes[g * NSA_GROUP + r]), slope)

        o_cmp, imp = _cmp_branch_rows(q, kcvc_ref[0, 0, g].astype(BF16), kcvc_ref[0, 1, g].astype(BF16),
                                      cover_ref[...], valid)
        imp_t = imp[0:t_new]
        for r in range(1, NSA_GROUP):
            imp_t = imp_t + imp[r * t_new:(r + 1) * t_new]
        sel_t = _select_blocks(imp_t, tpos_q, n_sel)
        selected = jnp.concatenate([sel_t] * NSA_GROUP, axis=0).astype(BF16)

        for p in range(n_pages):
            ks_scr[p * page:(p + 1) * page, :] = _kv_rows(page_refs[p], 0, page, g, 4).astype(BF16)
            vs_scr[p * page:(p + 1) * page, :] = _kv_rows(page_refs[p], 0, page, 2 + g, 4).astype(BF16)
        ks_scr[past:, :] = _padded_tile(_kv_rows(new_sel_ref, 0, t_new, g, 4), LANES)
        vs_scr[past:, :] = _padded_tile(_kv_rows(new_sel_ref, 0, t_new, 2 + g, 4), LANES)
        sel_mask = (_dot(selected, expand_ref[...]) > 0.5) & (dist >= 0)
        o_sel = _masked_attend(q, ks_scr[...], vs_scr[...], sel_mask, slope * dist.astype(F32))

        kw_scr[:n_win, :] = _kv_rows(cwin_ref, 0, n_win, g, 4).astype(BF16)
        vw_scr[:n_win, :] = _kv_rows(cwin_ref, 0, n_win, 2 + g, 4).astype(BF16)
        kw_scr[n_win:, :] = _padded_tile(_kv_rows(new_win_ref, 0, t_new, g, 4), LANES)
        vw_scr[n_win:, :] = _padded_tile(_kv_rows(new_win_ref, 0, t_new, 2 + g, 4), LANES)
        o_win = _masked_attend(q, kw_scr[...], vw_scr[...], win_mask, slope * dist_w.astype(F32))

        for r in range(NSA_GROUP):
            h = g * NSA_GROUP + r
            lane = g * LANES + 3 * r
            rs = slice(r * t_new, (r + 1) * t_new)
            o_ref[:, h * HEAD_DIM:(h + 1) * HEAD_DIM] = (gates[:, lane:lane + 1] * o_cmp[rs]
                                                          + gates[:, lane + 1:lane + 2] * o_sel[rs]
                                                          + gates[:, lane + 2:lane + 3] * o_win[rs])


def _cmp_branch_rows(q, kc, vc, cover, valid):
    s = jnp.where(valid, _dot_nt(q, kc) * SCALE, NEG_INF)
    e = jnp.where(valid, jnp.exp(s - jnp.max(s, axis=-1, keepdims=True)), 0.0)
    d = jnp.sum(e, axis=-1, keepdims=True)
    p = e / jnp.where(d > 0, d, 1.0)
    p_hi = p.astype(BF16)
    p_lo = (p - p_hi.astype(F32)).astype(BF16)
    return _dot(p_hi, vc), _dot(p_hi, cover) + _dot(p_lo, cover)


def nsa_sample(q_all, gates, kcvc, sel_pool, page_table, page, new_sel, cache_win, new_win, t_new):
    nb, n_pages = page_table.shape
    past = n_pages * page
    n_sub = past // CMP_STRIDE
    n_win = cache_win.shape[0] // (4 * nb)
    cover, expand = nsa_constants(n_sub, past + LANES)
    width = NSA_HEADS * HEAD_DIM
    const = lambda a: pl.BlockSpec(a.shape, lambda b, pt: (0,) * a.ndim)
    rows = lambda n: pl.BlockSpec((4 * n, LANES), lambda b, pt: (b, 0))
    grid_spec = pltpu.PrefetchScalarGridSpec(
        num_scalar_prefetch=1,
        grid=(nb,),
        in_specs=_page_specs(n_pages, 4 * page) + [
            pl.BlockSpec((t_new, width), lambda b, pt: (b, 0)),
            pl.BlockSpec((t_new, NSA_KV_HEADS * LANES), lambda b, pt: (b, 0)),
            pl.BlockSpec((1, 2, 2, n_sub, HEAD_DIM), lambda b, pt: (b, 0, 0, 0, 0)),
            rows(t_new), rows(n_win), rows(t_new),
            const(cover), const(expand)],
        out_specs=pl.BlockSpec((t_new, width), lambda b, pt: (b, 0)),
        scratch_shapes=[pltpu.VMEM((past + LANES, HEAD_DIM), BF16), pltpu.VMEM((past + LANES, HEAD_DIM), BF16),
                        pltpu.VMEM((n_win + LANES, HEAD_DIM), BF16), pltpu.VMEM((n_win + LANES, HEAD_DIM), BF16)],
    )
    return pl.pallas_call(
        functools.partial(_nsa_sample_kernel, n_pages=n_pages, page=page, t_new=t_new, n_win=n_win),
        grid_spec=grid_spec,
        out_shape=jax.ShapeDtypeStruct((nb * t_new, width), F32),
        compiler_params=_params(1, VMEM_LIMIT),
        name="nsa_sample",
    )(page_table.reshape(-1), *([sel_pool] * n_pages), q_all, gates, kcvc, new_sel, cache_win, new_win, cover, expand)


def _moba_sample_kernel(pt_ref, *refs, n_pages, page, t_new):
    page_refs = refs[:n_pages]
    q_ref, new_ref, expand_ref, o_ref, k_scr, v_scr, kmean_scr = refs[n_pages:]
    past = n_pages * page
    n_slab = 2 * MOBA_HEADS
    pages_per_block = MOBA_BLOCK // page
    n_blocks = -(-(past + t_new) // MOBA_BLOCK)
    slopes = _alibi_slopes(MOBA_HEADS)
    tpos = past + lax.broadcasted_iota(jnp.int32, (t_new, 1), 0)
    cur = tpos // MOBA_BLOCK
    kpos = lax.broadcasted_iota(jnp.int32, (1, past + LANES), 1)
    dist = tpos - kpos
    blk = lax.broadcasted_iota(jnp.int32, (t_new, LANES), 1)

    for h in range(MOBA_HEADS):
        q = q_ref[:, h * HEAD_DIM:(h + 1) * HEAD_DIM].astype(BF16)
        kmean_scr[...] = jnp.zeros(kmean_scr.shape, F32)
        for p in range(n_pages):
            k_page = _kv_rows(page_refs[p], 0, page, h, n_slab)
            k_scr[p * page:(p + 1) * page, :] = k_page.astype(BF16)
            v_scr[p * page:(p + 1) * page, :] = _kv_rows(page_refs[p], 0, page, MOBA_HEADS + h, n_slab).astype(BF16)
            n = p // pages_per_block
            kmean_scr[n:n + 1, :] = kmean_scr[n:n + 1, :] + jnp.sum(k_page, axis=0, keepdims=True) * (1.0 / MOBA_BLOCK)
        k_new = _kv_rows(new_ref, 0, t_new, h, n_slab)
        n = past // MOBA_BLOCK
        kmean_scr[n:n + 1, :] = kmean_scr[n:n + 1, :] + jnp.sum(k_new, axis=0, keepdims=True) * (1.0 / MOBA_BLOCK)
        k_scr[past:, :] = _padded_tile(k_new, LANES)
        v_scr[past:, :] = _padded_tile(_kv_rows(new_ref, 0, t_new, MOBA_HEADS + h, n_slab), LANES)

        gate = _dot_nt(q, kmean_scr[...].astype(BF16))
        selected = jnp.where(blk == cur, 1.0, _moba_select(gate, cur, n_blocks - 1)).astype(BF16)
        mask = (_dot(selected, expand_ref[...]) > 0.5) & (dist >= 0)
        o_ref[:, h * HEAD_DIM:(h + 1) * HEAD_DIM] = _masked_attend(q, k_scr[...], v_scr[...], mask,
                                                                    jnp.float32(slopes[h]) * dist.astype(F32))


def moba_sample(q_all, q_col, moba_pool, page_table, page, new_kv, t_new):
    nb, n_pages = page_table.shape
    past = n_pages * page
    width = MOBA_HEADS * HEAD_DIM
    n_slab = 2 * MOBA_HEADS
    expand = jnp.asarray(np.arange(past + LANES)[None, :] // MOBA_BLOCK == np.arange(LANES)[:, None], BF16)
    grid_spec = pltpu.PrefetchScalarGridSpec(
        num_scalar_prefetch=1,
        grid=(nb,),
        in_specs=_page_specs(n_pages, n_slab * page) + [
            pl.BlockSpec((t_new, width), lambda b, pt: (b, q_col)),
            pl.BlockSpec((t_new * n_slab, LANES), lambda b, pt: (b, 0)),
            pl.BlockSpec(expand.shape, lambda b, pt: (0, 0))],
        out_specs=pl.BlockSpec((t_new, width), lambda b, pt: (b, 0)),
        scratch_shapes=[pltpu.VMEM((past + LANES, HEAD_DIM), BF16), pltpu.VMEM((past + LANES, HEAD_DIM), BF16),
                        pltpu.VMEM((LANES, HEAD_DIM), F32)],
    )
    return pl.pallas_call(
        functools.partial(_moba_sample_kernel, n_pages=n_pages, page=page, t_new=t_new),
        grid_spec=grid_spec,
        out_shape=jax.ShapeDtypeStruct((nb * t_new, width), F32),
        compiler_params=_params(1, VMEM_LIMIT),
        name="moba_sample",
    )(page_table.reshape(-1), *([moba_pool] * n_pages), q_all, new_kv, expand)


def _cross_kernel(q_ref, kv_ref, o_ref, *, n_mem):
    h = pl.program_id(1)
    n_slab = 2 * MEM_HEADS
    k = _kv_rows(kv_ref, 0, n_mem, h, n_slab).astype(BF16)
    v = _kv_rows(kv_ref, 0, n_mem, MEM_HEADS + h, n_slab).astype(BF16)
    s = _dot_nt(q_ref[...].astype(BF16), k) * SCALE
    e = jnp.exp(s - jnp.max(s, axis=-1, keepdims=True))
    p = e / jnp.sum(e, axis=-1, keepdims=True)
    o_ref[...] = _dot(p.astype(BF16), v)


def cross_attend(q_all, q_col, mem_kv, batch, rows, tq):
    nq = rows // tq
    n_slab = 2 * MEM_HEADS
    n_mem = mem_kv.shape[0] // (batch * n_slab)
    return pl.pallas_call(
        functools.partial(_cross_kernel, n_mem=n_mem),
        grid=(batch, MEM_HEADS, nq),
        in_specs=[
            pl.BlockSpec((tq, HEAD_DIM), lambda b, h, i: (b * nq + i, q_col + h)),
            pl.BlockSpec((n_mem * n_slab, LANES), lambda b, h, i: (b, 0)),
        ],
        out_specs=pl.BlockSpec((tq, HEAD_DIM), lambda b, h, i: (b * nq + i, h)),
        out_shape=jax.ShapeDtypeStruct((batch * rows, MEM_HEADS * HEAD_DIM), F32),
        compiler_params=_params(3, VMEM_LIMIT),
        name="cross_attend",
    )(q_all, mem_kv)


def _merge_kernel(x_ref, on_ref, om_ref, oc_ref, z_ref, gl_ref, wn_ref, wm_ref, wc_ref, wo_ref, fg_ref, y_ref, mix_scr,
                  *, tn):
    d = x_ref.shape[1]
    wn_k, wm_k = wn_ref.shape[0], wm_ref.shape[0]
    z = z_ref[...]
    act = z * _sigmoid(z)
    un = (on_ref[...] * act[:, :wn_k]).astype(BF16)
    um = (om_ref[...] * act[:, wn_k:wn_k + wm_k]).astype(BF16)
    uc = (oc_ref[...] * act[:, wn_k + wm_k:]).astype(BF16)
    for c in range(d // tn):
        cols = slice(c * tn, (c + 1) * tn)
        mix = (_sigmoid(gl_ref[:, c * tn:(c + 1) * tn]) * _dot(un, wn_ref[:, cols])
               + _sigmoid(gl_ref[:, d + c * tn:d + (c + 1) * tn]) * _dot(um, wm_ref[:, cols])
               + _sigmoid(gl_ref[:, 2 * d + c * tn:2 * d + (c + 1) * tn]) * _dot(uc, wc_ref[:, cols]))
        mix_scr[:, cols] = mix.astype(BF16)
    y = x_ref[...] + _dot(mix_scr[...], wo_ref[...])
    inv = lax.rsqrt(jnp.mean(y * y, axis=-1, keepdims=True) + RMS_EPS)
    y_ref[...] = y * inv * fg_ref[...]


def merge_branches(x, o_nsa, o_moba, o_mem, z_all, gate_logits, w_nsa, w_moba, w_mem, w_out, final_g, tm=256, tn=512):
    m, d = x.shape
    tm = min(tm, m)
    row = lambda a: pl.BlockSpec((tm, a.shape[1]), lambda i: (i, 0))
    whole = lambda a: pl.BlockSpec(a.shape, lambda i: (0, 0), pipeline_mode=pl.Buffered(1))
    final_g = final_g.reshape(1, d)
    return pl.pallas_call(
        functools.partial(_merge_kernel, tn=tn),
        grid=(m // tm,),
        in_specs=[row(x), row(o_nsa), row(o_moba), row(o_mem), row(z_all), row(gate_logits),
                  whole(w_nsa), whole(w_moba), whole(w_mem), whole(w_out), whole(final_g)],
        out_specs=row(x),
        out_shape=jax.ShapeDtypeStruct((m, d), F32),
        scratch_shapes=[pltpu.VMEM((tm, d), BF16)],
        compiler_params=_params(1, VMEM_LIMIT),
        name="merge",
    )(x, o_nsa, o_moba, o_mem, z_all, gate_logits, w_nsa, w_moba, w_mem, w_out, final_g)


def _in_weights(w_in, d_model):
    splits = (NSA_HEADS * HEAD_DIM, 4 * HEAD_DIM, 4 * HEAD_DIM, 4 * HEAD_DIM, 3 * NSA_HEADS, NSA_HEADS * HEAD_DIM,
              MOBA_HEADS * HEAD_DIM, 2 * MOBA_HEADS * HEAD_DIM, MOBA_HEADS * HEAD_DIM, MEM_HEADS * HEAD_DIM,
              MEM_HEADS * HEAD_DIM, N_BRANCH * d_model)
    offs = np.cumsum((0,) + splits)
    assert offs[-1] == w_in.shape[1]
    (nsa_q, cmp_kv, sel_kv, win_kv, nsa_g, nsa_z, moba_q, moba_kv, moba_z, mem_q, mem_z, merge_g) = (
        w_in[:, offs[i]:offs[i + 1]] for i in range(len(splits)))
    per_group = 3 * NSA_GROUP
    gate_w = jnp.pad(nsa_g.reshape(-1, NSA_KV_HEADS, per_group), ((0, 0), (0, 0), (0, LANES - per_group)))
    cast = lambda w: w.astype(BF16)
    return dict(q=cast(jnp.concatenate([nsa_q, moba_q, mem_q], axis=1)), cmp=cast(cmp_kv), sel=cast(sel_kv),
                win=cast(win_kv), moba=cast(moba_kv), gate=cast(gate_w.reshape(-1, NSA_KV_HEADS * LANES)),
                z=cast(jnp.concatenate([nsa_z, moba_z, mem_z], axis=1)), merge=cast(merge_g))


def _in_projection(x, norm_g, w, q_dtype):
    tm = min(1024, x.shape[0] // 2)
    xn = rmsnorm_bf16(x, norm_g, 512)
    mm = lambda name, dtype: matmul(xn, w[name], dtype, tm, 512, "in_" + name)
    rows = lambda name: matmul_rows(xn, w[name], tm, "in_" + name)
    return dict(q=mm("q", q_dtype), cmp=rows("cmp"), sel=rows("sel"), win=rows("win"), moba=rows("moba"),
                gate=mm("gate", F32), z=mm("z", F32), merge=mm("merge", F32))


def kernel(x_prompt, x_sample, cache_cmp_kv, cache_sel_kv, cache_moba_kv, cache_win_kv, cache_mem_kv, page_table,
           mem_prompt, norm_g, w_in, phi_pos, phi_w1, phi_w2, mem_norm_g, w_mem_kv, w_br_nsa, w_br_moba, w_br_mem,
           w_out, final_norm_g):
    batch, seq, d_model = x_prompt.shape
    nb, t_new, _ = x_sample.shape
    n_mem = mem_prompt.shape[1]
    page = cache_cmp_kv.shape[2]
    n_win = cache_win_kv.shape[2]
    assert norm_g.shape[0] == 1, "one layer"
    assert seq % MOBA_BLOCK == 0 and (page_table.shape[1] * page) % MOBA_BLOCK == 0 and MOBA_BLOCK % page == 0
    l = 0
    q_moba_col = NSA_HEADS * HEAD_DIM // (MOBA_HEADS * HEAD_DIM)
    q_head_moba = NSA_HEADS
    q_head_mem = NSA_HEADS + MOBA_HEADS
    lane_rows = lambda a: a.reshape(-1, LANES)

    w = _in_weights(w_in[l], d_model)
    w1cat, pos8, w2 = compress_weights(phi_pos[l], phi_w1[l], phi_w2[l])
    br = [a[l].astype(BF16) for a in (w_br_nsa, w_br_moba, w_br_mem, w_out)]

    xp = x_prompt.reshape(batch * seq, d_model)
    hp = _in_projection(xp, norm_g[l], w, BF16)
    mem_n = rmsnorm_bf16(mem_prompt.reshape(batch * n_mem, d_model), mem_norm_g[l], 512)
    mem_kv_p = matmul_rows(mem_n, w_mem_kv[l].astype(BF16), 1024, "mem_kv")
    pages_p = seq // page
    identity_table = jnp.arange(batch * pages_p, dtype=jnp.int32).reshape(batch, pages_p)
    kcvc_p = compress_tokens(hp["cmp"], identity_table, page, w1cat, pos8, w2)
    o_nsa_p = nsa_prompt(hp["q"], hp["gate"], kcvc_p, hp["sel"], hp["win"], batch, seq)
    o_moba_p = moba_prompt(hp["q"], q_head_moba, hp["moba"], batch, seq)
    o_mem_p = cross_attend(hp["q"], q_head_mem, mem_kv_p, batch, seq, 256)
    y_prompt = merge_branches(xp, o_nsa_p, o_moba_p, o_mem_p, hp["z"], hp["merge"], *br, final_norm_g)

    xs = x_sample.reshape(nb * t_new, d_model)
    hs = _in_projection(xs, norm_g[l], w, F32)
    kcvc_s = compress_tokens(lane_rows(cache_cmp_kv[l]), page_table, page, w1cat, pos8, w2)
    o_nsa_s = nsa_sample(hs["q"], hs["gate"], kcvc_s, lane_rows(cache_sel_kv[l]), page_table, page, hs["sel"],
                         lane_rows(cache_win_kv[l]), hs["win"], t_new)
    o_moba_s = moba_sample(hs["q"], q_moba_col, lane_rows(cache_moba_kv[l]), page_table, page, hs["moba"], t_new)
    o_mem_s = cross_attend(hs["q"], q_head_mem, lane_rows(cache_mem_kv[l]), nb, t_new, t_new)
    y_sample = merge_branches(xs, o_nsa_s, o_moba_s, o_mem_s, hs["z"], hs["merge"], *br, final_norm_g)

    nsa_shape = lambda a, b_, t_: a.reshape(1, b_, t_, 2, NSA_KV_HEADS, HEAD_DIM)
    moba_shape = lambda a, b_, t_: a.reshape(1, b_, t_, 2, MOBA_HEADS, HEAD_DIM)
    keep_p = min(WINDOW, seq)
    win_all = jnp.concatenate([cache_win_kv[l], nsa_shape(hs["win"], nb, t_new)[0]], axis=1)
    keep_s = min(WINDOW, n_win + t_new)
    return (y_prompt.reshape(batch, seq, d_model), y_sample.reshape(nb, t_new, d_model),
            nsa_shape(hp["cmp"], batch, seq), nsa_shape(hp["sel"], batch, seq), moba_shape(hp["moba"], batch, seq),
            nsa_shape(hp["win"], batch, seq)[:, :, seq - keep_p:],
            mem_kv_p.reshape(1, batch, n_mem, 2, MEM_HEADS, HEAD_DIM),
            nsa_shape(hs["cmp"], nb, t_new), nsa_shape(hs["sel"], nb, t_new), moba_shape(hs["moba"], nb, t_new),
            win_all[None, :, n_win + t_new - keep_s:])
```

```python
import functools

import numpy as np
import jax
import jax.numpy as jnp
from jax import lax
from jax.experimental import pallas as pl
from jax.experimental.pallas import tpu as pltpu

HEAD_DIM = 128
NSA_HEADS = 8
NSA_KV_HEADS = 2
NSA_GROUP = NSA_HEADS // NSA_KV_HEADS
CMP_LEN = 32
CMP_STRIDE = 16
SEL_BLOCK = 64
SEL_TOP_N = 16
WINDOW = 512
FORCE_BONUS = 1.0e4
MOBA_HEADS = 4
MOBA_BLOCK = 256
MOBA_TOP_K = 3
MEM_HEADS = 4
N_BRANCH = 3
RMS_EPS = 1e-6
NEG_INF = -1e30
SCALE = HEAD_DIM ** -0.5
LOG2E = 1.4426950408889634
MASKED = -(2.0 ** 100)

LANES = 128
STAGE_PITCH = 24
VMEM_LIMIT = 56 * 1024 * 1024

F32 = jnp.float32
BF16 = jnp.bfloat16


def _params(n_axes, vmem=None):
    return pltpu.CompilerParams(dimension_semantics=("parallel",) * n_axes, vmem_limit_bytes=vmem)


def _dot(a, b):
    return jnp.dot(a, b, preferred_element_type=F32)


def _dot_nt(a, b):
    return lax.dot_general(a, b, (((1,), (1,)), ((), ())), preferred_element_type=F32)


def _sigmoid(x):
    return 1.0 / (1.0 + jnp.exp(-x))


def _rms_kernel(x_ref, g_ref, o_ref):
    x = x_ref[...]
    inv = lax.rsqrt(jnp.mean(x * x, axis=-1, keepdims=True) + RMS_EPS)
    o_ref[...] = (x * inv * g_ref[...]).astype(o_ref.dtype)


def rmsnorm_bf16(x, g, tm):
    n, d = x.shape
    return pl.pallas_call(
        _rms_kernel,
        grid=(n // tm,),
        in_specs=[pl.BlockSpec((tm, d), lambda i: (i, 0)), pl.BlockSpec((1, d), lambda i: (0, 0))],
        out_specs=pl.BlockSpec((tm, d), lambda i: (i, 0)),
        out_shape=jax.ShapeDtypeStruct((n, d), BF16),
        compiler_params=_params(1, VMEM_LIMIT),
        name="rmsnorm",
    )(x, g.reshape(1, d))


def _mm_kernel(x_ref, w_ref, o_ref):
    o_ref[...] = _dot(x_ref[...], w_ref[...]).astype(o_ref.dtype)


def matmul(x, w, out_dtype, tm, tn, name):
    m, k = x.shape
    n = w.shape[1]
    tm, tn = min(tm, m), min(tn, n)
    return pl.pallas_call(
        _mm_kernel,
        grid=(m // tm, n // tn),
        in_specs=[pl.BlockSpec((tm, k), lambda i, j: (i, 0)), pl.BlockSpec((k, tn), lambda i, j: (0, j))],
        out_specs=pl.BlockSpec((tm, tn), lambda i, j: (i, j)),
        out_shape=jax.ShapeDtypeStruct((m, n), out_dtype),
        compiler_params=_params(2, VMEM_LIMIT),
        name=name,
    )(x, w)


def _mm_rows_kernel(x_ref, w_ref, o_ref):
    tm = x_ref.shape[0]
    n_slab = w_ref.shape[1] // LANES
    res = _dot(x_ref[...], w_ref[...])
    for j in range(n_slab):
        o_ref[pl.ds(j, tm, stride=n_slab), :] = res[:, j * LANES:(j + 1) * LANES]


def matmul_rows(x, w, tm, name):
    m, k = x.shape
    n = w.shape[1]
    n_slab = n // LANES
    tm = min(tm, m)
    return pl.pallas_call(
        _mm_rows_kernel,
        grid=(m // tm,),
        in_specs=[pl.BlockSpec((tm, k), lambda i: (i, 0)), pl.BlockSpec((k, n), lambda i: (0, 0))],
        out_specs=pl.BlockSpec((tm * n_slab, LANES), lambda i: (i, 0)),
        out_shape=jax.ShapeDtypeStruct((m * n_slab, LANES), F32),
        compiler_params=_params(1, VMEM_LIMIT),
        name=name,
    )(x, w)


def _compress_kernel(pt_ref, *refs, n_pages, page):
    page_refs = refs[:n_pages]
    w1_ref, pos_ref, w2_ref, out_ref, lhs_scr, stage_scr = refs[n_pages:]
    sub_per_page = page // CMP_STRIDE
    n_sub = n_pages * sub_per_page
    for p in range(n_pages):
        for cg in range(4):
            stage = stage_scr.at[(p % 2) * 4 + cg]
            for j in range(sub_per_page):
                stage[j * STAGE_PITCH:j * STAGE_PITCH + CMP_STRIDE, :] = (
                    page_refs[p][pl.ds(4 * CMP_STRIDE * j + cg, CMP_STRIDE, stride=4), :])
            for s in range(CMP_STRIDE):
                lhs_scr[cg, p * sub_per_page:(p + 1) * sub_per_page, s * HEAD_DIM:(s + 1) * HEAD_DIM] = (
                    stage[pl.ds(s, sub_per_page, stride=STAGE_PITCH), :])
    row = lax.broadcasted_iota(jnp.int32, (n_sub, HEAD_DIM), 0)
    for c in range(2):
        w1 = w1_ref[c]
        bias = _dot(pos_ref[c], w1)
        b0 = bias[0:1, :HEAD_DIM]
        b1 = bias[1:2, HEAD_DIM:]
        for g in range(2):
            a = _dot(lhs_scr[c * 2 + g].astype(BF16), w1)
            a0 = a[:, :HEAD_DIM] + b0
            a1 = a[:, HEAD_DIM:] + b1
            hid = a0 + pltpu.roll(a1, shift=n_sub - 1, axis=0)
            act = hid * _sigmoid(hid)
            o = _dot(act.astype(BF16), w2_ref[c])
            out_ref[0, c, g] = jnp.where(row < n_sub - 1, o, 0.0)


def _page_specs(n_pages, rows):
    return [pl.BlockSpec((rows, LANES), functools.partial(lambda b, pt, p: (pt[b * n_pages + p], 0), p=p))
            for p in range(n_pages)]


def compress_tokens(pool, page_table, page, w1cat, pos8, w2):
    nb, n_pages = page_table.shape
    n_sub = n_pages * page // CMP_STRIDE
    const = lambda shape: pl.BlockSpec(shape, lambda b, pt: (0,) * len(shape))
    grid_spec = pltpu.PrefetchScalarGridSpec(
        num_scalar_prefetch=1,
        grid=(nb,),
        in_specs=_page_specs(n_pages, 4 * page) + [const(w1cat.shape), const(pos8.shape), const(w2.shape)],
        out_specs=pl.BlockSpec((1, 2, 2, n_sub, HEAD_DIM), lambda b, pt: (b, 0, 0, 0, 0)),
        scratch_shapes=[pltpu.VMEM((4, n_sub, CMP_STRIDE * HEAD_DIM), F32),
                        pltpu.VMEM((8, page // CMP_STRIDE * STAGE_PITCH, LANES), F32)],
    )
    return pl.pallas_call(
        functools.partial(_compress_kernel, n_pages=n_pages, page=page),
        grid_spec=grid_spec,
        out_shape=jax.ShapeDtypeStruct((nb, 2, 2, n_sub, HEAD_DIM), F32),
        compiler_params=_params(1, VMEM_LIMIT),
        name="compress",
    )(page_table.reshape(-1), *([pool] * n_pages), w1cat, pos8, w2)


def compress_weights(phi_pos, phi_w1, phi_w2):
    n_half = CMP_LEN // CMP_STRIDE
    w1 = phi_w1.reshape(2, n_half, CMP_STRIDE, HEAD_DIM, HEAD_DIM).transpose(0, 2, 3, 1, 4)
    w1cat = w1.reshape(2, CMP_STRIDE * HEAD_DIM, n_half * HEAD_DIM).astype(BF16)
    pos = phi_pos.reshape(2, n_half, CMP_STRIDE * HEAD_DIM)
    pos8 = jnp.pad(pos, ((0, 0), (0, 8 - n_half), (0, 0))).astype(BF16)
    return w1cat, pos8, phi_w2.astype(BF16)


def _biased_attend(q, k, v, bias, key_slope):
    s = _dot_nt(q, k) * (SCALE * LOG2E) + key_slope + bias
    e = jnp.exp2(s - jnp.max(s, axis=-1, keepdims=True))
    d = jnp.sum(e, axis=-1, keepdims=True)
    return _dot(e.astype(BF16), v) / jnp.where(d > 0, d, 1.0)


def _top_rank(score, n_blocks):
    blk = lax.broadcasted_iota(jnp.int32, score.shape, 1)
    rank = jnp.zeros(score.shape, F32)
    for n2 in range(n_blocks):
        col = score[:, n2:n2 + 1]
        rank = rank + jnp.where(col > score, 1.0, 0.0) + jnp.where((col == score) & (blk > n2), 1.0, 0.0)
    return rank


def _scalar_pick(idx, values):
    out = jnp.float32(values[-1])
    for i in range(len(values) - 2, -1, -1):
        out = jnp.where(idx == i, jnp.float32(values[i]), out)
    return out


def _alibi_slopes(n_heads):
    return [float(2.0 ** (-8.0 * (h + 1) / n_heads)) for h in range(n_heads)]


def _masked_softmax(s, valid, axis):
    s = jnp.where(valid, s, NEG_INF)
    e = jnp.where(valid, jnp.exp(s - jnp.max(s, axis=axis, keepdims=True)), 0.0)
    d = jnp.sum(e, axis=axis, keepdims=True)
    return e / jnp.where(d > 0, d, 1.0)


def _cmp_branch(q, kc, vc, cover_t, valid, valid_t):
    imp_t = None
    outs = []
    for r in range(NSA_GROUP):
        qr = q[:, r * HEAD_DIM:(r + 1) * HEAD_DIM]
        p = _masked_softmax(_dot_nt(qr, kc) * SCALE, valid, -1)
        outs.append(_dot(p.astype(BF16), vc))
        p_t = _masked_softmax(_dot_nt(kc, qr) * SCALE, valid_t, 0)
        p_hi = p_t.astype(BF16)
        p_lo = (p_t - p_hi.astype(F32)).astype(BF16)
        part = _dot(cover_t, p_hi) + _dot(cover_t, p_lo)
        imp_t = part if imp_t is None else imp_t + part
    return outs, imp_t


def _select_blocks_t(imp_t, tpos_row, n_sel):
    blk = lax.broadcasted_iota(jnp.int32, imp_t.shape, 0)
    cur = tpos_row // SEL_BLOCK
    forced = (blk == 0) | (blk == cur) | (blk == cur - 1)
    score = jnp.where(blk * SEL_BLOCK <= tpos_row, imp_t + FORCE_BONUS * jnp.where(forced, 1.0, 0.0), NEG_INF)
    rank = jnp.zeros(score.shape, F32)
    for n2 in range(n_sel):
        row = score[n2:n2 + 1, :]
        rank = rank + jnp.where(row > score, 1.0, 0.0) + jnp.where((row == score) & (blk > n2), 1.0, 0.0)
    return jnp.where(rank < min(SEL_TOP_N, n_sel), 1.0, 0.0)


def _select_blocks(imp, tpos, n_sel):
    blk = lax.broadcasted_iota(jnp.int32, imp.shape, 1)
    cur = tpos // SEL_BLOCK
    forced = (blk == 0) | (blk == cur) | (blk == cur - 1)
    allowed = (blk * SEL_BLOCK <= tpos) & (blk < n_sel)
    score = jnp.where(allowed, imp + FORCE_BONUS * jnp.where(forced, 1.0, 0.0), NEG_INF)
    rank = _top_rank(score, n_sel)
    return jnp.where((rank < min(SEL_TOP_N, n_sel)) & (blk < n_sel), 1.0, 0.0)


def _kv_rows(ref, token0, n_tokens, slab, n_slab):
    return ref[pl.ds(token0 * n_slab + slab, n_tokens, stride=n_slab), :]


def _causal_bias(bias, tq):
    w = bias.shape[1]
    row = lax.broadcasted_iota(jnp.int32, (tq, tq), 0)
    col = lax.broadcasted_iota(jnp.int32, (tq, tq), 1)
    diag = jnp.where(row >= col, bias[:, w - tq:], MASKED)
    return diag if w == tq else jnp.concatenate([bias[:, :w - tq], diag], axis=1)


def _nsa_prompt_kernel(q_ref, gate_ref, kcvc_ref, sel_ref, win_ref, cover_ref, expand_ref, wbias_ref, o_ref,
                       *, tq, seq):
    g = pl.program_id(1)
    qi = pl.program_id(2)
    nq = seq // tq
    n_sel = seq // SEL_BLOCK
    n_sub = seq // CMP_STRIDE
    n_win = WINDOW // tq
    tpos = qi * tq + lax.broadcasted_iota(jnp.int32, (tq, 1), 0)
    tpos_row = qi * tq + lax.broadcasted_iota(jnp.int32, (1, tq), 1)
    q = q_ref[...]

    n_idx = lax.broadcasted_iota(jnp.int32, (1, n_sub), 1)
    valid = (n_idx * CMP_STRIDE + (CMP_LEN - 1) <= tpos) & (n_idx < n_sub - 1)
    n_idx_t = lax.broadcasted_iota(jnp.int32, (n_sub, 1), 0)
    valid_t = (n_idx_t * CMP_STRIDE + (CMP_LEN - 1) <= tpos_row) & (n_idx_t < n_sub - 1)
    o_cmp, imp_t = _cmp_branch(q, kcvc_ref[0, 0, 0].astype(BF16), kcvc_ref[0, 1, 0].astype(BF16), cover_ref[...],
                               valid, valid_t)
    sel_t = _select_blocks_t(imp_t[:n_sel], tpos_row, n_sel)
    selected = jnp.concatenate([sel_t, jnp.zeros((LANES - n_sel, tq), F32)], axis=0).T.astype(BF16)

    slopes = _alibi_slopes(NSA_HEADS)
    slope = [jnp.where(g == 0, jnp.float32(slopes[r] * LOG2E), jnp.float32(slopes[NSA_GROUP + r] * LOG2E))
             for r in range(NSA_GROUP)]
    gates = _sigmoid(gate_ref[...])

    for c in range(nq):
        @pl.when(qi == c)
        def _(c=c):
            w = (c + 1) * tq
            bias = _causal_bias(_dot(selected, expand_ref[:, :w]) + MASKED, tq)
            k = _kv_rows(sel_ref, 0, w, g, 4).astype(BF16)
            v = _kv_rows(sel_ref, 0, w, 2 + g, 4).astype(BF16)
            kpos = lax.broadcasted_iota(jnp.int32, (1, w), 1).astype(F32)
            c0 = max(c - n_win, 0)
            ww = (c + 1 - c0) * tq
            kw = _kv_rows(win_ref, c0 * tq, ww, g, 4).astype(BF16)
            vw = _kv_rows(win_ref, c0 * tq, ww, 2 + g, 4).astype(BF16)
            wbias = wbias_ref[:, (n_win + 1) * tq - ww:]
            kpos_w = (c0 * tq + lax.broadcasted_iota(jnp.int32, (1, ww), 1)).astype(F32)
            for r in range(NSA_GROUP):
                qr = q[:, r * HEAD_DIM:(r + 1) * HEAD_DIM]
                o_sel = _biased_attend(qr, k, v, bias, slope[r] * kpos)
                o_win = _biased_attend(qr, kw, vw, wbias, slope[r] * kpos_w)
                o_ref[:, r * HEAD_DIM:(r + 1) * HEAD_DIM] = (gates[:, 3 * r:3 * r + 1] * o_cmp[r]
                                                              + gates[:, 3 * r + 1:3 * r + 2] * o_sel
                                                              + gates[:, 3 * r + 2:3 * r + 3] * o_win)


def nsa_prompt_constants(n_sub, n_keys, tq):
    c_start = np.arange(n_sub)[None, :] * CMP_STRIDE
    s_start = np.arange(LANES)[:, None] * SEL_BLOCK
    cover_t = ((c_start < s_start + SEL_BLOCK) & (c_start + CMP_LEN > s_start)).astype(np.float32)
    cover_t[:, n_sub - 1:] = 0.0
    expand = np.where(np.arange(n_keys)[None, :] // SEL_BLOCK == np.arange(LANES)[:, None], -MASKED, 0.0)
    dist = WINDOW + np.arange(tq)[:, None] - np.arange(WINDOW + tq)[None, :]
    wbias = np.where((dist >= 0) & (dist < WINDOW), 0.0, MASKED)
    return jnp.asarray(cover_t, BF16), jnp.asarray(expand, BF16), jnp.asarray(wbias, F32)


def nsa_constants(n_sub, n_keys):
    c_start = np.arange(n_sub)[:, None] * CMP_STRIDE
    s_start = np.arange(LANES)[None, :] * SEL_BLOCK
    cover = ((c_start < s_start + SEL_BLOCK) & (c_start + CMP_LEN > s_start)).astype(np.float32)
    cover[n_sub - 1:] = 0.0
    expand = (np.arange(n_keys)[None, :] // SEL_BLOCK == np.arange(LANES)[:, None]).astype(np.float32)
    return jnp.asarray(cover, BF16), jnp.asarray(expand, BF16)


def nsa_prompt(q_all, gates, kcvc, sel_kv, win_kv, batch, seq, tq=256):
    assert WINDOW % tq == 0 and seq % tq == 0 and (seq // SEL_BLOCK) % 8 == 0
    nq = seq // tq
    n_sub = seq // CMP_STRIDE
    cover_t, expand, wbias = nsa_prompt_constants(n_sub, seq, tq)
    gw = NSA_GROUP * HEAD_DIM
    kv_spec = pl.BlockSpec((seq * 4, LANES), lambda b, g, i: (b, 0))
    const = lambda a: pl.BlockSpec(a.shape, lambda b, g, i: (0, 0))
    return pl.pallas_call(
        functools.partial(_nsa_prompt_kernel, tq=tq, seq=seq),
        grid=(batch, NSA_KV_HEADS, nq),
        in_specs=[
            pl.BlockSpec((tq, gw), lambda b, g, i: (b * nq + i, g)),
            pl.BlockSpec((tq, LANES), lambda b, g, i: (b * nq + i, g)),
            pl.BlockSpec((1, 2, 1, n_sub, HEAD_DIM), lambda b, g, i: (b, 0, g, 0, 0)),
            kv_spec, kv_spec, const(cover_t), const(expand), const(wbias),
        ],
        out_specs=pl.BlockSpec((tq, gw), lambda b, g, i: (b * nq + i, g)),
        out_shape=jax.ShapeDtypeStruct((batch * seq, NSA_HEADS * HEAD_DIM), F32),
        compiler_params=_params(3, VMEM_LIMIT),
        name="nsa_prompt",
    )(q_all, gates, kcvc, sel_kv, win_kv, cover_t, expand, wbias)


def _moba_select(gate, n_past, n_blocks):
    blk = lax.broadcasted_iota(jnp.int32, gate.shape, 1)
    past = blk < n_past
    score = jnp.where(past, gate, NEG_INF)
    rank = _top_rank(score, n_blocks)
    return jnp.where(past & (rank < min(MOBA_TOP_K, n_blocks)), 1.0, 0.0)


def _moba_prompt_kernel(q_ref, kv_ref, expand_ref, o_ref, kmean_scr, *, seq):
    tq = MOBA_BLOCK
    n_slab = 2 * MOBA_HEADS
    h = pl.program_id(1)
    qi = pl.program_id(2)
    nb = seq // tq
    q = q_ref[...]
    slope = _scalar_pick(h, [s * LOG2E for s in _alibi_slopes(MOBA_HEADS)])

    @pl.when(qi == 0)
    def _():
        kmean_scr[...] = jnp.zeros(kmean_scr.shape, F32)
        for n in range(nb):
            kmean_scr[n:n + 1, :] = jnp.mean(_kv_rows(kv_ref, n * tq, tq, h, n_slab), axis=0, keepdims=True)

    gate = _dot_nt(q, kmean_scr[...].astype(BF16))
    blk = lax.broadcasted_iota(jnp.int32, gate.shape, 1)
    selected = jnp.where(blk == qi, 1.0, _moba_select(gate, qi, nb - 1)).astype(BF16)

    for c in range(nb):
        @pl.when(qi == c)
        def _(c=c):
            w = (c + 1) * tq
            bias = _causal_bias(_dot(selected, expand_ref[:, :w]) + MASKED, tq)
            k = _kv_rows(kv_ref, 0, w, h, n_slab).astype(BF16)
            v = _kv_rows(kv_ref, 0, w, MOBA_HEADS + h, n_slab).astype(BF16)
            kpos = lax.broadcasted_iota(jnp.int32, (1, w), 1).astype(F32)
            o_ref[...] = _biased_attend(q, k, v, bias, slope * kpos)


def moba_prompt(q_all, q_col, moba_kv, batch, seq):
    tq = MOBA_BLOCK
    nq = seq // tq
    expand = jnp.asarray(np.where(np.arange(seq)[None, :] // MOBA_BLOCK == np.arange(LANES)[:, None], -MASKED, 0.0),
                         BF16)
    return pl.pallas_call(
        functools.partial(_moba_prompt_kernel, seq=seq),
        grid=(batch, MOBA_HEADS, nq),
        in_specs=[
            pl.BlockSpec((tq, HEAD_DIM), lambda b, h, i: (b * nq + i, q_col + h)),
            pl.BlockSpec((seq * 2 * MOBA_HEADS, LANES), lambda b, h, i: (b, 0)),
            pl.BlockSpec(expand.shape, lambda b, h, i: (0, 0)),
        ],
        out_specs=pl.BlockSpec((tq, HEAD_DIM), lambda b, h, i: (b * nq + i, h)),
        out_shape=jax.ShapeDtypeStruct((batch * seq, MOBA_HEADS * HEAD_DIM), F32),
        scratch_shapes=[pltpu.VMEM((LANES, HEAD_DIM), F32)],
        compiler_params=pltpu.CompilerParams(dimension_semantics=("parallel", "parallel", "arbitrary"),
                                             vmem_limit_bytes=VMEM_LIMIT),
        name="moba_prompt",
    )(q_all, moba_kv, expand)


def _masked_attend(q, k, v, mask, bias):
    s = jnp.where(mask, _dot_nt(q, k) * SCALE - bias, NEG_INF)
    e = jnp.where(mask, jnp.exp(s - jnp.max(s, axis=-1, keepdims=True)), 0.0)
    d = jnp.sum(e, axis=-1, keepdims=True)
    return _dot(e.astype(BF16), v) / jnp.where(d > 0, d, 1.0)


def _padded_tile(rows, n):
    return jnp.concatenate([rows, jnp.zeros((n - rows.shape[0], rows.shape[1]), F32)], axis=0).astype(BF16)


def _nsa_sample_kernel(pt_ref, *refs, n_pages, page, t_new, n_win):
    page_refs = refs[:n_pages]
    (q_ref, gate_ref, kcvc_ref, new_sel_ref, cwin_ref, new_win_ref, cover_ref, expand_ref, o_ref,
     ks_all, vs_all, kw_all, vw_all) = refs[n_pages:]
    past = n_pages * page
    n_sub = past // CMP_STRIDE
    n_sel = -(-(past + t_new) // SEL_BLOCK)
    rows = NSA_GROUP * t_new
    slopes = _alibi_slopes(NSA_HEADS)
    row_t = lax.broadcasted_iota(jnp.int32, (rows, 1), 0)
    tpos = past + row_t % t_new
    tpos_q = past + lax.broadcasted_iota(jnp.int32, (t_new, 1), 0)
    n_idx = lax.broadcasted_iota(jnp.int32, (1, n_sub), 1)
    valid = (n_idx * CMP_STRIDE + (CMP_LEN - 1) <= tpos) & (n_idx < n_sub - 1)
    kpos = lax.broadcasted_iota(jnp.int32, (1, past + LANES), 1)
    dist = tpos - kpos
    kw_pos = past - n_win + lax.broadcasted_iota(jnp.int32, (1, n_win + LANES), 1)
    dist_w = tpos - kw_pos
    win_mask = (dist_w >= 0) & (dist_w < WINDOW) & (kw_pos >= 0)
    gates = _sigmoid(gate_ref[...])

    for g in range(NSA_KV_HEADS):
        ks_scr, vs_scr, kw_scr, vw_scr = ks_all.at[g], vs_all.at[g], kw_all.at[g], vw_all.at[g]
        q = jnp.concatenate([q_ref[:, (g * NSA_GROUP + r) * HEAD_DIM:(g * NSA_GROUP + r + 1) * HEAD_DIM]
                             for r in range(NSA_GROUP)], axis=0).astype(BF16)
        slope = jnp.float32(slopes[g * NSA_GROUP + NSA_GROUP - 1])
        for r in range(NSA_GROUP - 2, -1, -1):
            slope = jnp.where(row_t < (r + 1) * t_new, jnp.float32(slopes[g * NSA_GROUP + r]), slope)

        o_cmp, imp = _cmp_branch_rows(q, kcvc_ref[0, 0, g].astype(BF16), kcvc_ref[0, 1, g].astype(BF16),
                                      cover_ref[...], valid)
        imp_t = imp[0:t_new]
        for r in range(1, NSA_GROUP):
            imp_t = imp_t + imp[r * t_new:(r + 1) * t_new]
        sel_t = _select_blocks(imp_t, tpos_q, n_sel)
        selected = jnp.concatenate([sel_t] * NSA_GROUP, axis=0).astype(BF16)

        for p in range(n_pages):
            ks_scr[p * page:(p + 1) * page, :] = _kv_rows(page_refs[p], 0, page, g, 4).astype(BF16)
            vs_scr[p * page:(p + 1) * page, :] = _kv_rows(page_refs[p], 0, page, 2 + g, 4).astype(BF16)
        ks_scr[past:, :] = _padded_tile(_kv_rows(new_sel_ref, 0, t_new, g, 4), LANES)
        vs_scr[past:, :] = _padded_tile(_kv_rows(new_sel_ref, 0, t_new, 2 + g, 4), LANES)
        sel_mask = (_dot(selected, expand_ref[...]) > 0.5) & (dist >= 0)
        o_sel = _masked_attend(q, ks_scr[...], vs_scr[...], sel_mask, slope * dist.astype(F32))

        kw_scr[:n_win, :] = _kv_rows(cwin_ref, 0, n_win, g, 4).astype(BF16)
        vw_scr[:n_win, :] = _kv_rows(cwin_ref, 0, n_win, 2 + g, 4).astype(BF16)
        kw_scr[n_win:, :] = _padded_tile(_kv_rows(new_win_ref, 0, t_new, g, 4), LANES)
        vw_scr[n_win:, :] = _padded_tile(_kv_rows(new_win_ref, 0, t_new, 2 + g, 4), LANES)
        o_win = _masked_attend(q, kw_scr[...], vw_scr[...], win_mask, slope * dist_w.astype(F32))

        for r in range(NSA_GROUP):
            h = g * NSA_GROUP + r
            lane = g * LANES + 3 * r
            rs = slice(r * t_new, (r + 1) * t_new)
            o_ref[:, h * HEAD_DIM:(h + 1) * HEAD_DIM] = (gates[:, lane:lane + 1] * o_cmp[rs]
                                                          + gates[:, lane + 1:lane + 2] * o_sel[rs]
                                                          + gates[:, lane + 2:lane + 3] * o_win[rs])


def _cmp_branch_rows(q, kc, vc, cover, valid):
    s = jnp.where(valid, _dot_nt(q, kc) * SCALE, NEG_INF)
    e = jnp.where(valid, jnp.exp(s - jnp.max(s, axis=-1, keepdims=True)), 0.0)
    d = jnp.sum(e, axis=-1, keepdims=True)
    p = e / jnp.where(d > 0, d, 1.0)
    p_hi = p.astype(BF16)
    p_lo = (p - p_hi.astype(F32)).astype(BF16)
    return _dot(p_hi, vc), _dot(p_hi, cover) + _dot(p_lo, cover)


def nsa_sample(q_all, gates, kcvc, sel_pool, page_table, page, new_sel, cache_win, new_win, t_new):
    nb, n_pages = page_table.shape
    past = n_pages * page
    n_sub = past // CMP_STRIDE
    n_win = cache_win.shape[0] // (4 * nb)
    cover, expand = nsa_constants(n_sub, past + LANES)
    width = NSA_HEADS * HEAD_DIM
    const = lambda a: pl.BlockSpec(a.shape, lambda b, pt: (0,) * a.ndim)
    rows = lambda n: pl.BlockSpec((4 * n, LANES), lambda b, pt: (b, 0))
    grid_spec = pltpu.PrefetchScalarGridSpec(
        num_scalar_prefetch=1,
        grid=(nb,),
        in_specs=_page_specs(n_pages, 4 * page) + [
            pl.BlockSpec((t_new, width), lambda b, pt: (b, 0)),
            pl.BlockSpec((t_new, NSA_KV_HEADS * LANES), lambda b, pt: (b, 0)),
            pl.BlockSpec((1, 2, 2, n_sub, HEAD_DIM), lambda b, pt: (b, 0, 0, 0, 0)),
            rows(t_new), rows(n_win), rows(t_new),
            const(cover), const(expand)],
        out_specs=pl.BlockSpec((t_new, width), lambda b, pt: (b, 0)),
        scratch_shapes=[pltpu.VMEM((NSA_KV_HEADS, past + LANES, HEAD_DIM), BF16),
                        pltpu.VMEM((NSA_KV_HEADS, past + LANES, HEAD_DIM), BF16),
                        pltpu.VMEM((NSA_KV_HEADS, n_win + LANES, HEAD_DIM), BF16),
                        pltpu.VMEM((NSA_KV_HEADS, n_win + LANES, HEAD_DIM), BF16)],
    )
    return pl.pallas_call(
        functools.partial(_nsa_sample_kernel, n_pages=n_pages, page=page, t_new=t_new, n_win=n_win),
        grid_spec=grid_spec,
        out_shape=jax.ShapeDtypeStruct((nb * t_new, width), F32),
        compiler_params=_params(1, VMEM_LIMIT),
        name="nsa_sample",
    )(page_table.reshape(-1), *([sel_pool] * n_pages), q_all, gates, kcvc, new_sel, cache_win, new_win, cover, expand)


def _moba_sample_kernel(pt_ref, *refs, n_pages, page, t_new):
    page_refs = refs[:n_pages]
    q_ref, new_ref, expand_ref, o_ref, k_all, v_all, kmean_all = refs[n_pages:]
    past = n_pages * page
    n_slab = 2 * MOBA_HEADS
    pages_per_block = MOBA_BLOCK // page
    n_blocks = -(-(past + t_new) // MOBA_BLOCK)
    slopes = _alibi_slopes(MOBA_HEADS)
    tpos = past + lax.broadcasted_iota(jnp.int32, (t_new, 1), 0)
    cur = tpos // MOBA_BLOCK
    kpos = lax.broadcasted_iota(jnp.int32, (1, past + LANES), 1)
    dist = tpos - kpos
    blk = lax.broadcasted_iota(jnp.int32, (t_new, LANES), 1)

    for h in range(MOBA_HEADS):
        k_scr, v_scr, kmean_scr = k_all.at[h], v_all.at[h], kmean_all.at[h]
        q = q_ref[:, h * HEAD_DIM:(h + 1) * HEAD_DIM].astype(BF16)
        kmean_scr[...] = jnp.zeros(kmean_scr.shape, F32)
        for p in range(n_pages):
            k_page = _kv_rows(page_refs[p], 0, page, h, n_slab)
            k_scr[p * page:(p + 1) * page, :] = k_page.astype(BF16)
            v_scr[p * page:(p + 1) * page, :] = _kv_rows(page_refs[p], 0, page, MOBA_HEADS + h, n_slab).astype(BF16)
            n = p // pages_per_block
            kmean_scr[n:n + 1, :] = kmean_scr[n:n + 1, :] + jnp.sum(k_page, axis=0, keepdims=True) * (1.0 / MOBA_BLOCK)
        k_new = _kv_rows(new_ref, 0, t_new, h, n_slab)
        n = past // MOBA_BLOCK
        kmean_scr[n:n + 1, :] = kmean_scr[n:n + 1, :] + jnp.sum(k_new, axis=0, keepdims=True) * (1.0 / MOBA_BLOCK)
        k_scr[past:, :] = _padded_tile(k_new, LANES)
        v_scr[past:, :] = _padded_tile(_kv_rows(new_ref, 0, t_new, MOBA_HEADS + h, n_slab), LANES)

        gate = _dot_nt(q, kmean_scr[...].astype(BF16))
        selected = jnp.where(blk == cur, 1.0, _moba_select(gate, cur, n_blocks - 1)).astype(BF16)
        mask = (_dot(selected, expand_ref[...]) > 0.5) & (dist >= 0)
        o_ref[:, h * HEAD_DIM:(h + 1) * HEAD_DIM] = _masked_attend(q, k_scr[...], v_scr[...], mask,
                                                                    jnp.float32(slopes[h]) * dist.astype(F32))


def moba_sample(q_all, q_col, moba_pool, page_table, page, new_kv, t_new):
    nb, n_pages = page_table.shape
    past = n_pages * page
    width = MOBA_HEADS * HEAD_DIM
    n_slab = 2 * MOBA_HEADS
    expand = jnp.asarray(np.arange(past + LANES)[None, :] // MOBA_BLOCK == np.arange(LANES)[:, None], BF16)
    grid_spec = pltpu.PrefetchScalarGridSpec(
        num_scalar_prefetch=1,
        grid=(nb,),
        in_specs=_page_specs(n_pages, n_slab * page) + [
            pl.BlockSpec((t_new, width), lambda b, pt: (b, q_col)),
            pl.BlockSpec((t_new * n_slab, LANES), lambda b, pt: (b, 0)),
            pl.BlockSpec(expand.shape, lambda b, pt: (0, 0))],
        out_specs=pl.BlockSpec((t_new, width), lambda b, pt: (b, 0)),
        scratch_shapes=[pltpu.VMEM((MOBA_HEADS, past + LANES, HEAD_DIM), BF16),
                        pltpu.VMEM((MOBA_HEADS, past + LANES, HEAD_DIM), BF16),
                        pltpu.VMEM((MOBA_HEADS, LANES, HEAD_DIM), F32)],
    )
    return pl.pallas_call(
        functools.partial(_moba_sample_kernel, n_pages=n_pages, page=page, t_new=t_new),
        grid_spec=grid_spec,
        out_shape=jax.ShapeDtypeStruct((nb * t_new, width), F32),
        compiler_params=_params(1, VMEM_LIMIT),
        name="moba_sample",
    )(page_table.reshape(-1), *([moba_pool] * n_pages), q_all, new_kv, expand)


def _cross_kernel(q_ref, kv_ref, o_ref, *, n_mem, tq, bb):
    n_slab = 2 * MEM_HEADS
    for i in range(bb):
        for h in range(MEM_HEADS):
            k = _kv_rows(kv_ref, i * n_mem, n_mem, h, n_slab).astype(BF16)
            v = _kv_rows(kv_ref, i * n_mem, n_mem, MEM_HEADS + h, n_slab).astype(BF16)
            q = q_ref[i * tq:(i + 1) * tq, h * HEAD_DIM:(h + 1) * HEAD_DIM].astype(BF16)
            s = _dot_nt(q, k) * SCALE
            e = jnp.exp(s - jnp.max(s, axis=-1, keepdims=True))
            p = e / jnp.sum(e, axis=-1, keepdims=True)
            o_ref[i * tq:(i + 1) * tq, h * HEAD_DIM:(h + 1) * HEAD_DIM] = _dot(p.astype(BF16), v)


def cross_attend(q_all, q_col, mem_kv, batch, rows, tq, bb):
    nq = rows // tq
    assert batch % bb == 0 and (bb == 1 or nq == 1)
    n_slab = 2 * MEM_HEADS
    width = MEM_HEADS * HEAD_DIM
    n_mem = mem_kv.shape[0] // (batch * n_slab)
    return pl.pallas_call(
        functools.partial(_cross_kernel, n_mem=n_mem, tq=tq, bb=bb),
        grid=(batch // bb, nq),
        in_specs=[
            pl.BlockSpec((bb * tq, width), lambda b, i: (b * nq + i, q_col)),
            pl.BlockSpec((bb * n_mem * n_slab, LANES), lambda b, i: (b, 0)),
        ],
        out_specs=pl.BlockSpec((bb * tq, width), lambda b, i: (b * nq + i, 0)),
        out_shape=jax.ShapeDtypeStruct((batch * rows, width), F32),
        compiler_params=_params(2, VMEM_LIMIT),
        name="cross_attend",
    )(q_all, mem_kv)


def _merge_kernel(x_ref, on_ref, om_ref, oc_ref, z_ref, gl_ref, wn_ref, wm_ref, wc_ref, wo_ref, fg_ref, y_ref, mix_scr,
                  *, tn):
    d = x_ref.shape[1]
    wn_k, wm_k = wn_ref.shape[0], wm_ref.shape[0]
    z = z_ref[...]
    act = z * _sigmoid(z)
    un = (on_ref[...] * act[:, :wn_k]).astype(BF16)
    um = (om_ref[...] * act[:, wn_k:wn_k + wm_k]).astype(BF16)
    uc = (oc_ref[...] * act[:, wn_k + wm_k:]).astype(BF16)
    for c in range(d // tn):
        cols = slice(c * tn, (c + 1) * tn)
        mix = (_sigmoid(gl_ref[:, c * tn:(c + 1) * tn]) * _dot(un, wn_ref[:, cols])
               + _sigmoid(gl_ref[:, d + c * tn:d + (c + 1) * tn]) * _dot(um, wm_ref[:, cols])
               + _sigmoid(gl_ref[:, 2 * d + c * tn:2 * d + (c + 1) * tn]) * _dot(uc, wc_ref[:, cols]))
        mix_scr[:, cols] = mix.astype(BF16)
    y = x_ref[...] + _dot(mix_scr[...], wo_ref[...])
    inv = lax.rsqrt(jnp.mean(y * y, axis=-1, keepdims=True) + RMS_EPS)
    y_ref[...] = y * inv * fg_ref[...]


def merge_branches(x, o_nsa, o_moba, o_mem, z_all, gate_logits, w_nsa, w_moba, w_mem, w_out, final_g, tm=256, tn=512):
    m, d = x.shape
    tm = min(tm, m)
    row = lambda a: pl.BlockSpec((tm, a.shape[1]), lambda i: (i, 0))
    whole = lambda a: pl.BlockSpec(a.shape, lambda i: (0, 0), pipeline_mode=pl.Buffered(1))
    final_g = final_g.reshape(1, d)
    return pl.pallas_call(
        functools.partial(_merge_kernel, tn=tn),
        grid=(m // tm,),
        in_specs=[row(x), row(o_nsa), row(o_moba), row(o_mem), row(z_all), row(gate_logits),
                  whole(w_nsa), whole(w_moba), whole(w_mem), whole(w_out), whole(final_g)],
        out_specs=row(x),
        out_shape=jax.ShapeDtypeStruct((m, d), F32),
        scratch_shapes=[pltpu.VMEM((tm, d), BF16)],
        compiler_params=_params(1, VMEM_LIMIT),
        name="merge",
    )(x, o_nsa, o_moba, o_mem, z_all, gate_logits, w_nsa, w_moba, w_mem, w_out, final_g)


def _in_weights(w_in, d_model):
    splits = (NSA_HEADS * HEAD_DIM, 4 * HEAD_DIM, 4 * HEAD_DIM, 4 * HEAD_DIM, 3 * NSA_HEADS, NSA_HEADS * HEAD_DIM,
              MOBA_HEADS * HEAD_DIM, 2 * MOBA_HEADS * HEAD_DIM, MOBA_HEADS * HEAD_DIM, MEM_HEADS * HEAD_DIM,
              MEM_HEADS * HEAD_DIM, N_BRANCH * d_model)
    offs = np.cumsum((0,) + splits)
    assert offs[-1] == w_in.shape[1]
    (nsa_q, cmp_kv, sel_kv, win_kv, nsa_g, nsa_z, moba_q, moba_kv, moba_z, mem_q, mem_z, merge_g) = (
        w_in[:, offs[i]:offs[i + 1]] for i in range(len(splits)))
    per_group = 3 * NSA_GROUP
    gate_w = jnp.pad(nsa_g.reshape(-1, NSA_KV_HEADS, per_group), ((0, 0), (0, 0), (0, LANES - per_group)))
    cast = lambda w: w.astype(BF16)
    return dict(q=cast(jnp.concatenate([nsa_q, moba_q, mem_q], axis=1)), cmp=cast(cmp_kv), sel=cast(sel_kv),
                win=cast(win_kv), moba=cast(moba_kv), gate=cast(gate_w.reshape(-1, NSA_KV_HEADS * LANES)),
                z=cast(jnp.concatenate([nsa_z, moba_z, mem_z], axis=1)), merge=cast(merge_g))


def _in_projection(x, norm_g, w, q_dtype):
    tm = min(1024, x.shape[0] // 2)
    xn = rmsnorm_bf16(x, norm_g, 512)
    mm = lambda name, dtype: matmul(xn, w[name], dtype, tm, 512, "in_" + name)
    rows = lambda name: matmul_rows(xn, w[name], tm, "in_" + name)
    return dict(q=mm("q", q_dtype), cmp=rows("cmp"), sel=rows("sel"), win=rows("win"), moba=rows("moba"),
                gate=mm("gate", F32), z=mm("z", F32), merge=mm("merge", F32))


def kernel(x_prompt, x_sample, cache_cmp_kv, cache_sel_kv, cache_moba_kv, cache_win_kv, cache_mem_kv, page_table,
           mem_prompt, norm_g, w_in, phi_pos, phi_w1, phi_w2, mem_norm_g, w_mem_kv, w_br_nsa, w_br_moba, w_br_mem,
           w_out, final_norm_g):
    batch, seq, d_model = x_prompt.shape
    nb, t_new, _ = x_sample.shape
    n_mem = mem_prompt.shape[1]
    page = cache_cmp_kv.shape[2]
    n_win = cache_win_kv.shape[2]
    assert norm_g.shape[0] == 1, "one layer"
    assert seq % MOBA_BLOCK == 0 and (page_table.shape[1] * page) % MOBA_BLOCK == 0 and MOBA_BLOCK % page == 0
    l = 0
    q_moba_col = NSA_HEADS * HEAD_DIM // (MOBA_HEADS * HEAD_DIM)
    q_head_moba = NSA_HEADS
    q_mem_col = (NSA_HEADS + MOBA_HEADS) * HEAD_DIM // (MEM_HEADS * HEAD_DIM)
    lane_rows = lambda a: a.reshape(-1, LANES)

    w = _in_weights(w_in[l], d_model)
    w1cat, pos8, w2 = compress_weights(phi_pos[l], phi_w1[l], phi_w2[l])
    br = [a[l].astype(BF16) for a in (w_br_nsa, w_br_moba, w_br_mem, w_out)]

    xp = x_prompt.reshape(batch * seq, d_model)
    hp = _in_projection(xp, norm_g[l], w, BF16)
    mem_n = rmsnorm_bf16(mem_prompt.reshape(batch * n_mem, d_model), mem_norm_g[l], 512)
    mem_kv_p = matmul_rows(mem_n, w_mem_kv[l].astype(BF16), 1024, "mem_kv")
    pages_p = seq // page
    identity_table = jnp.arange(batch * pages_p, dtype=jnp.int32).reshape(batch, pages_p)
    kcvc_p = compress_tokens(hp["cmp"], identity_table, page, w1cat, pos8, w2)
    o_nsa_p = nsa_prompt(hp["q"], hp["gate"], kcvc_p, hp["sel"], hp["win"], batch, seq)
    o_moba_p = moba_prompt(hp["q"], q_head_moba, hp["moba"], batch, seq)
    o_mem_p = cross_attend(hp["q"], q_mem_col, mem_kv_p, batch, seq, min(seq, 1024), 1)
    y_prompt = merge_branches(xp, o_nsa_p, o_moba_p, o_mem_p, hp["z"], hp["merge"], *br, final_norm_g)

    xs = x_sample.reshape(nb * t_new, d_model)
    hs = _in_projection(xs, norm_g[l], w, F32)
    kcvc_s = compress_tokens(lane_rows(cache_cmp_kv[l]), page_table, page, w1cat, pos8, w2)
    o_nsa_s = nsa_sample(hs["q"], hs["gate"], kcvc_s, lane_rows(cache_sel_kv[l]), page_table, page, hs["sel"],
                         lane_rows(cache_win_kv[l]), hs["win"], t_new)
    o_moba_s = moba_sample(hs["q"], q_moba_col, lane_rows(cache_moba_kv[l]), page_table, page, hs["moba"], t_new)
    o_mem_s = cross_attend(hs["q"], q_mem_col, lane_rows(cache_mem_kv[l]), nb, t_new, t_new, 4 if nb % 4 == 0 else 1)
    y_sample = merge_branches(xs, o_nsa_s, o_moba_s, o_mem_s, hs["z"], hs["merge"], *br, final_norm_g)

    nsa_shape = lambda a, b_, t_: a.reshape(1, b_, t_, 2, NSA_KV_HEADS, HEAD_DIM)
    moba_shape = lambda a, b_, t_: a.reshape(1, b_, t_, 2, MOBA_HEADS, HEAD_DIM)
    keep_p = min(WINDOW, seq)
    win_all = jnp.concatenate([cache_win_kv[l], nsa_shape(hs["win"], nb, t_new)[0]], axis=1)
    keep_s = min(WINDOW, n_win + t_new)
    return (y_prompt.reshape(batch, seq, d_model), y_sample.reshape(nb, t_new, d_model),
            nsa_shape(hp["cmp"], batch, seq), nsa_shape(hp["sel"], batch, seq), moba_shape(hp["moba"], batch, seq),
            nsa_shape(hp["win"], batch, seq)[:, :, seq - keep_p:],
            mem_kv_p.reshape(1, batch, n_mem, 2, MEM_HEADS, HEAD_DIM),
            nsa_shape(hs["cmp"], nb, t_new), nsa_shape(hs["sel"], nb, t_new), moba_shape(hs["moba"], nb, t_new),
            win_all[None, :, n_win + t_new - keep_s:])
```

```python
import functools

import numpy as np
import jax
import jax.numpy as jnp
from jax import lax
from jax.experimental import pallas as pl
from jax.experimental.pallas import tpu as pltpu

HEAD_DIM = 128
NSA_HEADS = 8
NSA_KV_HEADS = 2
NSA_GROUP = NSA_HEADS // NSA_KV_HEADS
CMP_LEN = 32
CMP_STRIDE = 16
SEL_BLOCK = 64
SEL_TOP_N = 16
WINDOW = 512
FORCE_BONUS = 1.0e4
MOBA_HEADS = 4
MOBA_BLOCK = 256
MOBA_TOP_K = 3
MEM_HEADS = 4
N_BRANCH = 3
RMS_EPS = 1e-6
NEG_INF = -1e30
SCALE = HEAD_DIM ** -0.5
LOG2E = 1.4426950408889634
MASKED = -(2.0 ** 100)

LANES = 128
STAGE_PITCH = 24
VMEM_LIMIT = 56 * 1024 * 1024

F32 = jnp.float32
BF16 = jnp.bfloat16


def _params(n_axes, vmem=None):
    return pltpu.CompilerParams(dimension_semantics=("parallel",) * n_axes, vmem_limit_bytes=vmem)


def _dot(a, b):
    return jnp.dot(a, b, preferred_element_type=F32)


def _dot_nt(a, b):
    return lax.dot_general(a, b, (((1,), (1,)), ((), ())), preferred_element_type=F32)


def _sigmoid(x):
    return 1.0 / (1.0 + jnp.exp(-x))


def _rms_kernel(x_ref, g_ref, o_ref):
    x = x_ref[...]
    inv = lax.rsqrt(jnp.mean(x * x, axis=-1, keepdims=True) + RMS_EPS)
    o_ref[...] = (x * inv * g_ref[...]).astype(o_ref.dtype)


def rmsnorm_bf16(x, g, tm):
    n, d = x.shape
    return pl.pallas_call(
        _rms_kernel,
        grid=(n // tm,),
        in_specs=[pl.BlockSpec((tm, d), lambda i: (i, 0)), pl.BlockSpec((1, d), lambda i: (0, 0))],
        out_specs=pl.BlockSpec((tm, d), lambda i: (i, 0)),
        out_shape=jax.ShapeDtypeStruct((n, d), BF16),
        compiler_params=_params(1, VMEM_LIMIT),
        name="rmsnorm",
    )(x, g.reshape(1, d))


def _mm_kernel(x_ref, w_ref, o_ref):
    o_ref[...] = _dot(x_ref[...], w_ref[...]).astype(o_ref.dtype)


def matmul(x, w, out_dtype, tm, tn, name):
    m, k = x.shape
    n = w.shape[1]
    tm, tn = min(tm, m), min(tn, n)
    return pl.pallas_call(
        _mm_kernel,
        grid=(m // tm, n // tn),
        in_specs=[pl.BlockSpec((tm, k), lambda i, j: (i, 0)), pl.BlockSpec((k, tn), lambda i, j: (0, j))],
        out_specs=pl.BlockSpec((tm, tn), lambda i, j: (i, j)),
        out_shape=jax.ShapeDtypeStruct((m, n), out_dtype),
        compiler_params=_params(2, VMEM_LIMIT),
        name=name,
    )(x, w)


def _mm_rows_kernel(x_ref, w_ref, o_ref):
    tm = x_ref.shape[0]
    n_slab = w_ref.shape[1] // LANES
    res = _dot(x_ref[...], w_ref[...])
    for j in range(n_slab):
        o_ref[pl.ds(j, tm, stride=n_slab), :] = res[:, j * LANES:(j + 1) * LANES]


def matmul_rows(x, w, tm, name):
    m, k = x.shape
    n = w.shape[1]
    n_slab = n // LANES
    tm = min(tm, m)
    return pl.pallas_call(
        _mm_rows_kernel,
        grid=(m // tm,),
        in_specs=[pl.BlockSpec((tm, k), lambda i: (i, 0)), pl.BlockSpec((k, n), lambda i: (0, 0))],
        out_specs=pl.BlockSpec((tm * n_slab, LANES), lambda i: (i, 0)),
        out_shape=jax.ShapeDtypeStruct((m * n_slab, LANES), F32),
        compiler_params=_params(1, VMEM_LIMIT),
        name=name,
    )(x, w)


def _compress_kernel(pt_ref, *refs, n_pages, page):
    page_refs = refs[:n_pages]
    w1_ref, pos_ref, w2_ref, out_ref, lhs_scr, stage_scr = refs[n_pages:]
    sub_per_page = page // CMP_STRIDE
    n_sub = n_pages * sub_per_page
    for p in range(n_pages):
        for cg in range(4):
            stage = stage_scr.at[(p % 2) * 4 + cg]
            for j in range(sub_per_page):
                stage[j * STAGE_PITCH:j * STAGE_PITCH + CMP_STRIDE, :] = (
                    page_refs[p][pl.ds(4 * CMP_STRIDE * j + cg, CMP_STRIDE, stride=4), :])
            for s in range(CMP_STRIDE):
                lhs_scr[cg, p * sub_per_page:(p + 1) * sub_per_page, s * HEAD_DIM:(s + 1) * HEAD_DIM] = (
                    stage[pl.ds(s, sub_per_page, stride=STAGE_PITCH), :])
    row = lax.broadcasted_iota(jnp.int32, (n_sub, HEAD_DIM), 0)
    for c in range(2):
        w1 = w1_ref[c]
        bias = _dot(pos_ref[c], w1)
        b0 = bias[0:1, :HEAD_DIM]
        b1 = bias[1:2, HEAD_DIM:]
        for g in range(2):
            a = _dot(lhs_scr[c * 2 + g].astype(BF16), w1)
            a0 = a[:, :HEAD_DIM] + b0
            a1 = a[:, HEAD_DIM:] + b1
            hid = a0 + pltpu.roll(a1, shift=n_sub - 1, axis=0)
            act = hid * _sigmoid(hid)
            o = _dot(act.astype(BF16), w2_ref[c])
            out_ref[0, c, g] = jnp.where(row < n_sub - 1, o, 0.0)


def _page_specs(n_pages, rows):
    return [pl.BlockSpec((rows, LANES), functools.partial(lambda b, pt, p: (pt[b * n_pages + p], 0), p=p))
            for p in range(n_pages)]


def compress_tokens(pool, page_table, page, w1cat, pos8, w2):
    nb, n_pages = page_table.shape
    n_sub = n_pages * page // CMP_STRIDE
    const = lambda shape: pl.BlockSpec(shape, lambda b, pt: (0,) * len(shape))
    grid_spec = pltpu.PrefetchScalarGridSpec(
        num_scalar_prefetch=1,
        grid=(nb,),
        in_specs=_page_specs(n_pages, 4 * page) + [const(w1cat.shape), const(pos8.shape), const(w2.shape)],
        out_specs=pl.BlockSpec((1, 2, 2, n_sub, HEAD_DIM), lambda b, pt: (b, 0, 0, 0, 0)),
        scratch_shapes=[pltpu.VMEM((4, n_sub, CMP_STRIDE * HEAD_DIM), F32),
                        pltpu.VMEM((8, page // CMP_STRIDE * STAGE_PITCH, LANES), F32)],
    )
    return pl.pallas_call(
        functools.partial(_compress_kernel, n_pages=n_pages, page=page),
        grid_spec=grid_spec,
        out_shape=jax.ShapeDtypeStruct((nb, 2, 2, n_sub, HEAD_DIM), F32),
        compiler_params=_params(1, VMEM_LIMIT),
        name="compress",
    )(page_table.reshape(-1), *([pool] * n_pages), w1cat, pos8, w2)


def compress_weights(phi_pos, phi_w1, phi_w2):
    n_half = CMP_LEN // CMP_STRIDE
    w1 = phi_w1.reshape(2, n_half, CMP_STRIDE, HEAD_DIM, HEAD_DIM).transpose(0, 2, 3, 1, 4)
    w1cat = w1.reshape(2, CMP_STRIDE * HEAD_DIM, n_half * HEAD_DIM).astype(BF16)
    pos = phi_pos.reshape(2, n_half, CMP_STRIDE * HEAD_DIM)
    pos8 = jnp.pad(pos, ((0, 0), (0, 8 - n_half), (0, 0))).astype(BF16)
    return w1cat, pos8, phi_w2.astype(BF16)


def _biased_attend(q, k, v, bias, key_slope):
    s = _dot_nt(q, k) * (SCALE * LOG2E) + key_slope + bias
    e = jnp.exp2(s - jnp.max(s, axis=-1, keepdims=True))
    d = jnp.sum(e, axis=-1, keepdims=True)
    return _dot(e.astype(BF16), v) / jnp.where(d > 0, d, 1.0)


def _top_rank(score, n_blocks):
    blk = lax.broadcasted_iota(jnp.int32, score.shape, 1)
    rank = jnp.zeros(score.shape, F32)
    for n2 in range(n_blocks):
        col = score[:, n2:n2 + 1]
        rank = rank + jnp.where(col > score, 1.0, 0.0) + jnp.where((col == score) & (blk > n2), 1.0, 0.0)
    return rank


def _scalar_pick(idx, values):
    out = jnp.float32(values[-1])
    for i in range(len(values) - 2, -1, -1):
        out = jnp.where(idx == i, jnp.float32(values[i]), out)
    return out


def _alibi_slopes(n_heads):
    return [float(2.0 ** (-8.0 * (h + 1) / n_heads)) for h in range(n_heads)]


def _masked_softmax(s, valid, axis):
    s = jnp.where(valid, s, NEG_INF)
    e = jnp.where(valid, jnp.exp(s - jnp.max(s, axis=axis, keepdims=True)), 0.0)
    d = jnp.sum(e, axis=axis, keepdims=True)
    return e / jnp.where(d > 0, d, 1.0)


def _cmp_branch(q, kc, vc, cover_t, valid, valid_t):
    imp_t = None
    outs = []
    for r in range(NSA_GROUP):
        qr = q[:, r * HEAD_DIM:(r + 1) * HEAD_DIM]
        p = _masked_softmax(_dot_nt(qr, kc) * SCALE, valid, -1)
        outs.append(_dot(p.astype(BF16), vc))
        p_t = _masked_softmax(_dot_nt(kc, qr) * SCALE, valid_t, 0)
        p_hi = p_t.astype(BF16)
        p_lo = (p_t - p_hi.astype(F32)).astype(BF16)
        part = _dot(cover_t, p_hi) + _dot(cover_t, p_lo)
        imp_t = part if imp_t is None else imp_t + part
    return outs, imp_t


def _select_blocks_t(imp_t, tpos_row, n_sel):
    blk = lax.broadcasted_iota(jnp.int32, imp_t.shape, 0)
    cur = tpos_row // SEL_BLOCK
    forced = (blk == 0) | (blk == cur) | (blk == cur - 1)
    score = jnp.where(blk * SEL_BLOCK <= tpos_row, imp_t + FORCE_BONUS * jnp.where(forced, 1.0, 0.0), NEG_INF)
    rank = jnp.zeros(score.shape, F32)
    for n2 in range(n_sel):
        row = score[n2:n2 + 1, :]
        rank = rank + jnp.where(row > score, 1.0, 0.0) + jnp.where((row == score) & (blk > n2), 1.0, 0.0)
    return jnp.where(rank < min(SEL_TOP_N, n_sel), 1.0, 0.0)


def _select_blocks(imp, tpos, n_sel):
    blk = lax.broadcasted_iota(jnp.int32, imp.shape, 1)
    cur = tpos // SEL_BLOCK
    forced = (blk == 0) | (blk == cur) | (blk == cur - 1)
    allowed = (blk * SEL_BLOCK <= tpos) & (blk < n_sel)
    score = jnp.where(allowed, imp + FORCE_BONUS * jnp.where(forced, 1.0, 0.0), NEG_INF)
    rank = _top_rank(score, n_sel)
    return jnp.where((rank < min(SEL_TOP_N, n_sel)) & (blk < n_sel), 1.0, 0.0)


def _kv_rows(ref, token0, n_tokens, slab, n_slab):
    return ref[pl.ds(token0 * n_slab + slab, n_tokens, stride=n_slab), :]


def _causal_bias(bias, tq):
    w = bias.shape[1]
    row = lax.broadcasted_iota(jnp.int32, (tq, tq), 0)
    col = lax.broadcasted_iota(jnp.int32, (tq, tq), 1)
    diag = jnp.where(row >= col, bias[:, w - tq:], MASKED)
    return diag if w == tq else jnp.concatenate([bias[:, :w - tq], diag], axis=1)


def _nsa_prompt_kernel(q_ref, gate_ref, kcvc_ref, sel_ref, win_ref, cover_ref, expand_ref, wbias_ref, o_ref,
                       *, tq, seq):
    g = pl.program_id(1)
    qi = pl.program_id(2)
    nq = seq // tq
    n_sel = seq // SEL_BLOCK
    n_sub = seq // CMP_STRIDE
    n_win = WINDOW // tq
    tpos = qi * tq + lax.broadcasted_iota(jnp.int32, (tq, 1), 0)
    tpos_row = qi * tq + lax.broadcasted_iota(jnp.int32, (1, tq), 1)
    q = q_ref[...]

    n_idx = lax.broadcasted_iota(jnp.int32, (1, n_sub), 1)
    valid = (n_idx * CMP_STRIDE + (CMP_LEN - 1) <= tpos) & (n_idx < n_sub - 1)
    n_idx_t = lax.broadcasted_iota(jnp.int32, (n_sub, 1), 0)
    valid_t = (n_idx_t * CMP_STRIDE + (CMP_LEN - 1) <= tpos_row) & (n_idx_t < n_sub - 1)
    o_cmp, imp_t = _cmp_branch(q, kcvc_ref[0, 0, 0].astype(BF16), kcvc_ref[0, 1, 0].astype(BF16), cover_ref[...],
                               valid, valid_t)
    sel_t = _select_blocks_t(imp_t[:n_sel], tpos_row, n_sel)
    selected = jnp.concatenate([sel_t, jnp.zeros((LANES - n_sel, tq), F32)], axis=0).T.astype(BF16)

    slopes = _alibi_slopes(NSA_HEADS)
    slope = [jnp.where(g == 0, jnp.float32(slopes[r] * LOG2E), jnp.float32(slopes[NSA_GROUP + r] * LOG2E))
             for r in range(NSA_GROUP)]
    gates = _sigmoid(gate_ref[...])

    for c in range(nq):
        @pl.when(qi == c)
        def _(c=c):
            w = (c + 1) * tq
            bias = _causal_bias(_dot(selected, expand_ref[:, :w]) + MASKED, tq)
            k = _kv_rows(sel_ref, 0, w, g, 4).astype(BF16)
            v = _kv_rows(sel_ref, 0, w, 2 + g, 4).astype(BF16)
            kpos = lax.broadcasted_iota(jnp.int32, (1, w), 1).astype(F32)
            c0 = max(c - n_win, 0)
            ww = (c + 1 - c0) * tq
            kw = _kv_rows(win_ref, c0 * tq, ww, g, 4).astype(BF16)
            vw = _kv_rows(win_ref, c0 * tq, ww, 2 + g, 4).astype(BF16)
            wbias = wbias_ref[:, (n_win + 1) * tq - ww:]
            kpos_w = (c0 * tq + lax.broadcasted_iota(jnp.int32, (1, ww), 1)).astype(F32)
            for r in range(NSA_GROUP):
                qr = q[:, r * HEAD_DIM:(r + 1) * HEAD_DIM]
                o_sel = _biased_attend(qr, k, v, bias, slope[r] * kpos)
                o_win = _biased_attend(qr, kw, vw, wbias, slope[r] * kpos_w)
                o_ref[:, r * HEAD_DIM:(r + 1) * HEAD_DIM] = (gates[:, 3 * r:3 * r + 1] * o_cmp[r]
                                                              + gates[:, 3 * r + 1:3 * r + 2] * o_sel
                                                              + gates[:, 3 * r + 2:3 * r + 3] * o_win)


def nsa_prompt_constants(n_sub, n_keys, tq):
    c_start = np.arange(n_sub)[None, :] * CMP_STRIDE
    s_start = np.arange(LANES)[:, None] * SEL_BLOCK
    cover_t = ((c_start < s_start + SEL_BLOCK) & (c_start + CMP_LEN > s_start)).astype(np.float32)
    cover_t[:, n_sub - 1:] = 0.0
    expand = np.where(np.arange(n_keys)[None, :] // SEL_BLOCK == np.arange(LANES)[:, None], -MASKED, 0.0)
    dist = WINDOW + np.arange(tq)[:, None] - np.arange(WINDOW + tq)[None, :]
    wbias = np.where((dist >= 0) & (dist < WINDOW), 0.0, MASKED)
    return jnp.asarray(cover_t, BF16), jnp.asarray(expand, BF16), jnp.asarray(wbias, F32)


def nsa_constants(n_sub, n_keys):
    c_start = np.arange(n_sub)[:, None] * CMP_STRIDE
    s_start = np.arange(LANES)[None, :] * SEL_BLOCK
    cover = ((c_start < s_start + SEL_BLOCK) & (c_start + CMP_LEN > s_start)).astype(np.float32)
    cover[n_sub - 1:] = 0.0
    expand = (np.arange(n_keys)[None, :] // SEL_BLOCK == np.arange(LANES)[:, None]).astype(np.float32)
    return jnp.asarray(cover, BF16), jnp.asarray(expand, BF16)


def nsa_prompt(q_all, gates, kcvc, sel_kv, win_kv, batch, seq, tq=256):
    assert WINDOW % tq == 0 and seq % tq == 0 and (seq // SEL_BLOCK) % 8 == 0
    nq = seq // tq
    n_sub = seq // CMP_STRIDE
    cover_t, expand, wbias = nsa_prompt_constants(n_sub, seq, tq)
    gw = NSA_GROUP * HEAD_DIM
    kv_spec = pl.BlockSpec((seq * 4, LANES), lambda b, g, i: (b, 0))
    const = lambda a: pl.BlockSpec(a.shape, lambda b, g, i: (0, 0))
    return pl.pallas_call(
        functools.partial(_nsa_prompt_kernel, tq=tq, seq=seq),
        grid=(batch, NSA_KV_HEADS, nq),
        in_specs=[
            pl.BlockSpec((tq, gw), lambda b, g, i: (b * nq + i, g)),
            pl.BlockSpec((tq, LANES), lambda b, g, i: (b * nq + i, g)),
            pl.BlockSpec((1, 2, 1, n_sub, HEAD_DIM), lambda b, g, i: (b, 0, g, 0, 0)),
            kv_spec, kv_spec, const(cover_t), const(expand), const(wbias),
        ],
        out_specs=pl.BlockSpec((tq, gw), lambda b, g, i: (b * nq + i, g)),
        out_shape=jax.ShapeDtypeStruct((batch * seq, NSA_HEADS * HEAD_DIM), F32),
        compiler_params=_params(3, VMEM_LIMIT),
        name="nsa_prompt",
    )(q_all, gates, kcvc, sel_kv, win_kv, cover_t, expand, wbias)


def _moba_select(gate, n_past, n_blocks):
    blk = lax.broadcasted_iota(jnp.int32, gate.shape, 1)
    past = blk < n_past
    score = jnp.where(past, gate, NEG_INF)
    rank = _top_rank(score, n_blocks)
    return jnp.where(past & (rank < min(MOBA_TOP_K, n_blocks)), 1.0, 0.0)


def _moba_prompt_kernel(q_ref, kv_ref, expand_ref, o_ref, kmean_scr, *, seq):
    tq = MOBA_BLOCK
    n_slab = 2 * MOBA_HEADS
    h = pl.program_id(1)
    qi = pl.program_id(2)
    nb = seq // tq
    q = q_ref[...]
    slope = _scalar_pick(h, [s * LOG2E for s in _alibi_slopes(MOBA_HEADS)])

    @pl.when(qi == 0)
    def _():
        kmean_scr[...] = jnp.zeros(kmean_scr.shape, F32)
        for n in range(nb):
            kmean_scr[n:n + 1, :] = jnp.mean(_kv_rows(kv_ref, n * tq, tq, h, n_slab), axis=0, keepdims=True)

    gate = _dot_nt(q, kmean_scr[...].astype(BF16))
    blk = lax.broadcasted_iota(jnp.int32, gate.shape, 1)
    selected = jnp.where(blk == qi, 1.0, _moba_select(gate, qi, nb - 1)).astype(BF16)

    for c in range(nb):
        @pl.when(qi == c)
        def _(c=c):
            w = (c + 1) * tq
            bias = _causal_bias(_dot(selected, expand_ref[:, :w]) + MASKED, tq)
            k = _kv_rows(kv_ref, 0, w, h, n_slab).astype(BF16)
            v = _kv_rows(kv_ref, 0, w, MOBA_HEADS + h, n_slab).astype(BF16)
            kpos = lax.broadcasted_iota(jnp.int32, (1, w), 1).astype(F32)
            o_ref[...] = _biased_attend(q, k, v, bias, slope * kpos)


def moba_prompt(q_all, q_col, moba_kv, batch, seq):
    tq = MOBA_BLOCK
    nq = seq // tq
    expand = jnp.asarray(np.where(np.arange(seq)[None, :] // MOBA_BLOCK == np.arange(LANES)[:, None], -MASKED, 0.0),
                         BF16)
    return pl.pallas_call(
        functools.partial(_moba_prompt_kernel, seq=seq),
        grid=(batch, MOBA_HEADS, nq),
        in_specs=[
            pl.BlockSpec((tq, HEAD_DIM), lambda b, h, i: (b * nq + i, q_col + h)),
            pl.BlockSpec((seq * 2 * MOBA_HEADS, LANES), lambda b, h, i: (b, 0)),
            pl.BlockSpec(expand.shape, lambda b, h, i: (0, 0)),
        ],
        out_specs=pl.BlockSpec((tq, HEAD_DIM), lambda b, h, i: (b * nq + i, h)),
        out_shape=jax.ShapeDtypeStruct((batch * seq, MOBA_HEADS * HEAD_DIM), F32),
        scratch_shapes=[pltpu.VMEM((LANES, HEAD_DIM), F32)],
        compiler_params=pltpu.CompilerParams(dimension_semantics=("parallel", "parallel", "arbitrary"),
                                             vmem_limit_bytes=VMEM_LIMIT),
        name="moba_prompt",
    )(q_all, moba_kv, expand)


def _padded_tile(rows, n):
    return jnp.concatenate([rows, jnp.zeros((n - rows.shape[0], rows.shape[1]), F32)], axis=0).astype(BF16)


def _nsa_sample_kernel(pt_ref, *refs, n_pages, page, t_new, n_win):
    page_refs = refs[:n_pages]
    (q_ref, gate_ref, kcvc_ref, new_sel_ref, cwin_ref, new_win_ref, cover_ref, sbias_ref, wbias_ref, o_ref,
     kv_scr, win_scr) = refs[n_pages:]
    n_slab = 2 * NSA_KV_HEADS
    rows_pp = page * n_slab
    past = n_pages * page
    n_sub = past // CMP_STRIDE
    n_sel = -(-(past + t_new) // SEL_BLOCK)
    block_w = SEL_BLOCK * n_slab
    rows = NSA_GROUP * t_new

    for p in range(n_pages):
        kv_scr[p * rows_pp:(p + 1) * rows_pp, :] = page_refs[p][...].astype(BF16)
    kv_scr[past * n_slab:, :] = _padded_tile(new_sel_ref[...], LANES)
    win_scr[:n_win * n_slab, :] = cwin_ref[...].astype(BF16)
    win_scr[n_win * n_slab:, :] = _padded_tile(new_win_ref[...], LANES)
    q = _stack_heads(q_ref, 0, t_new, NSA_HEADS)

    tpos = past + lax.broadcasted_iota(jnp.int32, (rows, 1), 0) % t_new
    tpos_q = past + lax.broadcasted_iota(jnp.int32, (t_new, 1), 0)
    n_idx = lax.broadcasted_iota(jnp.int32, (1, n_sub), 1)
    valid = (n_idx * CMP_STRIDE + (CMP_LEN - 1) <= tpos) & (n_idx < n_sub - 1)
    o_cmp, sel_bias = [], []
    for g in range(NSA_KV_HEADS):
        o_g, imp = _cmp_branch_rows(q[g * rows:(g + 1) * rows], kcvc_ref[0, 0, g].astype(BF16),
                                    kcvc_ref[0, 1, g].astype(BF16), cover_ref[...], valid)
        imp_t = imp[0:t_new]
        for r in range(1, NSA_GROUP):
            imp_t = imp_t + imp[r * t_new:(r + 1) * t_new]
        o_cmp.append(o_g)
        sel_bias += [(1.0 - _select_blocks(imp_t, tpos_q, n_sel)) * MASKED] * NSA_GROUP
    sel_bias = jnp.concatenate(sel_bias, axis=0)

    bias = jnp.concatenate(
        [sbias_ref[:, n * block_w:(n + 1) * block_w] + sel_bias[:, n:n + 1] for n in range(past // SEL_BLOCK)]
        + [sbias_ref[:, past * n_slab:] + sel_bias[:, n_sel - 1:n_sel]], axis=1)
    o_sel = _slab_attend(q, kv_scr[...], bias, NSA_KV_HEADS)
    o_win = _slab_attend(q, win_scr[...], wbias_ref[...], NSA_KV_HEADS)

    gates = _sigmoid(gate_ref[...])
    for h in range(NSA_HEADS):
        g, r = divmod(h, NSA_GROUP)
        lane = g * LANES + 3 * r
        hs = slice(h * t_new, (h + 1) * t_new)
        o_ref[:, h * HEAD_DIM:(h + 1) * HEAD_DIM] = (gates[:, lane:lane + 1] * o_cmp[g][r * t_new:(r + 1) * t_new]
                                                      + gates[:, lane + 1:lane + 2] * o_sel[hs]
                                                      + gates[:, lane + 2:lane + 3] * o_win[hs])


def _cmp_branch_rows(q, kc, vc, cover, valid):
    s = jnp.where(valid, _dot_nt(q, kc) * SCALE, NEG_INF)
    e = jnp.where(valid, jnp.exp(s - jnp.max(s, axis=-1, keepdims=True)), 0.0)
    d = jnp.sum(e, axis=-1, keepdims=True)
    p = e / jnp.where(d > 0, d, 1.0)
    p_hi = p.astype(BF16)
    p_lo = (p - p_hi.astype(F32)).astype(BF16)
    return _dot(p_hi, vc), _dot(p_hi, cover) + _dot(p_lo, cover)


def nsa_sample(q_all, gates, kcvc, sel_pool, page_table, page, new_sel, cache_win, new_win, t_new):
    nb, n_pages = page_table.shape
    past = n_pages * page
    n_sub = past // CMP_STRIDE
    n_slab = 2 * NSA_KV_HEADS
    n_win = cache_win.shape[0] // (n_slab * nb)
    assert past % SEL_BLOCK == 0 and t_new * n_slab <= LANES and past // SEL_BLOCK < LANES
    cover, _ = nsa_constants(n_sub, LANES)
    new_keys = LANES // n_slab
    slopes = _alibi_slopes(NSA_HEADS)
    head_slab = [h // NSA_GROUP for h in range(NSA_HEADS)]
    t_pos = past + np.arange(t_new)
    causal = lambda pos, real: (pos[None, :] <= t_pos[:, None]) & real[None, :]
    sel_pos = np.arange(past + new_keys)
    sbias = _slab_bias(head_slab, slopes, sel_pos, causal(sel_pos, sel_pos < past + t_new), n_slab, t_new)
    win_pos = past - n_win + np.arange(n_win + new_keys)
    in_window = (t_pos[:, None] - win_pos[None, :] < WINDOW) & (win_pos[None, :] >= 0)
    wbias = _slab_bias(head_slab, slopes, win_pos, causal(win_pos, win_pos < past + t_new) & in_window, n_slab, t_new)
    width = NSA_HEADS * HEAD_DIM
    const = lambda a: pl.BlockSpec(a.shape, lambda b, pt: (0,) * a.ndim)
    rows = lambda n: pl.BlockSpec((n_slab * n, LANES), lambda b, pt: (b, 0))
    grid_spec = pltpu.PrefetchScalarGridSpec(
        num_scalar_prefetch=1,
        grid=(nb,),
        in_specs=_page_specs(n_pages, n_slab * page) + [
            pl.BlockSpec((t_new, width), lambda b, pt: (b, 0)),
            pl.BlockSpec((t_new, NSA_KV_HEADS * LANES), lambda b, pt: (b, 0)),
            pl.BlockSpec((1, 2, 2, n_sub, HEAD_DIM), lambda b, pt: (b, 0, 0, 0, 0)),
            rows(t_new), rows(n_win), rows(t_new),
            const(cover), const(sbias), const(wbias)],
        out_specs=pl.BlockSpec((t_new, width), lambda b, pt: (b, 0)),
        scratch_shapes=[pltpu.VMEM((past * n_slab + LANES, HEAD_DIM), BF16),
                        pltpu.VMEM((n_win * n_slab + LANES, HEAD_DIM), BF16)],
    )
    return pl.pallas_call(
        functools.partial(_nsa_sample_kernel, n_pages=n_pages, page=page, t_new=t_new, n_win=n_win),
        grid_spec=grid_spec,
        out_shape=jax.ShapeDtypeStruct((nb * t_new, width), F32),
        compiler_params=_params(1, VMEM_LIMIT),
        name="nsa_sample",
    )(page_table.reshape(-1), *([sel_pool] * n_pages), q_all, gates, kcvc, new_sel, cache_win, new_win, cover, sbias,
      wbias)


def _slab_attend(q, kv, bias, shift):
    s = _dot_nt(q, kv) * (SCALE * LOG2E) + bias
    e = jnp.exp2(s - jnp.max(s, axis=-1, keepdims=True))
    d = jnp.sum(e, axis=-1, keepdims=True)
    return _dot(pltpu.roll(e, shift, axis=1).astype(BF16), kv) / jnp.where(d > 0, d, 1.0)


def _stack_heads(ref, row0, n_rows, n_heads):
    return jnp.concatenate([ref[row0:row0 + n_rows, h * HEAD_DIM:(h + 1) * HEAD_DIM] for h in range(n_heads)],
                           axis=0).astype(BF16)


def _slab_bias(head_slab, slopes, key_pos, key_ok, n_slab, t_new):
    n_heads, n_keys = len(head_slab), key_pos.shape[0]
    slab = np.arange(n_keys * n_slab) % n_slab
    out = np.full((n_heads, t_new, n_keys * n_slab), MASKED, np.float64)
    for h in range(n_heads):
        ok = np.repeat(key_ok, n_slab, axis=1) & (slab == head_slab[h])[None, :]
        out[h] = np.where(ok, slopes[h] * LOG2E * np.repeat(key_pos, n_slab)[None, :], MASKED)
    return jnp.asarray(out.reshape(n_heads * t_new, n_keys * n_slab), F32)


def _moba_sample_kernel(pt_ref, *refs, n_pages, page, t_new):
    page_refs = refs[:n_pages]
    q_ref, new_ref, bias_ref, o_ref, kv_scr, mean_scr = refs[n_pages:]
    n_slab = 2 * MOBA_HEADS
    rows_pp = page * n_slab
    past = n_pages * page
    pages_per_block = MOBA_BLOCK // page
    n_past = n_pages // pages_per_block
    block_w = MOBA_BLOCK * n_slab

    mean_scr[...] = jnp.zeros(mean_scr.shape, F32)
    for n in range(n_past):
        total = None
        for p in range(n * pages_per_block, (n + 1) * pages_per_block):
            x = page_refs[p][...]
            kv_scr[p * rows_pp:(p + 1) * rows_pp, :] = x.astype(BF16)
            part = jnp.sum(x.reshape(page, n_slab, LANES), axis=0)
            total = part if total is None else total + part
        mean_scr[n * n_slab:(n + 1) * n_slab, :] = total * (1.0 / MOBA_BLOCK)
    x_new = new_ref[...]
    kv_scr[past * n_slab:, :] = _padded_tile(x_new, LANES)
    mean_scr[n_past * n_slab:(n_past + 1) * n_slab, :] = (
        jnp.sum(x_new.reshape(t_new, n_slab, LANES), axis=0) * (1.0 / MOBA_BLOCK))

    tpos = past + lax.broadcasted_iota(jnp.int32, (t_new, 1), 0)
    cur = tpos // MOBA_BLOCK
    blk = lax.broadcasted_iota(jnp.int32, (t_new, LANES), 1)
    sel_bias = []
    for h in range(MOBA_HEADS):
        q = q_ref[:, h * HEAD_DIM:(h + 1) * HEAD_DIM].astype(BF16)
        kmean = mean_scr[pl.ds(h, LANES // n_slab, stride=n_slab), :]
        kmean = jnp.concatenate([kmean, jnp.zeros((LANES - kmean.shape[0], LANES), F32)], axis=0)
        gate = _dot_nt(q, kmean.astype(BF16))
        selected = jnp.where(blk == cur, 1.0, _moba_select(gate, cur, n_past))
        sel_bias.append((1.0 - selected) * MASKED)
    sel_bias = jnp.concatenate(sel_bias, axis=0)

    bias = jnp.concatenate(
        [bias_ref[:, n * block_w:(n + 1) * block_w] + sel_bias[:, n:n + 1] for n in range(n_past)]
        + [bias_ref[:, n_past * block_w:] + sel_bias[:, n_past:n_past + 1]], axis=1)
    o = _slab_attend(_stack_heads(q_ref, 0, t_new, MOBA_HEADS), kv_scr[...], bias, MOBA_HEADS)
    for h in range(MOBA_HEADS):
        o_ref[:, h * HEAD_DIM:(h + 1) * HEAD_DIM] = o[h * t_new:(h + 1) * t_new]


def moba_sample(q_all, q_col, moba_pool, page_table, page, new_kv, t_new):
    nb, n_pages = page_table.shape
    past = n_pages * page
    width = MOBA_HEADS * HEAD_DIM
    n_slab = 2 * MOBA_HEADS
    assert past % MOBA_BLOCK == 0 and t_new * n_slab <= LANES and past // MOBA_BLOCK < LANES // n_slab
    new_keys = LANES // n_slab
    key_pos = np.arange(past + new_keys)
    t_pos = past + np.arange(t_new)
    key_ok = (key_pos[None, :] <= t_pos[:, None]) & (key_pos[None, :] < past + t_new)
    bias = _slab_bias(list(range(MOBA_HEADS)), _alibi_slopes(MOBA_HEADS), key_pos, key_ok, n_slab, t_new)
    grid_spec = pltpu.PrefetchScalarGridSpec(
        num_scalar_prefetch=1,
        grid=(nb,),
        in_specs=_page_specs(n_pages, n_slab * page) + [
            pl.BlockSpec((t_new, width), lambda b, pt: (b, q_col)),
            pl.BlockSpec((t_new * n_slab, LANES), lambda b, pt: (b, 0)),
            pl.BlockSpec(bias.shape, lambda b, pt: (0, 0))],
        out_specs=pl.BlockSpec((t_new, width), lambda b, pt: (b, 0)),
        scratch_shapes=[pltpu.VMEM((past * n_slab + LANES, HEAD_DIM), BF16), pltpu.VMEM((LANES, LANES), F32)],
    )
    return pl.pallas_call(
        functools.partial(_moba_sample_kernel, n_pages=n_pages, page=page, t_new=t_new),
        grid_spec=grid_spec,
        out_shape=jax.ShapeDtypeStruct((nb * t_new, width), F32),
        compiler_params=_params(1, VMEM_LIMIT),
        name="moba_sample",
    )(page_table.reshape(-1), *([moba_pool] * n_pages), q_all, new_kv, bias)


def _cross_kernel(q_ref, kv_ref, o_ref, *, n_mem):
    n_slab = 2 * MEM_HEADS
    for h in range(MEM_HEADS):
        k = _kv_rows(kv_ref, 0, n_mem, h, n_slab).astype(BF16)
        v = _kv_rows(kv_ref, 0, n_mem, MEM_HEADS + h, n_slab).astype(BF16)
        s = _dot_nt(q_ref[:, h * HEAD_DIM:(h + 1) * HEAD_DIM], k) * SCALE
        e = jnp.exp(s - jnp.max(s, axis=-1, keepdims=True))
        p = e / jnp.sum(e, axis=-1, keepdims=True)
        o_ref[:, h * HEAD_DIM:(h + 1) * HEAD_DIM] = _dot(p.astype(BF16), v)


def cross_attend(q_all, q_col, mem_kv, batch, rows, tq):
    nq = rows // tq
    n_slab = 2 * MEM_HEADS
    width = MEM_HEADS * HEAD_DIM
    n_mem = mem_kv.shape[0] // (batch * n_slab)
    return pl.pallas_call(
        functools.partial(_cross_kernel, n_mem=n_mem),
        grid=(batch, nq),
        in_specs=[
            pl.BlockSpec((tq, width), lambda b, i: (b * nq + i, q_col)),
            pl.BlockSpec((n_mem * n_slab, LANES), lambda b, i: (b, 0)),
        ],
        out_specs=pl.BlockSpec((tq, width), lambda b, i: (b * nq + i, 0)),
        out_shape=jax.ShapeDtypeStruct((batch * rows, width), F32),
        compiler_params=_params(2, VMEM_LIMIT),
        name="cross_attend",
    )(q_all, mem_kv)


def _cross_sample_kernel(q_ref, kv_ref, bias_ref, o_ref, *, n_mem, t_new, bb):
    n_slab = 2 * MEM_HEADS
    rows = n_mem * n_slab
    for i in range(bb):
        q = _stack_heads(q_ref, i * t_new, t_new, MEM_HEADS)
        o = _slab_attend(q, kv_ref[i * rows:(i + 1) * rows, :].astype(BF16), bias_ref[...], MEM_HEADS)
        for h in range(MEM_HEADS):
            o_ref[i * t_new:(i + 1) * t_new, h * HEAD_DIM:(h + 1) * HEAD_DIM] = o[h * t_new:(h + 1) * t_new]


def cross_attend_sample(q_all, q_col, mem_kv, batch, t_new, bb):
    assert batch % bb == 0
    n_slab = 2 * MEM_HEADS
    width = MEM_HEADS * HEAD_DIM
    n_mem = mem_kv.shape[0] // (batch * n_slab)
    bias = _slab_bias(list(range(MEM_HEADS)), [0.0] * MEM_HEADS, np.arange(n_mem), np.ones((t_new, n_mem), bool),
                      n_slab, t_new)
    return pl.pallas_call(
        functools.partial(_cross_sample_kernel, n_mem=n_mem, t_new=t_new, bb=bb),
        grid=(batch // bb,),
        in_specs=[
            pl.BlockSpec((bb * t_new, width), lambda b: (b, q_col)),
            pl.BlockSpec((bb * n_mem * n_slab, LANES), lambda b: (b, 0)),
            pl.BlockSpec(bias.shape, lambda b: (0, 0)),
        ],
        out_specs=pl.BlockSpec((bb * t_new, width), lambda b: (b, 0)),
        out_shape=jax.ShapeDtypeStruct((batch * t_new, width), F32),
        compiler_params=_params(1, VMEM_LIMIT),
        name="cross_sample",
    )(q_all, mem_kv, bias)


def _merge_kernel(x_ref, on_ref, om_ref, oc_ref, z_ref, gl_ref, wn_ref, wm_ref, wc_ref, wo_ref, fg_ref, y_ref, mix_scr,
                  *, tn):
    d = x_ref.shape[1]
    wn_k, wm_k = wn_ref.shape[0], wm_ref.shape[0]
    z = z_ref[...]
    act = z * _sigmoid(z)
    un = (on_ref[...] * act[:, :wn_k]).astype(BF16)
    um = (om_ref[...] * act[:, wn_k:wn_k + wm_k]).astype(BF16)
    uc = (oc_ref[...] * act[:, wn_k + wm_k:]).astype(BF16)
    for c in range(d // tn):
        cols = slice(c * tn, (c + 1) * tn)
        mix = (_sigmoid(gl_ref[:, c * tn:(c + 1) * tn]) * _dot(un, wn_ref[:, cols])
               + _sigmoid(gl_ref[:, d + c * tn:d + (c + 1) * tn]) * _dot(um, wm_ref[:, cols])
               + _sigmoid(gl_ref[:, 2 * d + c * tn:2 * d + (c + 1) * tn]) * _dot(uc, wc_ref[:, cols]))
        mix_scr[:, cols] = mix.astype(BF16)
    y = x_ref[...] + _dot(mix_scr[...], wo_ref[...])
    inv = lax.rsqrt(jnp.mean(y * y, axis=-1, keepdims=True) + RMS_EPS)
    y_ref[...] = y * inv * fg_ref[...]


def merge_branches(x, o_nsa, o_moba, o_mem, z_all, gate_logits, w_nsa, w_moba, w_mem, w_out, final_g, tm=256, tn=512):
    m, d = x.shape
    tm = min(tm, m)
    row = lambda a: pl.BlockSpec((tm, a.shape[1]), lambda i: (i, 0))
    whole = lambda a: pl.BlockSpec(a.shape, lambda i: (0, 0), pipeline_mode=pl.Buffered(1))
    final_g = final_g.reshape(1, d)
    return pl.pallas_call(
        functools.partial(_merge_kernel, tn=tn),
        grid=(m // tm,),
        in_specs=[row(x), row(o_nsa), row(o_moba), row(o_mem), row(z_all), row(gate_logits),
                  whole(w_nsa), whole(w_moba), whole(w_mem), whole(w_out), whole(final_g)],
        out_specs=row(x),
        out_shape=jax.ShapeDtypeStruct((m, d), F32),
        scratch_shapes=[pltpu.VMEM((tm, d), BF16)],
        compiler_params=_params(1, VMEM_LIMIT),
        name="merge",
    )(x, o_nsa, o_moba, o_mem, z_all, gate_logits, w_nsa, w_moba, w_mem, w_out, final_g)


def _in_weights(w_in, d_model):
    splits = (NSA_HEADS * HEAD_DIM, 4 * HEAD_DIM, 4 * HEAD_DIM, 4 * HEAD_DIM, 3 * NSA_HEADS, NSA_HEADS * HEAD_DIM,
              MOBA_HEADS * HEAD_DIM, 2 * MOBA_HEADS * HEAD_DIM, MOBA_HEADS * HEAD_DIM, MEM_HEADS * HEAD_DIM,
              MEM_HEADS * HEAD_DIM, N_BRANCH * d_model)
    offs = np.cumsum((0,) + splits)
    assert offs[-1] == w_in.shape[1]
    (nsa_q, cmp_kv, sel_kv, win_kv, nsa_g, nsa_z, moba_q, moba_kv, moba_z, mem_q, mem_z, merge_g) = (
        w_in[:, offs[i]:offs[i + 1]] for i in range(len(splits)))
    per_group = 3 * NSA_GROUP
    gate_w = jnp.pad(nsa_g.reshape(-1, NSA_KV_HEADS, per_group), ((0, 0), (0, 0), (0, LANES - per_group)))
    cast = lambda w: w.astype(BF16)
    return dict(q=cast(jnp.concatenate([nsa_q, moba_q, mem_q], axis=1)), cmp=cast(cmp_kv), sel=cast(sel_kv),
                win=cast(win_kv), moba=cast(moba_kv), gate=cast(gate_w.reshape(-1, NSA_KV_HEADS * LANES)),
                z=cast(jnp.concatenate([nsa_z, moba_z, mem_z], axis=1)), merge=cast(merge_g))


def _in_projection(x, norm_g, w, q_dtype):
    tm = min(1024, x.shape[0] // 2)
    xn = rmsnorm_bf16(x, norm_g, 512)
    mm = lambda name, dtype: matmul(xn, w[name], dtype, tm, 512, "in_" + name)
    rows = lambda name: matmul_rows(xn, w[name], tm, "in_" + name)
    return dict(q=mm("q", q_dtype), cmp=rows("cmp"), sel=rows("sel"), win=rows("win"), moba=rows("moba"),
                gate=mm("gate", F32), z=mm("z", F32), merge=mm("merge", F32))


def kernel(x_prompt, x_sample, cache_cmp_kv, cache_sel_kv, cache_moba_kv, cache_win_kv, cache_mem_kv, page_table,
           mem_prompt, norm_g, w_in, phi_pos, phi_w1, phi_w2, mem_norm_g, w_mem_kv, w_br_nsa, w_br_moba, w_br_mem,
           w_out, final_norm_g):
    batch, seq, d_model = x_prompt.shape
    nb, t_new, _ = x_sample.shape
    n_mem = mem_prompt.shape[1]
    page = cache_cmp_kv.shape[2]
    n_win = cache_win_kv.shape[2]
    assert norm_g.shape[0] == 1, "one layer"
    assert seq % MOBA_BLOCK == 0 and (page_table.shape[1] * page) % MOBA_BLOCK == 0 and MOBA_BLOCK % page == 0
    l = 0
    q_moba_col = NSA_HEADS * HEAD_DIM // (MOBA_HEADS * HEAD_DIM)
    q_head_moba = NSA_HEADS
    q_mem_col = (NSA_HEADS + MOBA_HEADS) * HEAD_DIM // (MEM_HEADS * HEAD_DIM)
    lane_rows = lambda a: a.reshape(-1, LANES)

    w = _in_weights(w_in[l], d_model)
    w1cat, pos8, w2 = compress_weights(phi_pos[l], phi_w1[l], phi_w2[l])
    br = [a[l].astype(BF16) for a in (w_br_nsa, w_br_moba, w_br_mem, w_out)]

    xp = x_prompt.reshape(batch * seq, d_model)
    hp = _in_projection(xp, norm_g[l], w, BF16)
    mem_n = rmsnorm_bf16(mem_prompt.reshape(batch * n_mem, d_model), mem_norm_g[l], 512)
    mem_kv_p = matmul_rows(mem_n, w_mem_kv[l].astype(BF16), 1024, "mem_kv")
    pages_p = seq // page
    identity_table = jnp.arange(batch * pages_p, dtype=jnp.int32).reshape(batch, pages_p)
    kcvc_p = compress_tokens(hp["cmp"], identity_table, page, w1cat, pos8, w2)
    o_nsa_p = nsa_prompt(hp["q"], hp["gate"], kcvc_p, hp["sel"], hp["win"], batch, seq)
    o_moba_p = moba_prompt(hp["q"], q_head_moba, hp["moba"], batch, seq)
    o_mem_p = cross_attend(hp["q"], q_mem_col, mem_kv_p, batch, seq, min(seq, 1024))
    y_prompt = merge_branches(xp, o_nsa_p, o_moba_p, o_mem_p, hp["z"], hp["merge"], *br, final_norm_g)

    xs = x_sample.reshape(nb * t_new, d_model)
    hs = _in_projection(xs, norm_g[l], w, F32)
    kcvc_s = compress_tokens(lane_rows(cache_cmp_kv[l]), page_table, page, w1cat, pos8, w2)
    o_nsa_s = nsa_sample(hs["q"], hs["gate"], kcvc_s, lane_rows(cache_sel_kv[l]), page_table, page, hs["sel"],
                         lane_rows(cache_win_kv[l]), hs["win"], t_new)
    o_moba_s = moba_sample(hs["q"], q_moba_col, lane_rows(cache_moba_kv[l]), page_table, page, hs["moba"], t_new)
    o_mem_s = cross_attend_sample(hs["q"], q_mem_col, lane_rows(cache_mem_kv[l]), nb, t_new, 4 if nb % 4 == 0 else 1)
    y_sample = merge_branches(xs, o_nsa_s, o_moba_s, o_mem_s, hs["z"], hs["merge"], *br, final_norm_g)

    nsa_shape = lambda a, b_, t_: a.reshape(1, b_, t_, 2, NSA_KV_HEADS, HEAD_DIM)
    moba_shape = lambda a, b_, t_: a.reshape(1, b_, t_, 2, MOBA_HEADS, HEAD_DIM)
    keep_p = min(WINDOW, seq)
    win_all = jnp.concatenate([cache_win_kv[l], nsa_shape(hs["win"], nb, t_new)[0]], axis=1)
    keep_s = min(WINDOW, n_win + t_new)
    return (y_prompt.reshape(batch, seq, d_model), y_sample.reshape(nb, t_new, d_model),
            nsa_shape(hp["cmp"], batch, seq), nsa_shape(hp["sel"], batch, seq), moba_shape(hp["moba"], batch, seq),
            nsa_shape(hp["win"], batch, seq)[:, :, seq - keep_p:],
            mem_kv_p.reshape(1, batch, n_mem, 2, MEM_HEADS, HEAD_DIM),
            nsa_shape(hs["cmp"], nb, t_new), nsa_shape(hs["sel"], nb, t_new), moba_shape(hs["moba"], nb, t_new),
            win_all[None, :, n_win + t_new - keep_s:])
```

```python
import functools

import numpy as np
import jax
import jax.numpy as jnp
from jax import lax
from jax.experimental import pallas as pl
from jax.experimental.pallas import tpu as pltpu

HEAD_DIM = 128
NSA_HEADS = 8
NSA_KV_HEADS = 2
NSA_GROUP = NSA_HEADS // NSA_KV_HEADS
CMP_LEN = 32
CMP_STRIDE = 16
SEL_BLOCK = 64
SEL_TOP_N = 16
WINDOW = 512
FORCE_BONUS = 1.0e4
MOBA_HEADS = 4
MOBA_BLOCK = 256
MOBA_TOP_K = 3
MEM_HEADS = 4
N_BRANCH = 3
RMS_EPS = 1e-6
NEG_INF = -1e30
SCALE = HEAD_DIM ** -0.5
LOG2E = 1.4426950408889634
MASKED = -(2.0 ** 100)

LANES = 128
STAGE_PITCH = 24
VMEM_LIMIT = 56 * 1024 * 1024

F32 = jnp.float32
BF16 = jnp.bfloat16


def _params(n_axes, vmem=None):
    return pltpu.CompilerParams(dimension_semantics=("parallel",) * n_axes, vmem_limit_bytes=vmem)


def _dot(a, b):
    return jnp.dot(a, b, preferred_element_type=F32)


def _dot_nt(a, b):
    return lax.dot_general(a, b, (((1,), (1,)), ((), ())), preferred_element_type=F32)


def _sigmoid(x):
    return 1.0 / (1.0 + jnp.exp(-x))


def _rms_kernel(x_ref, g_ref, o_ref):
    x = x_ref[...]
    inv = lax.rsqrt(jnp.mean(x * x, axis=-1, keepdims=True) + RMS_EPS)
    o_ref[...] = (x * inv * g_ref[...]).astype(o_ref.dtype)


def rmsnorm_bf16(x, g, tm):
    n, d = x.shape
    return pl.pallas_call(
        _rms_kernel,
        grid=(n // tm,),
        in_specs=[pl.BlockSpec((tm, d), lambda i: (i, 0)), pl.BlockSpec((1, d), lambda i: (0, 0))],
        out_specs=pl.BlockSpec((tm, d), lambda i: (i, 0)),
        out_shape=jax.ShapeDtypeStruct((n, d), BF16),
        compiler_params=_params(1, VMEM_LIMIT),
        name="rmsnorm",
    )(x, g.reshape(1, d))


def _mm_kernel(x_ref, w_ref, o_ref):
    o_ref[...] = _dot(x_ref[...], w_ref[...]).astype(o_ref.dtype)


def matmul(x, w, out_dtype, tm, tn, name):
    m, k = x.shape
    n = w.shape[1]
    tm, tn = min(tm, m), min(tn, n)
    return pl.pallas_call(
        _mm_kernel,
        grid=(m // tm, n // tn),
        in_specs=[pl.BlockSpec((tm, k), lambda i, j: (i, 0)), pl.BlockSpec((k, tn), lambda i, j: (0, j))],
        out_specs=pl.BlockSpec((tm, tn), lambda i, j: (i, j)),
        out_shape=jax.ShapeDtypeStruct((m, n), out_dtype),
        compiler_params=_params(2, VMEM_LIMIT),
        name=name,
    )(x, w)


def _mm_rows_kernel(x_ref, w_ref, o_ref):
    tm = x_ref.shape[0]
    n_slab = w_ref.shape[1] // LANES
    res = _dot(x_ref[...], w_ref[...])
    for j in range(n_slab):
        o_ref[pl.ds(j, tm, stride=n_slab), :] = res[:, j * LANES:(j + 1) * LANES]


def matmul_rows(x, w, tm, name):
    m, k = x.shape
    n = w.shape[1]
    n_slab = n // LANES
    tm = min(tm, m)
    return pl.pallas_call(
        _mm_rows_kernel,
        grid=(m // tm,),
        in_specs=[pl.BlockSpec((tm, k), lambda i: (i, 0)), pl.BlockSpec((k, n), lambda i: (0, 0))],
        out_specs=pl.BlockSpec((tm * n_slab, LANES), lambda i: (i, 0)),
        out_shape=jax.ShapeDtypeStruct((m * n_slab, LANES), F32),
        compiler_params=_params(1, VMEM_LIMIT),
        name=name,
    )(x, w)


def _compress_kernel(pt_ref, *refs, n_pages, page):
    page_refs = refs[:n_pages]
    w1_ref, pos_ref, w2_ref, out_ref, lhs_scr, stage_scr = refs[n_pages:]
    sub_per_page = page // CMP_STRIDE
    n_sub = n_pages * sub_per_page
    for p in range(n_pages):
        for cg in range(4):
            stage = stage_scr.at[(p % 2) * 4 + cg]
            for j in range(sub_per_page):
                stage[j * STAGE_PITCH:j * STAGE_PITCH + CMP_STRIDE, :] = (
                    page_refs[p][pl.ds(4 * CMP_STRIDE * j + cg, CMP_STRIDE, stride=4), :])
            for s in range(CMP_STRIDE):
                lhs_scr[cg, p * sub_per_page:(p + 1) * sub_per_page, s * HEAD_DIM:(s + 1) * HEAD_DIM] = (
                    stage[pl.ds(s, sub_per_page, stride=STAGE_PITCH), :])
    row = lax.broadcasted_iota(jnp.int32, (n_sub, HEAD_DIM), 0)
    for c in range(2):
        w1 = w1_ref[c]
        bias = _dot(pos_ref[c], w1)
        b0 = bias[0:1, :HEAD_DIM]
        b1 = bias[1:2, HEAD_DIM:]
        for g in range(2):
            a = _dot(lhs_scr[c * 2 + g].astype(BF16), w1)
            a0 = a[:, :HEAD_DIM] + b0
            a1 = a[:, HEAD_DIM:] + b1
            hid = a0 + pltpu.roll(a1, shift=n_sub - 1, axis=0)
            act = hid * _sigmoid(hid)
            o = _dot(act.astype(BF16), w2_ref[c])
            out_ref[0, c, g] = jnp.where(row < n_sub - 1, o, 0.0)


def _page_specs(n_pages, rows):
    return [pl.BlockSpec((rows, LANES), functools.partial(lambda b, pt, p: (pt[b * n_pages + p], 0), p=p))
            for p in range(n_pages)]


def compress_tokens(pool, page_table, page, w1cat, pos8, w2):
    nb, n_pages = page_table.shape
    n_sub = n_pages * page // CMP_STRIDE
    const = lambda shape: pl.BlockSpec(shape, lambda b, pt: (0,) * len(shape))
    grid_spec = pltpu.PrefetchScalarGridSpec(
        num_scalar_prefetch=1,
        grid=(nb,),
        in_specs=_page_specs(n_pages, 4 * page) + [const(w1cat.shape), const(pos8.shape), const(w2.shape)],
        out_specs=pl.BlockSpec((1, 2, 2, n_sub, HEAD_DIM), lambda b, pt: (b, 0, 0, 0, 0)),
        scratch_shapes=[pltpu.VMEM((4, n_sub, CMP_STRIDE * HEAD_DIM), F32),
                        pltpu.VMEM((8, page // CMP_STRIDE * STAGE_PITCH, LANES), F32)],
    )
    return pl.pallas_call(
        functools.partial(_compress_kernel, n_pages=n_pages, page=page),
        grid_spec=grid_spec,
        out_shape=jax.ShapeDtypeStruct((nb, 2, 2, n_sub, HEAD_DIM), F32),
        compiler_params=_params(1, VMEM_LIMIT),
        name="compress",
    )(page_table.reshape(-1), *([pool] * n_pages), w1cat, pos8, w2)


def compress_weights(phi_pos, phi_w1, phi_w2):
    n_half = CMP_LEN // CMP_STRIDE
    w1 = phi_w1.reshape(2, n_half, CMP_STRIDE, HEAD_DIM, HEAD_DIM).transpose(0, 2, 3, 1, 4)
    w1cat = w1.reshape(2, CMP_STRIDE * HEAD_DIM, n_half * HEAD_DIM).astype(BF16)
    pos = phi_pos.reshape(2, n_half, CMP_STRIDE * HEAD_DIM)
    pos8 = jnp.pad(pos, ((0, 0), (0, 8 - n_half), (0, 0))).astype(BF16)
    return w1cat, pos8, phi_w2.astype(BF16)


def _biased_attend(q, k, v, bias, key_slope):
    s = _dot_nt(q, k) * (SCALE * LOG2E) + key_slope + bias
    e = jnp.exp2(s - jnp.max(s, axis=-1, keepdims=True))
    d = jnp.sum(e, axis=-1, keepdims=True)
    return _dot(e.astype(BF16), v) / jnp.where(d > 0, d, 1.0)


def _top_rank(score, n_blocks):
    blk = lax.broadcasted_iota(jnp.int32, score.shape, 1)
    rank = jnp.zeros(score.shape, F32)
    for n2 in range(n_blocks):
        col = score[:, n2:n2 + 1]
        rank = rank + jnp.where(col > score, 1.0, 0.0) + jnp.where((col == score) & (blk > n2), 1.0, 0.0)
    return rank


def _alibi_slopes(n_heads):
    return [float(2.0 ** (-8.0 * (h + 1) / n_heads)) for h in range(n_heads)]


def _masked_softmax(s, valid, axis):
    s = jnp.where(valid, s, NEG_INF)
    e = jnp.where(valid, jnp.exp(s - jnp.max(s, axis=axis, keepdims=True)), 0.0)
    d = jnp.sum(e, axis=axis, keepdims=True)
    return e / jnp.where(d > 0, d, 1.0)


def _cmp_branch(q, kc, vc, cover_t, valid, valid_t):
    imp_t = None
    outs = []
    for r in range(NSA_GROUP):
        qr = q[:, r * HEAD_DIM:(r + 1) * HEAD_DIM]
        p = _masked_softmax(_dot_nt(qr, kc) * SCALE, valid, -1)
        outs.append(_dot(p.astype(BF16), vc))
        p_t = _masked_softmax(_dot_nt(kc, qr) * SCALE, valid_t, 0)
        p_hi = p_t.astype(BF16)
        p_lo = (p_t - p_hi.astype(F32)).astype(BF16)
        part = _dot(cover_t, p_hi) + _dot(cover_t, p_lo)
        imp_t = part if imp_t is None else imp_t + part
    return outs, imp_t


def _select_blocks_t(imp_t, tpos_row, n_sel):
    blk = lax.broadcasted_iota(jnp.int32, imp_t.shape, 0)
    cur = tpos_row // SEL_BLOCK
    forced = (blk == 0) | (blk == cur) | (blk == cur - 1)
    score = jnp.where(blk * SEL_BLOCK <= tpos_row, imp_t + FORCE_BONUS * jnp.where(forced, 1.0, 0.0), NEG_INF)
    rank = jnp.zeros(score.shape, F32)
    for n2 in range(n_sel):
        row = score[n2:n2 + 1, :]
        rank = rank + jnp.where(row > score, 1.0, 0.0) + jnp.where((row == score) & (blk > n2), 1.0, 0.0)
    return jnp.where(rank < min(SEL_TOP_N, n_sel), 1.0, 0.0)


def _select_blocks(imp, tpos, n_sel):
    blk = lax.broadcasted_iota(jnp.int32, imp.shape, 1)
    cur = tpos // SEL_BLOCK
    forced = (blk == 0) | (blk == cur) | (blk == cur - 1)
    allowed = (blk * SEL_BLOCK <= tpos) & (blk < n_sel)
    score = jnp.where(allowed, imp + FORCE_BONUS * jnp.where(forced, 1.0, 0.0), NEG_INF)
    rank = _top_rank(score, n_sel)
    return jnp.where((rank < min(SEL_TOP_N, n_sel)) & (blk < n_sel), 1.0, 0.0)


def _kv_rows(ref, token0, n_tokens, slab, n_slab):
    return ref[pl.ds(token0 * n_slab + slab, n_tokens, stride=n_slab), :]


def _causal_bias(bias, tq):
    w = bias.shape[1]
    row = lax.broadcasted_iota(jnp.int32, (tq, tq), 0)
    col = lax.broadcasted_iota(jnp.int32, (tq, tq), 1)
    diag = jnp.where(row >= col, bias[:, w - tq:], MASKED)
    return diag if w == tq else jnp.concatenate([bias[:, :w - tq], diag], axis=1)


def _nsa_prompt_kernel(q_ref, gate_ref, kcvc_ref, sel_ref, win_ref, cover_ref, expand_ref, wbias_ref, o_ref,
                       ks_scr, vs_scr, kw_scr, vw_scr, *, tq, seq):
    g = pl.program_id(1)
    qi = pl.program_id(2)
    nq = seq // tq

    @pl.when(qi == 0)
    def _():
        ks_scr[...] = _kv_rows(sel_ref, 0, seq, g, 4).astype(BF16)
        vs_scr[...] = _kv_rows(sel_ref, 0, seq, 2 + g, 4).astype(BF16)
        kw_scr[...] = _kv_rows(win_ref, 0, seq, g, 4).astype(BF16)
        vw_scr[...] = _kv_rows(win_ref, 0, seq, 2 + g, 4).astype(BF16)

    n_sel = seq // SEL_BLOCK
    n_sub = seq // CMP_STRIDE
    n_win = WINDOW // tq
    tpos = qi * tq + lax.broadcasted_iota(jnp.int32, (tq, 1), 0)
    tpos_row = qi * tq + lax.broadcasted_iota(jnp.int32, (1, tq), 1)
    q = q_ref[...]

    n_idx = lax.broadcasted_iota(jnp.int32, (1, n_sub), 1)
    valid = (n_idx * CMP_STRIDE + (CMP_LEN - 1) <= tpos) & (n_idx < n_sub - 1)
    n_idx_t = lax.broadcasted_iota(jnp.int32, (n_sub, 1), 0)
    valid_t = (n_idx_t * CMP_STRIDE + (CMP_LEN - 1) <= tpos_row) & (n_idx_t < n_sub - 1)
    o_cmp, imp_t = _cmp_branch(q, kcvc_ref[0, 0, 0].astype(BF16), kcvc_ref[0, 1, 0].astype(BF16), cover_ref[...],
                               valid, valid_t)
    sel_t = _select_blocks_t(imp_t[:n_sel], tpos_row, n_sel)
    selected = jnp.concatenate([sel_t, jnp.zeros((LANES - n_sel, tq), F32)], axis=0).T.astype(BF16)

    slopes = _alibi_slopes(NSA_HEADS)
    slope = [jnp.where(g == 0, jnp.float32(slopes[r] * LOG2E), jnp.float32(slopes[NSA_GROUP + r] * LOG2E))
             for r in range(NSA_GROUP)]
    gates = _sigmoid(gate_ref[...])

    for c in range(nq):
        @pl.when(qi == c)
        def _(c=c):
            w = (c + 1) * tq
            bias = _causal_bias(_dot(selected, expand_ref[:, :w]) + MASKED, tq)
            k = ks_scr[:w, :]
            v = vs_scr[:w, :]
            kpos = lax.broadcasted_iota(jnp.int32, (1, w), 1).astype(F32)
            c0 = max(c - n_win, 0)
            ww = (c + 1 - c0) * tq
            kw = kw_scr[c0 * tq:(c + 1) * tq, :]
            vw = vw_scr[c0 * tq:(c + 1) * tq, :]
            wbias = wbias_ref[:, (n_win + 1) * tq - ww:]
            kpos_w = (c0 * tq + lax.broadcasted_iota(jnp.int32, (1, ww), 1)).astype(F32)
            for r in range(NSA_GROUP):
                qr = q[:, r * HEAD_DIM:(r + 1) * HEAD_DIM]
                o_sel = _biased_attend(qr, k, v, bias, slope[r] * kpos)
                o_win = _biased_attend(qr, kw, vw, wbias, slope[r] * kpos_w)
                o_ref[:, r * HEAD_DIM:(r + 1) * HEAD_DIM] = (gates[:, 3 * r:3 * r + 1] * o_cmp[r]
                                                              + gates[:, 3 * r + 1:3 * r + 2] * o_sel
                                                              + gates[:, 3 * r + 2:3 * r + 3] * o_win)


def nsa_prompt_constants(n_sub, n_keys, tq):
    c_start = np.arange(n_sub)[None, :] * CMP_STRIDE
    s_start = np.arange(LANES)[:, None] * SEL_BLOCK
    cover_t = ((c_start < s_start + SEL_BLOCK) & (c_start + CMP_LEN > s_start)).astype(np.float32)
    cover_t[:, n_sub - 1:] = 0.0
    expand = np.where(np.arange(n_keys)[None, :] // SEL_BLOCK == np.arange(LANES)[:, None], -MASKED, 0.0)
    dist = WINDOW + np.arange(tq)[:, None] - np.arange(WINDOW + tq)[None, :]
    wbias = np.where((dist >= 0) & (dist < WINDOW), 0.0, MASKED)
    return jnp.asarray(cover_t, BF16), jnp.asarray(expand, BF16), jnp.asarray(wbias, F32)


def nsa_constants(n_sub, n_keys):
    c_start = np.arange(n_sub)[:, None] * CMP_STRIDE
    s_start = np.arange(LANES)[None, :] * SEL_BLOCK
    cover = ((c_start < s_start + SEL_BLOCK) & (c_start + CMP_LEN > s_start)).astype(np.float32)
    cover[n_sub - 1:] = 0.0
    expand = (np.arange(n_keys)[None, :] // SEL_BLOCK == np.arange(LANES)[:, None]).astype(np.float32)
    return jnp.asarray(cover, BF16), jnp.asarray(expand, BF16)


def nsa_prompt(q_all, gates, kcvc, sel_kv, win_kv, batch, seq, tq=256):
    assert WINDOW % tq == 0 and seq % tq == 0 and (seq // SEL_BLOCK) % 8 == 0
    nq = seq // tq
    n_sub = seq // CMP_STRIDE
    cover_t, expand, wbias = nsa_prompt_constants(n_sub, seq, tq)
    gw = NSA_GROUP * HEAD_DIM
    kv_spec = pl.BlockSpec((seq * 4, LANES), lambda b, g, i: (b, 0))
    const = lambda a: pl.BlockSpec(a.shape, lambda b, g, i: (0, 0))
    return pl.pallas_call(
        functools.partial(_nsa_prompt_kernel, tq=tq, seq=seq),
        grid=(batch, NSA_KV_HEADS, nq),
        in_specs=[
            pl.BlockSpec((tq, gw), lambda b, g, i: (b * nq + i, g)),
            pl.BlockSpec((tq, LANES), lambda b, g, i: (b * nq + i, g)),
            pl.BlockSpec((1, 2, 1, n_sub, HEAD_DIM), lambda b, g, i: (b, 0, g, 0, 0)),
            kv_spec, kv_spec, const(cover_t), const(expand), const(wbias),
        ],
        out_specs=pl.BlockSpec((tq, gw), lambda b, g, i: (b * nq + i, g)),
        out_shape=jax.ShapeDtypeStruct((batch * seq, NSA_HEADS * HEAD_DIM), F32),
        scratch_shapes=[pltpu.VMEM((seq, HEAD_DIM), BF16)] * 4,
        compiler_params=pltpu.CompilerParams(dimension_semantics=("parallel", "parallel", "arbitrary"),
                                             vmem_limit_bytes=VMEM_LIMIT),
        name="nsa_prompt",
    )(q_all, gates, kcvc, sel_kv, win_kv, cover_t, expand, wbias)


def _moba_select(gate, n_past, n_blocks):
    blk = lax.broadcasted_iota(jnp.int32, gate.shape, 1)
    past = blk < n_past
    score = jnp.where(past, gate, NEG_INF)
    rank = _top_rank(score, n_blocks)
    return jnp.where(past & (rank < min(MOBA_TOP_K, n_blocks)), 1.0, 0.0)


def _moba_select_t(gate_t, n_past, n_blocks):
    blk = lax.broadcasted_iota(jnp.int32, gate_t.shape, 0)
    past = blk < n_past
    score = jnp.where(past, gate_t, NEG_INF)
    rank = jnp.zeros(score.shape, F32)
    for n2 in range(n_blocks):
        row = score[n2:n2 + 1, :]
        rank = rank + jnp.where(row > score, 1.0, 0.0) + jnp.where((row == score) & (blk > n2), 1.0, 0.0)
    return jnp.where(past & (rank < min(MOBA_TOP_K, n_blocks)), 1.0, 0.0)


def _moba_prompt_kernel(q_ref, kv_ref, o_ref, kmean_scr, k_scr, v_scr, *, seq):
    tq = MOBA_BLOCK
    n_slab = 2 * MOBA_HEADS
    qi = pl.program_id(1)
    nb = seq // tq
    slopes = [s * LOG2E for s in _alibi_slopes(MOBA_HEADS)]

    @pl.when(qi == 0)
    def _():
        kmean_scr[...] = jnp.zeros(kmean_scr.shape, F32)
        for h in range(MOBA_HEADS):
            k = _kv_rows(kv_ref, 0, seq, h, n_slab)
            k_scr[h] = k.astype(BF16)
            v_scr[h] = _kv_rows(kv_ref, 0, seq, MOBA_HEADS + h, n_slab).astype(BF16)
            for n in range(nb):
                kmean_scr[h, n:n + 1, :] = jnp.mean(k[n * tq:(n + 1) * tq], axis=0, keepdims=True)

    row = lax.broadcasted_iota(jnp.int32, (tq, tq), 0)
    col = lax.broadcasted_iota(jnp.int32, (tq, tq), 1)
    own = jnp.where(row >= col, 0.0, MASKED)
    q, sel_bias = [], []
    nb8 = -(-nb // 8) * 8
    for h in range(MOBA_HEADS):
        q.append(q_ref[:, h * HEAD_DIM:(h + 1) * HEAD_DIM])
        gate_t = _dot_nt(kmean_scr[h].astype(BF16), q[h])[:nb8]
        bias_t = (1.0 - _moba_select_t(gate_t, qi, nb - 1)) * MASKED
        sel_bias.append(jnp.concatenate([bias_t, jnp.zeros((LANES - nb8, tq), F32)], axis=0).T)

    for c in range(nb):
        @pl.when(qi == c)
        def _(c=c):
            w = (c + 1) * tq
            kpos = lax.broadcasted_iota(jnp.int32, (1, w), 1).astype(F32)
            for h in range(MOBA_HEADS):
                bias = jnp.concatenate(
                    [jnp.broadcast_to(sel_bias[h][:, n:n + 1], (tq, tq)) for n in range(c)] + [own], axis=1)
                o_ref[:, h * HEAD_DIM:(h + 1) * HEAD_DIM] = _biased_attend(q[h], k_scr[h, :w, :], v_scr[h, :w, :], bias,
                                                                            slopes[h] * kpos)


def moba_prompt(q_all, q_col, moba_kv, batch, seq):
    tq = MOBA_BLOCK
    nq = seq // tq
    width = MOBA_HEADS * HEAD_DIM
    return pl.pallas_call(
        functools.partial(_moba_prompt_kernel, seq=seq),
        grid=(batch, nq),
        in_specs=[
            pl.BlockSpec((tq, width), lambda b, i: (b * nq + i, q_col)),
            pl.BlockSpec((seq * 2 * MOBA_HEADS, LANES), lambda b, i: (b, 0)),
        ],
        out_specs=pl.BlockSpec((tq, width), lambda b, i: (b * nq + i, 0)),
        out_shape=jax.ShapeDtypeStruct((batch * seq, width), F32),
        scratch_shapes=[pltpu.VMEM((MOBA_HEADS, LANES, HEAD_DIM), F32), pltpu.VMEM((MOBA_HEADS, seq, HEAD_DIM), BF16),
                        pltpu.VMEM((MOBA_HEADS, seq, HEAD_DIM), BF16)],
        compiler_params=pltpu.CompilerParams(dimension_semantics=("parallel", "arbitrary"),
                                             vmem_limit_bytes=VMEM_LIMIT),
        name="moba_prompt",
    )(q_all, moba_kv)


def _padded_tile(rows, n):
    return jnp.concatenate([rows, jnp.zeros((n - rows.shape[0], rows.shape[1]), F32)], axis=0).astype(BF16)


def _nsa_sample_kernel(pt_ref, *refs, n_pages, page, t_new, n_win):
    page_refs = refs[:n_pages]
    (q_ref, gate_ref, kcvc_ref, new_sel_ref, cwin_ref, new_win_ref, cover_ref, sbias_ref, wbias_ref, o_ref,
     kv_scr, win_scr) = refs[n_pages:]
    n_slab = 2 * NSA_KV_HEADS
    rows_pp = page * n_slab
    past = n_pages * page
    n_sub = past // CMP_STRIDE
    n_sel = -(-(past + t_new) // SEL_BLOCK)
    block_w = SEL_BLOCK * n_slab
    rows = NSA_GROUP * t_new

    for p in range(n_pages):
        kv_scr[p * rows_pp:(p + 1) * rows_pp, :] = page_refs[p][...].astype(BF16)
    kv_scr[past * n_slab:, :] = _padded_tile(new_sel_ref[...], LANES)
    win_scr[:n_win * n_slab, :] = cwin_ref[...].astype(BF16)
    win_scr[n_win * n_slab:, :] = _padded_tile(new_win_ref[...], LANES)
    q = _stack_heads(q_ref, 0, t_new, NSA_HEADS)

    tpos = past + lax.broadcasted_iota(jnp.int32, (rows, 1), 0) % t_new
    tpos_q = past + lax.broadcasted_iota(jnp.int32, (t_new, 1), 0)
    n_idx = lax.broadcasted_iota(jnp.int32, (1, n_sub), 1)
    valid = (n_idx * CMP_STRIDE + (CMP_LEN - 1) <= tpos) & (n_idx < n_sub - 1)
    o_cmp, sel_bias = [], []
    for g in range(NSA_KV_HEADS):
        o_g, imp = _cmp_branch_rows(q[g * rows:(g + 1) * rows], kcvc_ref[0, 0, g].astype(BF16),
                                    kcvc_ref[0, 1, g].astype(BF16), cover_ref[...], valid)
        imp_t = imp[0:t_new]
        for r in range(1, NSA_GROUP):
            imp_t = imp_t + imp[r * t_new:(r + 1) * t_new]
        o_cmp.append(o_g)
        sel_bias += [(1.0 - _select_blocks(imp_t, tpos_q, n_sel)) * MASKED] * NSA_GROUP
    sel_bias = jnp.concatenate(sel_bias, axis=0)

    bias = jnp.concatenate(
        [sbias_ref[:, n * block_w:(n + 1) * block_w] + sel_bias[:, n:n + 1] for n in range(past // SEL_BLOCK)]
        + [sbias_ref[:, past * n_slab:] + sel_bias[:, n_sel - 1:n_sel]], axis=1)
    o_sel = _slab_attend(q, kv_scr[...], bias, NSA_KV_HEADS)
    o_win = _slab_attend(q, win_scr[...], wbias_ref[...], NSA_KV_HEADS)

    gates = _sigmoid(gate_ref[...])
    for h in range(NSA_HEADS):
        g, r = divmod(h, NSA_GROUP)
        lane = g * LANES + 3 * r
        hs = slice(h * t_new, (h + 1) * t_new)
        o_ref[:, h * HEAD_DIM:(h + 1) * HEAD_DIM] = (gates[:, lane:lane + 1] * o_cmp[g][r * t_new:(r + 1) * t_new]
                                                      + gates[:, lane + 1:lane + 2] * o_sel[hs]
                                                      + gates[:, lane + 2:lane + 3] * o_win[hs])


def _cmp_branch_rows(q, kc, vc, cover, valid):
    s = jnp.where(valid, _dot_nt(q, kc) * SCALE, NEG_INF)
    e = jnp.where(valid, jnp.exp(s - jnp.max(s, axis=-1, keepdims=True)), 0.0)
    d = jnp.sum(e, axis=-1, keepdims=True)
    p = e / jnp.where(d > 0, d, 1.0)
    p_hi = p.astype(BF16)
    p_lo = (p - p_hi.astype(F32)).astype(BF16)
    return _dot(p_hi, vc), _dot(p_hi, cover) + _dot(p_lo, cover)


def nsa_sample(q_all, gates, kcvc, sel_pool, page_table, page, new_sel, cache_win, new_win, t_new):
    nb, n_pages = page_table.shape
    past = n_pages * page
    n_sub = past // CMP_STRIDE
    n_slab = 2 * NSA_KV_HEADS
    n_win = cache_win.shape[0] // (n_slab * nb)
    assert past % SEL_BLOCK == 0 and t_new * n_slab <= LANES and past // SEL_BLOCK < LANES
    cover, _ = nsa_constants(n_sub, LANES)
    new_keys = LANES // n_slab
    slopes = _alibi_slopes(NSA_HEADS)
    head_slab = [h // NSA_GROUP for h in range(NSA_HEADS)]
    t_pos = past + np.arange(t_new)
    causal = lambda pos, real: (pos[None, :] <= t_pos[:, None]) & real[None, :]
    sel_pos = np.arange(past + new_keys)
    sbias = _slab_bias(head_slab, slopes, sel_pos, causal(sel_pos, sel_pos < past + t_new), n_slab, t_new)
    win_pos = past - n_win + np.arange(n_win + new_keys)
    in_window = (t_pos[:, None] - win_pos[None, :] < WINDOW) & (win_pos[None, :] >= 0)
    wbias = _slab_bias(head_slab, slopes, win_pos, causal(win_pos, win_pos < past + t_new) & in_window, n_slab, t_new)
    width = NSA_HEADS * HEAD_DIM
    const = lambda a: pl.BlockSpec(a.shape, lambda b, pt: (0,) * a.ndim)
    rows = lambda n: pl.BlockSpec((n_slab * n, LANES), lambda b, pt: (b, 0))
    grid_spec = pltpu.PrefetchScalarGridSpec(
        num_scalar_prefetch=1,
        grid=(nb,),
        in_specs=_page_specs(n_pages, n_slab * page) + [
            pl.BlockSpec((t_new, width), lambda b, pt: (b, 0)),
            pl.BlockSpec((t_new, NSA_KV_HEADS * LANES), lambda b, pt: (b, 0)),
            pl.BlockSpec((1, 2, 2, n_sub, HEAD_DIM), lambda b, pt: (b, 0, 0, 0, 0)),
            rows(t_new), rows(n_win), rows(t_new),
            const(cover), const(sbias), const(wbias)],
        out_specs=pl.BlockSpec((t_new, width), lambda b, pt: (b, 0)),
        scratch_shapes=[pltpu.VMEM((past * n_slab + LANES, HEAD_DIM), BF16),
                        pltpu.VMEM((n_win * n_slab + LANES, HEAD_DIM), BF16)],
    )
    return pl.pallas_call(
        functools.partial(_nsa_sample_kernel, n_pages=n_pages, page=page, t_new=t_new, n_win=n_win),
        grid_spec=grid_spec,
        out_shape=jax.ShapeDtypeStruct((nb * t_new, width), F32),
        compiler_params=_params(1, VMEM_LIMIT),
        name="nsa_sample",
    )(page_table.reshape(-1), *([sel_pool] * n_pages), q_all, gates, kcvc, new_sel, cache_win, new_win, cover, sbias,
      wbias)


def _slab_attend(q, kv, bias, shift):
    s = _dot_nt(q, kv) * (SCALE * LOG2E) + bias
    e = jnp.exp2(s - jnp.max(s, axis=-1, keepdims=True))
    d = jnp.sum(e, axis=-1, keepdims=True)
    return _dot(pltpu.roll(e, shift, axis=1).astype(BF16), kv) / jnp.where(d > 0, d, 1.0)


def _stack_heads(ref, row0, n_rows, n_heads):
    return jnp.concatenate([ref[row0:row0 + n_rows, h * HEAD_DIM:(h + 1) * HEAD_DIM] for h in range(n_heads)],
                           axis=0).astype(BF16)


def _slab_bias(head_slab, slopes, key_pos, key_ok, n_slab, t_new):
    n_heads, n_keys = len(head_slab), key_pos.shape[0]
    slab = np.arange(n_keys * n_slab) % n_slab
    out = np.full((n_heads, t_new, n_keys * n_slab), MASKED, np.float64)
    for h in range(n_heads):
        ok = np.repeat(key_ok, n_slab, axis=1) & (slab == head_slab[h])[None, :]
        out[h] = np.where(ok, slopes[h] * LOG2E * np.repeat(key_pos, n_slab)[None, :], MASKED)
    return jnp.asarray(out.reshape(n_heads * t_new, n_keys * n_slab), F32)


def _moba_sample_kernel(pt_ref, *refs, n_pages, page, t_new):
    page_refs = refs[:n_pages]
    q_ref, new_ref, bias_ref, o_ref, kv_scr, mean_scr = refs[n_pages:]
    n_slab = 2 * MOBA_HEADS
    rows_pp = page * n_slab
    past = n_pages * page
    pages_per_block = MOBA_BLOCK // page
    n_past = n_pages // pages_per_block
    block_w = MOBA_BLOCK * n_slab

    mean_scr[...] = jnp.zeros(mean_scr.shape, F32)
    for n in range(n_past):
        total = None
        for p in range(n * pages_per_block, (n + 1) * pages_per_block):
            x = page_refs[p][...]
            kv_scr[p * rows_pp:(p + 1) * rows_pp, :] = x.astype(BF16)
            part = jnp.sum(x.reshape(page, n_slab, LANES), axis=0)
            total = part if total is None else total + part
        mean_scr[n * n_slab:(n + 1) * n_slab, :] = total * (1.0 / MOBA_BLOCK)
    x_new = new_ref[...]
    kv_scr[past * n_slab:, :] = _padded_tile(x_new, LANES)
    mean_scr[n_past * n_slab:(n_past + 1) * n_slab, :] = (
        jnp.sum(x_new.reshape(t_new, n_slab, LANES), axis=0) * (1.0 / MOBA_BLOCK))

    tpos = past + lax.broadcasted_iota(jnp.int32, (t_new, 1), 0)
    cur = tpos // MOBA_BLOCK
    blk = lax.broadcasted_iota(jnp.int32, (t_new, LANES), 1)
    sel_bias = []
    for h in range(MOBA_HEADS):
        q = q_ref[:, h * HEAD_DIM:(h + 1) * HEAD_DIM].astype(BF16)
        kmean = mean_scr[pl.ds(h, LANES // n_slab, stride=n_slab), :]
        kmean = jnp.concatenate([kmean, jnp.zeros((LANES - kmean.shape[0], LANES), F32)], axis=0)
        gate = _dot_nt(q, kmean.astype(BF16))
        selected = jnp.where(blk == cur, 1.0, _moba_select(gate, cur, n_past))
        sel_bias.append((1.0 - selected) * MASKED)
    sel_bias = jnp.concatenate(sel_bias, axis=0)

    bias = jnp.concatenate(
        [bias_ref[:, n * block_w:(n + 1) * block_w] + sel_bias[:, n:n + 1] for n in range(n_past)]
        + [bias_ref[:, n_past * block_w:] + sel_bias[:, n_past:n_past + 1]], axis=1)
    o = _slab_attend(_stack_heads(q_ref, 0, t_new, MOBA_HEADS), kv_scr[...], bias, MOBA_HEADS)
    for h in range(MOBA_HEADS):
        o_ref[:, h * HEAD_DIM:(h + 1) * HEAD_DIM] = o[h * t_new:(h + 1) * t_new]


def moba_sample(q_all, q_col, moba_pool, page_table, page, new_kv, t_new):
    nb, n_pages = page_table.shape
    past = n_pages * page
    width = MOBA_HEADS * HEAD_DIM
    n_slab = 2 * MOBA_HEADS
    assert past % MOBA_BLOCK == 0 and t_new * n_slab <= LANES and past // MOBA_BLOCK < LANES // n_slab
    new_keys = LANES // n_slab
    key_pos = np.arange(past + new_keys)
    t_pos = past + np.arange(t_new)
    key_ok = (key_pos[None, :] <= t_pos[:, None]) & (key_pos[None, :] < past + t_new)
    bias = _slab_bias(list(range(MOBA_HEADS)), _alibi_slopes(MOBA_HEADS), key_pos, key_ok, n_slab, t_new)
    grid_spec = pltpu.PrefetchScalarGridSpec(
        num_scalar_prefetch=1,
        grid=(nb,),
        in_specs=_page_specs(n_pages, n_slab * page) + [
            pl.BlockSpec((t_new, width), lambda b, pt: (b, q_col)),
            pl.BlockSpec((t_new * n_slab, LANES), lambda b, pt: (b, 0)),
            pl.BlockSpec(bias.shape, lambda b, pt: (0, 0))],
        out_specs=pl.BlockSpec((t_new, width), lambda b, pt: (b, 0)),
        scratch_shapes=[pltpu.VMEM((past * n_slab + LANES, HEAD_DIM), BF16), pltpu.VMEM((LANES, LANES), F32)],
    )
    return pl.pallas_call(
        functools.partial(_moba_sample_kernel, n_pages=n_pages, page=page, t_new=t_new),
        grid_spec=grid_spec,
        out_shape=jax.ShapeDtypeStruct((nb * t_new, width), F32),
        compiler_params=_params(1, VMEM_LIMIT),
        name="moba_sample",
    )(page_table.reshape(-1), *([moba_pool] * n_pages), q_all, new_kv, bias)


def _cross_kernel(q_ref, kv_ref, o_ref, *, n_mem):
    n_slab = 2 * MEM_HEADS
    for h in range(MEM_HEADS):
        k = _kv_rows(kv_ref, 0, n_mem, h, n_slab).astype(BF16)
        v = _kv_rows(kv_ref, 0, n_mem, MEM_HEADS + h, n_slab).astype(BF16)
        s = _dot_nt(q_ref[:, h * HEAD_DIM:(h + 1) * HEAD_DIM], k) * SCALE
        e = jnp.exp(s - jnp.max(s, axis=-1, keepdims=True))
        p = e / jnp.sum(e, axis=-1, keepdims=True)
        o_ref[:, h * HEAD_DIM:(h + 1) * HEAD_DIM] = _dot(p.astype(BF16), v)


def cross_attend(q_all, q_col, mem_kv, batch, rows, tq):
    nq = rows // tq
    n_slab = 2 * MEM_HEADS
    width = MEM_HEADS * HEAD_DIM
    n_mem = mem_kv.shape[0] // (batch * n_slab)
    return pl.pallas_call(
        functools.partial(_cross_kernel, n_mem=n_mem),
        grid=(batch, nq),
        in_specs=[
            pl.BlockSpec((tq, width), lambda b, i: (b * nq + i, q_col)),
            pl.BlockSpec((n_mem * n_slab, LANES), lambda b, i: (b, 0)),
        ],
        out_specs=pl.BlockSpec((tq, width), lambda b, i: (b * nq + i, 0)),
        out_shape=jax.ShapeDtypeStruct((batch * rows, width), F32),
        compiler_params=_params(2, VMEM_LIMIT),
        name="cross_attend",
    )(q_all, mem_kv)


def _cross_sample_kernel(q_ref, kv_ref, bias_ref, o_ref, *, n_mem, t_new, bb):
    n_slab = 2 * MEM_HEADS
    rows = n_mem * n_slab
    for i in range(bb):
        q = _stack_heads(q_ref, i * t_new, t_new, MEM_HEADS)
        o = _slab_attend(q, kv_ref[i * rows:(i + 1) * rows, :].astype(BF16), bias_ref[...], MEM_HEADS)
        for h in range(MEM_HEADS):
            o_ref[i * t_new:(i + 1) * t_new, h * HEAD_DIM:(h + 1) * HEAD_DIM] = o[h * t_new:(h + 1) * t_new]


def cross_attend_sample(q_all, q_col, mem_kv, batch, t_new, bb):
    assert batch % bb == 0
    n_slab = 2 * MEM_HEADS
    width = MEM_HEADS * HEAD_DIM
    n_mem = mem_kv.shape[0] // (batch * n_slab)
    bias = _slab_bias(list(range(MEM_HEADS)), [0.0] * MEM_HEADS, np.arange(n_mem), np.ones((t_new, n_mem), bool),
                      n_slab, t_new)
    return pl.pallas_call(
        functools.partial(_cross_sample_kernel, n_mem=n_mem, t_new=t_new, bb=bb),
        grid=(batch // bb,),
        in_specs=[
            pl.BlockSpec((bb * t_new, width), lambda b: (b, q_col)),
            pl.BlockSpec((bb * n_mem * n_slab, LANES), lambda b: (b, 0)),
            pl.BlockSpec(bias.shape, lambda b: (0, 0)),
        ],
        out_specs=pl.BlockSpec((bb * t_new, width), lambda b: (b, 0)),
        out_shape=jax.ShapeDtypeStruct((batch * t_new, width), F32),
        compiler_params=_params(1, VMEM_LIMIT),
        name="cross_sample",
    )(q_all, mem_kv, bias)


def _merge_kernel(x_ref, on_ref, om_ref, oc_ref, z_ref, gl_ref, wn_ref, wm_ref, wc_ref, wo_ref, fg_ref, y_ref, mix_scr,
                  *, tn):
    d = x_ref.shape[1]
    wn_k, wm_k = wn_ref.shape[0], wm_ref.shape[0]
    z = z_ref[...]
    act = z * _sigmoid(z)
    un = (on_ref[...] * act[:, :wn_k]).astype(BF16)
    um = (om_ref[...] * act[:, wn_k:wn_k + wm_k]).astype(BF16)
    uc = (oc_ref[...] * act[:, wn_k + wm_k:]).astype(BF16)
    for c in range(d // tn):
        cols = slice(c * tn, (c + 1) * tn)
        mix = (_sigmoid(gl_ref[:, c * tn:(c + 1) * tn]) * _dot(un, wn_ref[:, cols])
               + _sigmoid(gl_ref[:, d + c * tn:d + (c + 1) * tn]) * _dot(um, wm_ref[:, cols])
               + _sigmoid(gl_ref[:, 2 * d + c * tn:2 * d + (c + 1) * tn]) * _dot(uc, wc_ref[:, cols]))
        mix_scr[:, cols] = mix.astype(BF16)
    y = x_ref[...] + _dot(mix_scr[...], wo_ref[...])
    inv = lax.rsqrt(jnp.mean(y * y, axis=-1, keepdims=True) + RMS_EPS)
    y_ref[...] = y * inv * fg_ref[...]


def merge_branches(x, o_nsa, o_moba, o_mem, z_all, gate_logits, w_nsa, w_moba, w_mem, w_out, final_g, tm=256, tn=512):
    m, d = x.shape
    tm = min(tm, m)
    row = lambda a: pl.BlockSpec((tm, a.shape[1]), lambda i: (i, 0))
    whole = lambda a: pl.BlockSpec(a.shape, lambda i: (0, 0), pipeline_mode=pl.Buffered(1))
    final_g = final_g.reshape(1, d)
    return pl.pallas_call(
        functools.partial(_merge_kernel, tn=tn),
        grid=(m // tm,),
        in_specs=[row(x), row(o_nsa), row(o_moba), row(o_mem), row(z_all), row(gate_logits),
                  whole(w_nsa), whole(w_moba), whole(w_mem), whole(w_out), whole(final_g)],
        out_specs=row(x),
        out_shape=jax.ShapeDtypeStruct((m, d), F32),
        scratch_shapes=[pltpu.VMEM((tm, d), BF16)],
        compiler_params=_params(1, VMEM_LIMIT),
        name="merge",
    )(x, o_nsa, o_moba, o_mem, z_all, gate_logits, w_nsa, w_moba, w_mem, w_out, final_g)


def _in_weights(w_in, d_model):
    splits = (NSA_HEADS * HEAD_DIM, 4 * HEAD_DIM, 4 * HEAD_DIM, 4 * HEAD_DIM, 3 * NSA_HEADS, NSA_HEADS * HEAD_DIM,
              MOBA_HEADS * HEAD_DIM, 2 * MOBA_HEADS * HEAD_DIM, MOBA_HEADS * HEAD_DIM, MEM_HEADS * HEAD_DIM,
              MEM_HEADS * HEAD_DIM, N_BRANCH * d_model)
    offs = np.cumsum((0,) + splits)
    assert offs[-1] == w_in.shape[1]
    (nsa_q, cmp_kv, sel_kv, win_kv, nsa_g, nsa_z, moba_q, moba_kv, moba_z, mem_q, mem_z, merge_g) = (
        w_in[:, offs[i]:offs[i + 1]] for i in range(len(splits)))
    per_group = 3 * NSA_GROUP
    gate_w = jnp.pad(nsa_g.reshape(-1, NSA_KV_HEADS, per_group), ((0, 0), (0, 0), (0, LANES - per_group)))
    cast = lambda w: w.astype(BF16)
    return dict(q=cast(jnp.concatenate([nsa_q, moba_q, mem_q], axis=1)), cmp=cast(cmp_kv), sel=cast(sel_kv),
                win=cast(win_kv), moba=cast(moba_kv), gate=cast(gate_w.reshape(-1, NSA_KV_HEADS * LANES)),
                z=cast(jnp.concatenate([nsa_z, moba_z, mem_z], axis=1)), merge=cast(merge_g))


def _in_projection(x, norm_g, w, q_dtype):
    tm = min(1024, x.shape[0] // 2)
    xn = rmsnorm_bf16(x, norm_g, 512)
    mm = lambda name, dtype: matmul(xn, w[name], dtype, tm, 1024, "in_" + name)
    rows = lambda name: matmul_rows(xn, w[name], tm, "in_" + name)
    return dict(q=mm("q", q_dtype), cmp=rows("cmp"), sel=rows("sel"), win=rows("win"), moba=rows("moba"),
                gate=mm("gate", F32), z=mm("z", F32), merge=mm("merge", F32))


def kernel(x_prompt, x_sample, cache_cmp_kv, cache_sel_kv, cache_moba_kv, cache_win_kv, cache_mem_kv, page_table,
           mem_prompt, norm_g, w_in, phi_pos, phi_w1, phi_w2, mem_norm_g, w_mem_kv, w_br_nsa, w_br_moba, w_br_mem,
           w_out, final_norm_g):
    batch, seq, d_model = x_prompt.shape
    nb, t_new, _ = x_sample.shape
    n_mem = mem_prompt.shape[1]
    page = cache_cmp_kv.shape[2]
    n_win = cache_win_kv.shape[2]
    assert norm_g.shape[0] == 1, "one layer"
    assert seq % MOBA_BLOCK == 0 and (page_table.shape[1] * page) % MOBA_BLOCK == 0 and MOBA_BLOCK % page == 0
    l = 0
    q_moba_col = NSA_HEADS * HEAD_DIM // (MOBA_HEADS * HEAD_DIM)
    q_mem_col = (NSA_HEADS + MOBA_HEADS) * HEAD_DIM // (MEM_HEADS * HEAD_DIM)
    lane_rows = lambda a: a.reshape(-1, LANES)

    w = _in_weights(w_in[l], d_model)
    w1cat, pos8, w2 = compress_weights(phi_pos[l], phi_w1[l], phi_w2[l])
    br = [a[l].astype(BF16) for a in (w_br_nsa, w_br_moba, w_br_mem, w_out)]

    xp = x_prompt.reshape(batch * seq, d_model)
    hp = _in_projection(xp, norm_g[l], w, BF16)
    mem_n = rmsnorm_bf16(mem_prompt.reshape(batch * n_mem, d_model), mem_norm_g[l], 512)
    mem_kv_p = matmul_rows(mem_n, w_mem_kv[l].astype(BF16), 1024, "mem_kv")
    pages_p = seq // page
    identity_table = jnp.arange(batch * pages_p, dtype=jnp.int32).reshape(batch, pages_p)
    kcvc_p = compress_tokens(hp["cmp"], identity_table, page, w1cat, pos8, w2)
    o_nsa_p = nsa_prompt(hp["q"], hp["gate"], kcvc_p, hp["sel"], hp["win"], batch, seq)
    o_moba_p = moba_prompt(hp["q"], q_moba_col, hp["moba"], batch, seq)
    o_mem_p = cross_attend(hp["q"], q_mem_col, mem_kv_p, batch, seq, min(seq, 1024))
    y_prompt = merge_branches(xp, o_nsa_p, o_moba_p, o_mem_p, hp["z"], hp["merge"], *br, final_norm_g)

    xs = x_sample.reshape(nb * t_new, d_model)
    hs = _in_projection(xs, norm_g[l], w, F32)
    kcvc_s = compress_tokens(lane_rows(cache_cmp_kv[l]), page_table, page, w1cat, pos8, w2)
    o_nsa_s = nsa_sample(hs["q"], hs["gate"], kcvc_s, lane_rows(cache_sel_kv[l]), page_table, page, hs["sel"],
                         lane_rows(cache_win_kv[l]), hs["win"], t_new)
    o_moba_s = moba_sample(hs["q"], q_moba_col, lane_rows(cache_moba_kv[l]), page_table, page, hs["moba"], t_new)
    o_mem_s = cross_attend_sample(hs["q"], q_mem_col, lane_rows(cache_mem_kv[l]), nb, t_new, 4 if nb % 4 == 0 else 1)
    y_sample = merge_branches(xs, o_nsa_s, o_moba_s, o_mem_s, hs["z"], hs["merge"], *br, final_norm_g)

    nsa_shape = lambda a, b_, t_: a.reshape(1, b_, t_, 2, NSA_KV_HEADS, HEAD_DIM)
    moba_shape = lambda a, b_, t_: a.reshape(1, b_, t_, 2, MOBA_HEADS, HEAD_DIM)
    keep_p = min(WINDOW, seq)
    win_all = jnp.concatenate([cache_win_kv[l], nsa_shape(hs["win"], nb, t_new)[0]], axis=1)
    keep_s = min(WINDOW, n_win + t_new)
    return (y_prompt.reshape(batch, seq, d_model), y_sample.reshape(nb, t_new, d_model),
            nsa_shape(hp["cmp"], batch, seq), nsa_shape(hp["sel"], batch, seq), moba_shape(hp["moba"], batch, seq),
            nsa_shape(hp["win"], batch, seq)[:, :, seq - keep_p:],
            mem_kv_p.reshape(1, batch, n_mem, 2, MEM_HEADS, HEAD_DIM),
            nsa_shape(hs["cmp"], nb, t_new), nsa_shape(hs["sel"], nb, t_new), moba_shape(hs["moba"], nb, t_new),
            win_all[None, :, n_win + t_new - keep_s:])
```

```python
import functools

import numpy as np
import jax
import jax.numpy as jnp
from jax import lax
from jax.experimental import pallas as pl
from jax.experimental.pallas import tpu as pltpu

HEAD_DIM = 128
NSA_HEADS = 8
NSA_KV_HEADS = 2
NSA_GROUP = NSA_HEADS // NSA_KV_HEADS
CMP_LEN = 32
CMP_STRIDE = 16
SEL_BLOCK = 64
SEL_TOP_N = 16
WINDOW = 512
FORCE_BONUS = 1.0e4
MOBA_HEADS = 4
MOBA_BLOCK = 256
MOBA_TOP_K = 3
MEM_HEADS = 4
N_BRANCH = 3
RMS_EPS = 1e-6
NEG_INF = -1e30
SCALE = HEAD_DIM ** -0.5
LOG2E = 1.4426950408889634
MASKED = -(2.0 ** 100)

LANES = 128
STAGE_PITCH = 24
VMEM_LIMIT = 56 * 1024 * 1024

F32 = jnp.float32
BF16 = jnp.bfloat16


def _params(n_axes, vmem=None):
    return pltpu.CompilerParams(dimension_semantics=("parallel",) * n_axes, vmem_limit_bytes=vmem)


def _dot(a, b):
    return jnp.dot(a, b, preferred_element_type=F32)


def _dot_nt(a, b):
    return lax.dot_general(a, b, (((1,), (1,)), ((), ())), preferred_element_type=F32)


def _sigmoid(x):
    return 1.0 / (1.0 + jnp.exp(-x))


def _rms_kernel(x_ref, g_ref, o_ref):
    x = x_ref[...]
    inv = lax.rsqrt(jnp.mean(x * x, axis=-1, keepdims=True) + RMS_EPS)
    o_ref[...] = (x * inv * g_ref[...]).astype(o_ref.dtype)


def rmsnorm_bf16(x, g, tm):
    n, d = x.shape
    return pl.pallas_call(
        _rms_kernel,
        grid=(n // tm,),
        in_specs=[pl.BlockSpec((tm, d), lambda i: (i, 0)), pl.BlockSpec((1, d), lambda i: (0, 0))],
        out_specs=pl.BlockSpec((tm, d), lambda i: (i, 0)),
        out_shape=jax.ShapeDtypeStruct((n, d), BF16),
        compiler_params=_params(1, VMEM_LIMIT),
        name="rmsnorm",
    )(x, g.reshape(1, d))


def _mm_kernel(x_ref, w_ref, o_ref):
    o_ref[...] = _dot(x_ref[...], w_ref[...]).astype(o_ref.dtype)


def matmul(x, w, out_dtype, tm, tn, name):
    m, k = x.shape
    n = w.shape[1]
    tm, tn = min(tm, m), min(tn, n)
    return pl.pallas_call(
        _mm_kernel,
        grid=(m // tm, n // tn),
        in_specs=[pl.BlockSpec((tm, k), lambda i, j: (i, 0)), pl.BlockSpec((k, tn), lambda i, j: (0, j))],
        out_specs=pl.BlockSpec((tm, tn), lambda i, j: (i, j)),
        out_shape=jax.ShapeDtypeStruct((m, n), out_dtype),
        compiler_params=_params(2, VMEM_LIMIT),
        name=name,
    )(x, w)


def _mm_rows_kernel(x_ref, w_ref, o_ref):
    tm = x_ref.shape[0]
    n_slab = w_ref.shape[1] // LANES
    res = _dot(x_ref[...], w_ref[...])
    for j in range(n_slab):
        o_ref[pl.ds(j, tm, stride=n_slab), :] = res[:, j * LANES:(j + 1) * LANES]


def matmul_rows(x, w, tm, name):
    m, k = x.shape
    n = w.shape[1]
    n_slab = n // LANES
    tm = min(tm, m)
    return pl.pallas_call(
        _mm_rows_kernel,
        grid=(m // tm,),
        in_specs=[pl.BlockSpec((tm, k), lambda i: (i, 0)), pl.BlockSpec((k, n), lambda i: (0, 0))],
        out_specs=pl.BlockSpec((tm * n_slab, LANES), lambda i: (i, 0)),
        out_shape=jax.ShapeDtypeStruct((m * n_slab, LANES), F32),
        compiler_params=_params(1, VMEM_LIMIT),
        name=name,
    )(x, w)


def _compress_kernel(pt_ref, *refs, n_pages, page):
    page_refs = refs[:n_pages]
    w1_ref, pos_ref, w2_ref, out_ref, lhs_scr, stage_scr = refs[n_pages:]
    sub_per_page = page // CMP_STRIDE
    n_sub = n_pages * sub_per_page
    for p in range(n_pages):
        for cg in range(4):
            stage = stage_scr.at[(p % 2) * 4 + cg]
            for j in range(sub_per_page):
                stage[j * STAGE_PITCH:j * STAGE_PITCH + CMP_STRIDE, :] = (
                    page_refs[p][pl.ds(4 * CMP_STRIDE * j + cg, CMP_STRIDE, stride=4), :])
            for s in range(CMP_STRIDE):
                lhs_scr[cg, p * sub_per_page:(p + 1) * sub_per_page, s * HEAD_DIM:(s + 1) * HEAD_DIM] = (
                    stage[pl.ds(s, sub_per_page, stride=STAGE_PITCH), :])
    row = lax.broadcasted_iota(jnp.int32, (n_sub, HEAD_DIM), 0)
    for c in range(2):
        w1 = w1_ref[c]
        bias = _dot(pos_ref[c], w1)
        b0 = bias[0:1, :HEAD_DIM]
        b1 = bias[1:2, HEAD_DIM:]
        for g in range(2):
            a = _dot(lhs_scr[c * 2 + g].astype(BF16), w1)
            a0 = a[:, :HEAD_DIM] + b0
            a1 = a[:, HEAD_DIM:] + b1
            hid = a0 + pltpu.roll(a1, shift=n_sub - 1, axis=0)
            act = hid * _sigmoid(hid)
            o = _dot(act.astype(BF16), w2_ref[c])
            out_ref[0, c, g] = jnp.where(row < n_sub - 1, o, 0.0)


def _page_specs(n_pages, rows):
    return [pl.BlockSpec((rows, LANES), functools.partial(lambda b, pt, p: (pt[b * n_pages + p], 0), p=p))
            for p in range(n_pages)]


def compress_tokens(pool, page_table, page, w1cat, pos8, w2):
    nb, n_pages = page_table.shape
    n_sub = n_pages * page // CMP_STRIDE
    const = lambda shape: pl.BlockSpec(shape, lambda b, pt: (0,) * len(shape))
    grid_spec = pltpu.PrefetchScalarGridSpec(
        num_scalar_prefetch=1,
        grid=(nb,),
        in_specs=_page_specs(n_pages, 4 * page) + [const(w1cat.shape), const(pos8.shape), const(w2.shape)],
        out_specs=pl.BlockSpec((1, 2, 2, n_sub, HEAD_DIM), lambda b, pt: (b, 0, 0, 0, 0)),
        scratch_shapes=[pltpu.VMEM((4, n_sub, CMP_STRIDE * HEAD_DIM), F32),
                        pltpu.VMEM((8, page // CMP_STRIDE * STAGE_PITCH, LANES), F32)],
    )
    return pl.pallas_call(
        functools.partial(_compress_kernel, n_pages=n_pages, page=page),
        grid_spec=grid_spec,
        out_shape=jax.ShapeDtypeStruct((nb, 2, 2, n_sub, HEAD_DIM), F32),
        compiler_params=_params(1, VMEM_LIMIT),
        name="compress",
    )(page_table.reshape(-1), *([pool] * n_pages), w1cat, pos8, w2)


def compress_weights(phi_pos, phi_w1, phi_w2):
    n_half = CMP_LEN // CMP_STRIDE
    w1 = phi_w1.reshape(2, n_half, CMP_STRIDE, HEAD_DIM, HEAD_DIM).transpose(0, 2, 3, 1, 4)
    w1cat = w1.reshape(2, CMP_STRIDE * HEAD_DIM, n_half * HEAD_DIM).astype(BF16)
    pos = phi_pos.reshape(2, n_half, CMP_STRIDE * HEAD_DIM)
    pos8 = jnp.pad(pos, ((0, 0), (0, 8 - n_half), (0, 0))).astype(BF16)
    return w1cat, pos8, phi_w2.astype(BF16)


def _softmax_attend(s, v):
    e = jnp.exp2(s - jnp.max(s, axis=-1, keepdims=True))
    d = jnp.sum(e, axis=-1, keepdims=True)
    return _dot(e.astype(BF16), v) / jnp.where(d > 0, d, 1.0)


def _biased_attend(q, k, v, bias, key_slope):
    return _softmax_attend(_dot_nt(q, k) * (SCALE * LOG2E) + key_slope + bias, v)


def _top_rank(score, n_blocks):
    blk = lax.broadcasted_iota(jnp.int32, score.shape, 1)
    rank = jnp.zeros(score.shape, F32)
    for n2 in range(n_blocks):
        col = score[:, n2:n2 + 1]
        rank = rank + jnp.where(col > score, 1.0, 0.0) + jnp.where((col == score) & (blk > n2), 1.0, 0.0)
    return rank


def _alibi_slopes(n_heads):
    return [float(2.0 ** (-8.0 * (h + 1) / n_heads)) for h in range(n_heads)]


def _masked_softmax(s, valid, axis):
    s = jnp.where(valid, s, NEG_INF)
    e = jnp.where(valid, jnp.exp(s - jnp.max(s, axis=axis, keepdims=True)), 0.0)
    d = jnp.sum(e, axis=axis, keepdims=True)
    return e / jnp.where(d > 0, d, 1.0)


def _cmp_branch(q, kc, vc, cover_t, valid, valid_t):
    imp_t = None
    outs = []
    for r in range(NSA_GROUP):
        qr = q[:, r * HEAD_DIM:(r + 1) * HEAD_DIM]
        p = _masked_softmax(_dot_nt(qr, kc) * SCALE, valid, -1)
        outs.append(_dot(p.astype(BF16), vc))
        p_t = _masked_softmax(_dot_nt(kc, qr) * SCALE, valid_t, 0)
        p_hi = p_t.astype(BF16)
        p_lo = (p_t - p_hi.astype(F32)).astype(BF16)
        part = _dot(cover_t, p_hi) + _dot(cover_t, p_lo)
        imp_t = part if imp_t is None else imp_t + part
    return outs, imp_t


def _select_blocks_t(imp_t, tpos_row, n_sel):
    blk = lax.broadcasted_iota(jnp.int32, imp_t.shape, 0)
    cur = tpos_row // SEL_BLOCK
    forced = (blk == 0) | (blk == cur) | (blk == cur - 1)
    score = jnp.where(blk * SEL_BLOCK <= tpos_row, imp_t + FORCE_BONUS * jnp.where(forced, 1.0, 0.0), NEG_INF)
    rank = jnp.zeros(score.shape, F32)
    for n2 in range(n_sel):
        row = score[n2:n2 + 1, :]
        rank = rank + jnp.where(row > score, 1.0, 0.0) + jnp.where((row == score) & (blk > n2), 1.0, 0.0)
    return jnp.where(rank < min(SEL_TOP_N, n_sel), 1.0, 0.0)


def _select_blocks(imp, tpos, n_sel):
    blk = lax.broadcasted_iota(jnp.int32, imp.shape, 1)
    cur = tpos // SEL_BLOCK
    forced = (blk == 0) | (blk == cur) | (blk == cur - 1)
    allowed = (blk * SEL_BLOCK <= tpos) & (blk < n_sel)
    score = jnp.where(allowed, imp + FORCE_BONUS * jnp.where(forced, 1.0, 0.0), NEG_INF)
    rank = _top_rank(score, n_sel)
    return jnp.where((rank < min(SEL_TOP_N, n_sel)) & (blk < n_sel), 1.0, 0.0)


def _kv_rows(ref, token0, n_tokens, slab, n_slab):
    return ref[pl.ds(token0 * n_slab + slab, n_tokens, stride=n_slab), :]


def _causal_bias(bias, tq):
    w = bias.shape[1]
    row = lax.broadcasted_iota(jnp.int32, (tq, tq), 0)
    col = lax.broadcasted_iota(jnp.int32, (tq, tq), 1)
    diag = jnp.where(row >= col, bias[:, w - tq:], MASKED)
    return diag if w == tq else jnp.concatenate([bias[:, :w - tq], diag], axis=1)


def _nsa_prompt_kernel(q_ref, gate_ref, kcvc_ref, sel_ref, win_ref, cover_ref, expand_ref, wbias_ref, o_ref,
                       ks_scr, vs_scr, kw_scr, vw_scr, *, tq, seq):
    g = pl.program_id(1)
    qi = pl.program_id(2)
    nq = seq // tq

    @pl.when(qi == 0)
    def _():
        ks_scr[...] = _kv_rows(sel_ref, 0, seq, g, 4).astype(BF16)
        vs_scr[...] = _kv_rows(sel_ref, 0, seq, 2 + g, 4).astype(BF16)
        kw_scr[...] = _kv_rows(win_ref, 0, seq, g, 4).astype(BF16)
        vw_scr[...] = _kv_rows(win_ref, 0, seq, 2 + g, 4).astype(BF16)

    n_sel = seq // SEL_BLOCK
    n_sub = seq // CMP_STRIDE
    n_win = WINDOW // tq
    tpos = qi * tq + lax.broadcasted_iota(jnp.int32, (tq, 1), 0)
    tpos_row = qi * tq + lax.broadcasted_iota(jnp.int32, (1, tq), 1)
    q = q_ref[...]

    n_idx = lax.broadcasted_iota(jnp.int32, (1, n_sub), 1)
    valid = (n_idx * CMP_STRIDE + (CMP_LEN - 1) <= tpos) & (n_idx < n_sub - 1)
    n_idx_t = lax.broadcasted_iota(jnp.int32, (n_sub, 1), 0)
    valid_t = (n_idx_t * CMP_STRIDE + (CMP_LEN - 1) <= tpos_row) & (n_idx_t < n_sub - 1)
    o_cmp, imp_t = _cmp_branch(q, kcvc_ref[0, 0, 0].astype(BF16), kcvc_ref[0, 1, 0].astype(BF16), cover_ref[...],
                               valid, valid_t)
    sel_t = _select_blocks_t(imp_t[:n_sel], tpos_row, n_sel)
    selected = jnp.concatenate([sel_t, jnp.zeros((LANES - n_sel, tq), F32)], axis=0).T.astype(BF16)

    slopes = _alibi_slopes(NSA_HEADS)
    slope = [jnp.where(g == 0, jnp.float32(slopes[r] * LOG2E), jnp.float32(slopes[NSA_GROUP + r] * LOG2E))
             for r in range(NSA_GROUP)]
    gates = _sigmoid(gate_ref[...])

    for c in range(nq):
        @pl.when(qi == c)
        def _(c=c):
            w = (c + 1) * tq
            bias = _causal_bias(_dot(selected, expand_ref[:, :w]) + MASKED, tq)
            k = ks_scr[:w, :]
            v = vs_scr[:w, :]
            kpos = lax.broadcasted_iota(jnp.int32, (1, w), 1).astype(F32)
            c0 = max(c - n_win, 0)
            ww = (c + 1 - c0) * tq
            kw = kw_scr[c0 * tq:(c + 1) * tq, :]
            vw = vw_scr[c0 * tq:(c + 1) * tq, :]
            wbias = wbias_ref[:, (n_win + 1) * tq - ww:]
            kpos_w = (c0 * tq + lax.broadcasted_iota(jnp.int32, (1, ww), 1)).astype(F32)
            for r in range(NSA_GROUP):
                qr = q[:, r * HEAD_DIM:(r + 1) * HEAD_DIM]
                o_sel = _biased_attend(qr, k, v, bias, slope[r] * kpos)
                o_win = _biased_attend(qr, kw, vw, wbias, slope[r] * kpos_w)
                o_ref[:, r * HEAD_DIM:(r + 1) * HEAD_DIM] = (gates[:, 3 * r:3 * r + 1] * o_cmp[r]
                                                              + gates[:, 3 * r + 1:3 * r + 2] * o_sel
                                                              + gates[:, 3 * r + 2:3 * r + 3] * o_win)


def nsa_prompt_constants(n_sub, n_keys, tq):
    c_start = np.arange(n_sub)[None, :] * CMP_STRIDE
    s_start = np.arange(LANES)[:, None] * SEL_BLOCK
    cover_t = ((c_start < s_start + SEL_BLOCK) & (c_start + CMP_LEN > s_start)).astype(np.float32)
    cover_t[:, n_sub - 1:] = 0.0
    expand = np.where(np.arange(n_keys)[None, :] // SEL_BLOCK == np.arange(LANES)[:, None], -MASKED, 0.0)
    dist = WINDOW + np.arange(tq)[:, None] - np.arange(WINDOW + tq)[None, :]
    wbias = np.where((dist >= 0) & (dist < WINDOW), 0.0, MASKED)
    return jnp.asarray(cover_t, BF16), jnp.asarray(expand, BF16), jnp.asarray(wbias, F32)


def nsa_constants(n_sub, n_keys):
    c_start = np.arange(n_sub)[:, None] * CMP_STRIDE
    s_start = np.arange(LANES)[None, :] * SEL_BLOCK
    cover = ((c_start < s_start + SEL_BLOCK) & (c_start + CMP_LEN > s_start)).astype(np.float32)
    cover[n_sub - 1:] = 0.0
    expand = (np.arange(n_keys)[None, :] // SEL_BLOCK == np.arange(LANES)[:, None]).astype(np.float32)
    return jnp.asarray(cover, BF16), jnp.asarray(expand, BF16)


def nsa_prompt(q_all, gates, kcvc, sel_kv, win_kv, batch, seq, tq=256):
    assert WINDOW % tq == 0 and seq % tq == 0 and (seq // SEL_BLOCK) % 8 == 0
    nq = seq // tq
    n_sub = seq // CMP_STRIDE
    cover_t, expand, wbias = nsa_prompt_constants(n_sub, seq, tq)
    gw = NSA_GROUP * HEAD_DIM
    kv_spec = pl.BlockSpec((seq * 4, LANES), lambda b, g, i: (b, 0))
    const = lambda a: pl.BlockSpec(a.shape, lambda b, g, i: (0, 0))
    return pl.pallas_call(
        functools.partial(_nsa_prompt_kernel, tq=tq, seq=seq),
        grid=(batch, NSA_KV_HEADS, nq),
        in_specs=[
            pl.BlockSpec((tq, gw), lambda b, g, i: (b * nq + i, g)),
            pl.BlockSpec((tq, LANES), lambda b, g, i: (b * nq + i, g)),
            pl.BlockSpec((1, 2, 1, n_sub, HEAD_DIM), lambda b, g, i: (b, 0, g, 0, 0)),
            kv_spec, kv_spec, const(cover_t), const(expand), const(wbias),
        ],
        out_specs=pl.BlockSpec((tq, gw), lambda b, g, i: (b * nq + i, g)),
        out_shape=jax.ShapeDtypeStruct((batch * seq, NSA_HEADS * HEAD_DIM), F32),
        scratch_shapes=[pltpu.VMEM((seq, HEAD_DIM), BF16)] * 4,
        compiler_params=pltpu.CompilerParams(dimension_semantics=("parallel", "parallel", "arbitrary"),
                                             vmem_limit_bytes=VMEM_LIMIT),
        name="nsa_prompt",
    )(q_all, gates, kcvc, sel_kv, win_kv, cover_t, expand, wbias)


def _moba_select(gate, n_past, n_blocks):
    blk = lax.broadcasted_iota(jnp.int32, gate.shape, 1)
    past = blk < n_past
    score = jnp.where(past, gate, NEG_INF)
    rank = _top_rank(score, n_blocks)
    return jnp.where(past & (rank < min(MOBA_TOP_K, n_blocks)), 1.0, 0.0)


def _moba_select_t(gate_t, n_past, n_blocks):
    blk = lax.broadcasted_iota(jnp.int32, gate_t.shape, 0)
    past = blk < n_past
    score = jnp.where(past, gate_t, NEG_INF)
    rank = jnp.zeros(score.shape, F32)
    for n2 in range(n_blocks):
        row = score[n2:n2 + 1, :]
        rank = rank + jnp.where(row > score, 1.0, 0.0) + jnp.where((row == score) & (blk > n2), 1.0, 0.0)
    return jnp.where(past & (rank < min(MOBA_TOP_K, n_blocks)), 1.0, 0.0)


def _moba_prompt_kernel(q_ref, kv_ref, o_ref, kmean_scr, k_scr, v_scr, *, seq):
    tq = MOBA_BLOCK
    n_slab = 2 * MOBA_HEADS
    qi = pl.program_id(1)
    nb = seq // tq
    slopes = [s * LOG2E for s in _alibi_slopes(MOBA_HEADS)]

    @pl.when(qi == 0)
    def _():
        kmean_scr[...] = jnp.zeros(kmean_scr.shape, F32)
        for h in range(MOBA_HEADS):
            k = _kv_rows(kv_ref, 0, seq, h, n_slab)
            k_scr[h] = k.astype(BF16)
            v_scr[h] = _kv_rows(kv_ref, 0, seq, MOBA_HEADS + h, n_slab).astype(BF16)
            for n in range(nb):
                kmean_scr[h, n:n + 1, :] = jnp.mean(k[n * tq:(n + 1) * tq], axis=0, keepdims=True)

    row = lax.broadcasted_iota(jnp.int32, (tq, tq), 0)
    col = lax.broadcasted_iota(jnp.int32, (tq, tq), 1)
    own = jnp.where(row >= col, 0.0, MASKED)
    q, sel_bias = [], []
    nb8 = -(-nb // 8) * 8
    for h in range(MOBA_HEADS):
        q.append(q_ref[:, h * HEAD_DIM:(h + 1) * HEAD_DIM])
        gate_t = _dot_nt(kmean_scr[h].astype(BF16), q[h])[:nb8]
        bias_t = (1.0 - _moba_select_t(gate_t, qi, nb - 1)) * MASKED
        sel_bias.append(jnp.concatenate([bias_t, jnp.zeros((LANES - nb8, tq), F32)], axis=0).T)

    for c in range(nb):
        @pl.when(qi == c)
        def _(c=c):
            w = (c + 1) * tq
            kpos = lax.broadcasted_iota(jnp.int32, (1, w), 1).astype(F32)
            for h in range(MOBA_HEADS):
                bias = jnp.concatenate(
                    [jnp.broadcast_to(sel_bias[h][:, n:n + 1], (tq, tq)) for n in range(c)] + [own], axis=1)
                o_ref[:, h * HEAD_DIM:(h + 1) * HEAD_DIM] = _biased_attend(q[h], k_scr[h, :w, :], v_scr[h, :w, :], bias,
                                                                            slopes[h] * kpos)


def moba_prompt(q_all, q_col, moba_kv, batch, seq):
    tq = MOBA_BLOCK
    nq = seq // tq
    width = MOBA_HEADS * HEAD_DIM
    return pl.pallas_call(
        functools.partial(_moba_prompt_kernel, seq=seq),
        grid=(batch, nq),
        in_specs=[
            pl.BlockSpec((tq, width), lambda b, i: (b * nq + i, q_col)),
            pl.BlockSpec((seq * 2 * MOBA_HEADS, LANES), lambda b, i: (b, 0)),
        ],
        out_specs=pl.BlockSpec((tq, width), lambda b, i: (b * nq + i, 0)),
        out_shape=jax.ShapeDtypeStruct((batch * seq, width), F32),
        scratch_shapes=[pltpu.VMEM((MOBA_HEADS, LANES, HEAD_DIM), F32), pltpu.VMEM((MOBA_HEADS, seq, HEAD_DIM), BF16),
                        pltpu.VMEM((MOBA_HEADS, seq, HEAD_DIM), BF16)],
        compiler_params=pltpu.CompilerParams(dimension_semantics=("parallel", "arbitrary"),
                                             vmem_limit_bytes=VMEM_LIMIT),
        name="moba_prompt",
    )(q_all, moba_kv)


def _padded_tile(rows, n):
    return jnp.concatenate([rows, jnp.zeros((n - rows.shape[0], rows.shape[1]), F32)], axis=0).astype(BF16)


def _nsa_sample_kernel(pt_ref, *refs, n_pages, page, t_new, n_win, bb):
    page_refs = refs[:bb * n_pages]
    (q_ref, gate_ref, kcvc_ref, new_sel_ref, cwin_ref, new_win_ref, cover_ref, expand_ref, sbias_ref, wbias_ref,
     o_ref, win_out_ref, ks_scr, vs_scr, kw_scr, vw_scr) = refs[bb * n_pages:]
    n_slab = 2 * NSA_KV_HEADS
    past = n_pages * page
    n_sub = past // CMP_STRIDE
    n_sel = -(-(past + t_new) // SEL_BLOCK)
    rows = NSA_GROUP * t_new
    keep = win_out_ref.shape[0] // (bb * n_slab)
    chains = [(i, g) for i in range(bb) for g in range(NSA_KV_HEADS)]

    for i in range(bb):
        out0, kept = i * keep * n_slab, (keep - t_new) * n_slab
        win_out_ref[out0:out0 + kept, :] = cwin_ref[(i + 1) * n_win * n_slab - kept:(i + 1) * n_win * n_slab, :]
        win_out_ref[out0 + kept:out0 + keep * n_slab, :] = new_win_ref[i * t_new * n_slab:(i + 1) * t_new * n_slab, :]

    for c, (i, g) in enumerate(chains):
        for p in range(n_pages):
            ks_scr[c, p * page:(p + 1) * page, :] = _kv_rows(page_refs[i * n_pages + p], 0, page, g, n_slab).astype(BF16)
            vs_scr[c, p * page:(p + 1) * page, :] = (
                _kv_rows(page_refs[i * n_pages + p], 0, page, 2 + g, n_slab).astype(BF16))
        ks_scr[c, past:, :] = _padded_tile(_kv_rows(new_sel_ref, i * t_new, t_new, g, n_slab), LANES)
        vs_scr[c, past:, :] = _padded_tile(_kv_rows(new_sel_ref, i * t_new, t_new, 2 + g, n_slab), LANES)
        kw_scr[c, :n_win, :] = _kv_rows(cwin_ref, i * n_win, n_win, g, n_slab).astype(BF16)
        vw_scr[c, :n_win, :] = _kv_rows(cwin_ref, i * n_win, n_win, 2 + g, n_slab).astype(BF16)
        kw_scr[c, n_win:, :] = _padded_tile(_kv_rows(new_win_ref, i * t_new, t_new, g, n_slab), LANES)
        vw_scr[c, n_win:, :] = _padded_tile(_kv_rows(new_win_ref, i * t_new, t_new, 2 + g, n_slab), LANES)

    q_all = [_stack_heads(q_ref, i * t_new, t_new, NSA_HEADS) for i in range(bb)]
    q = [q_all[i][g * rows:(g + 1) * rows] for i, g in chains]
    tpos = past + lax.broadcasted_iota(jnp.int32, (rows, 1), 0) % t_new
    tpos_q = past + lax.broadcasted_iota(jnp.int32, (t_new, 1), 0)
    n_idx = lax.broadcasted_iota(jnp.int32, (1, n_sub), 1)
    valid = (n_idx * CMP_STRIDE + (CMP_LEN - 1) <= tpos) & (n_idx < n_sub - 1)
    cmp_out = [_cmp_branch_rows(q[c], kcvc_ref[i, 0, g].astype(BF16), kcvc_ref[i, 1, g].astype(BF16),
                                cover_ref[...], valid) for c, (i, g) in enumerate(chains)]
    selected = []
    for c in range(len(chains)):
        imp = cmp_out[c][1]
        imp_t = imp[0:t_new]
        for r in range(1, NSA_GROUP):
            imp_t = imp_t + imp[r * t_new:(r + 1) * t_new]
        selected.append(jnp.concatenate([_select_blocks(imp_t, tpos_q, n_sel)] * NSA_GROUP, axis=0).astype(BF16))
    o_sel, o_win = [], []
    for c, (i, g) in enumerate(chains):
        bias = (_dot(selected[c], expand_ref[...]) + MASKED) + sbias_ref[g]
        o_sel.append(_softmax_attend(_dot_nt(q[c], ks_scr[c]) * (SCALE * LOG2E) + bias, vs_scr[c]))
        o_win.append(_softmax_attend(_dot_nt(q[c], kw_scr[c]) * (SCALE * LOG2E) + wbias_ref[g], vw_scr[c]))

    gates = _sigmoid(gate_ref[...])
    for c, (i, g) in enumerate(chains):
        out_rows = slice(i * t_new, (i + 1) * t_new)
        for r in range(NSA_GROUP):
            h = g * NSA_GROUP + r
            lane = g * LANES + 3 * r
            rs = slice(r * t_new, (r + 1) * t_new)
            o_ref[out_rows, h * HEAD_DIM:(h + 1) * HEAD_DIM] = (
                gates[out_rows, lane:lane + 1] * cmp_out[c][0][rs]
                + gates[out_rows, lane + 1:lane + 2] * o_sel[c][rs]
                + gates[out_rows, lane + 2:lane + 3] * o_win[c][rs])


def _cmp_branch_rows(q, kc, vc, cover, valid):
    s = jnp.where(valid, _dot_nt(q, kc) * SCALE, NEG_INF)
    e = jnp.where(valid, jnp.exp(s - jnp.max(s, axis=-1, keepdims=True)), 0.0)
    d = jnp.sum(e, axis=-1, keepdims=True)
    p = e / jnp.where(d > 0, d, 1.0)
    p_hi = p.astype(BF16)
    p_lo = (p - p_hi.astype(F32)).astype(BF16)
    return _dot(p_hi, vc), _dot(p_hi, cover) + _dot(p_lo, cover)


def nsa_sample(q_all, gates, kcvc, sel_pool, page_table, page, new_sel, cache_win, new_win, t_new):
    nb, n_pages = page_table.shape
    past = n_pages * page
    n_sub = past // CMP_STRIDE
    n_slab = 2 * NSA_KV_HEADS
    n_win = cache_win.shape[0] // (n_slab * nb)
    assert past % SEL_BLOCK == 0 and t_new <= SEL_BLOCK and past // SEL_BLOCK + 2 <= LANES
    cover, expand = nsa_constants(n_sub, past + LANES)
    expand = expand * jnp.asarray(-MASKED, BF16)
    slopes = _alibi_slopes(NSA_HEADS)
    t_pos = past + np.arange(t_new)

    def alibi_bias(pos, ok):
        per_head = [np.where(ok, s * LOG2E * pos[None, :], MASKED) for s in slopes]
        return jnp.asarray(np.stack(per_head).reshape(NSA_KV_HEADS, NSA_GROUP * t_new, pos.shape[0]), F32)

    sel_pos = np.arange(past + LANES)
    sbias = alibi_bias(sel_pos, (sel_pos[None, :] <= t_pos[:, None]) & (sel_pos[None, :] < past + t_new))
    win_pos = past - n_win + np.arange(n_win + LANES)
    dist = t_pos[:, None] - win_pos[None, :]
    wbias = alibi_bias(win_pos, (dist >= 0) & (dist < WINDOW) & (win_pos[None, :] >= 0)
                       & (win_pos[None, :] < past + t_new))
    keep = min(WINDOW, n_win + t_new)
    assert keep >= t_new and (n_slab * (keep - t_new)) % 8 == 0
    bb = 2 if nb % 2 == 0 else 1
    n_chain = bb * NSA_KV_HEADS
    width = NSA_HEADS * HEAD_DIM
    const = lambda a: pl.BlockSpec(a.shape, lambda b, pt: (0,) * a.ndim)
    rows = lambda n: pl.BlockSpec((bb * n_slab * n, LANES), lambda b, pt: (b, 0))
    page_specs = [pl.BlockSpec((n_slab * page, LANES),
                               functools.partial(lambda b, pt, i, p: (pt[(b * bb + i) * n_pages + p], 0), i=i, p=p))
                  for i in range(bb) for p in range(n_pages)]
    grid_spec = pltpu.PrefetchScalarGridSpec(
        num_scalar_prefetch=1,
        grid=(nb // bb,),
        in_specs=page_specs + [
            pl.BlockSpec((bb * t_new, width), lambda b, pt: (b, 0)),
            pl.BlockSpec((bb * t_new, NSA_KV_HEADS * LANES), lambda b, pt: (b, 0)),
            pl.BlockSpec((bb, 2, 2, n_sub, HEAD_DIM), lambda b, pt: (b, 0, 0, 0, 0)),
            rows(t_new), rows(n_win), rows(t_new),
            const(cover), const(expand), const(sbias), const(wbias)],
        out_specs=[pl.BlockSpec((bb * t_new, width), lambda b, pt: (b, 0)), rows(keep)],
        scratch_shapes=[pltpu.VMEM((n_chain, past + LANES, HEAD_DIM), BF16),
                        pltpu.VMEM((n_chain, past + LANES, HEAD_DIM), BF16),
                        pltpu.VMEM((n_chain, n_win + LANES, HEAD_DIM), BF16),
                        pltpu.VMEM((n_chain, n_win + LANES, HEAD_DIM), BF16)],
    )
    return pl.pallas_call(
        functools.partial(_nsa_sample_kernel, n_pages=n_pages, page=page, t_new=t_new, n_win=n_win, bb=bb),
        grid_spec=grid_spec,
        out_shape=[jax.ShapeDtypeStruct((nb * t_new, width), F32),
                   jax.ShapeDtypeStruct((nb * keep * n_slab, LANES), F32)],
        compiler_params=_params(1, VMEM_LIMIT),
        name="nsa_sample",
    )(page_table.reshape(-1), *([sel_pool] * (bb * n_pages)), q_all, gates, kcvc, new_sel, cache_win, new_win, cover,
      expand, sbias, wbias)


def _slab_attend(q, kv, bias, shift):
    s = _dot_nt(q, kv) * (SCALE * LOG2E) + bias
    e = jnp.exp2(s - jnp.max(s, axis=-1, keepdims=True))
    d = jnp.sum(e, axis=-1, keepdims=True)
    return _dot(pltpu.roll(e, shift, axis=1).astype(BF16), kv) / jnp.where(d > 0, d, 1.0)


def _stack_heads(ref, row0, n_rows, n_heads):
    return jnp.concatenate([ref[row0:row0 + n_rows, h * HEAD_DIM:(h + 1) * HEAD_DIM] for h in range(n_heads)],
                           axis=0).astype(BF16)


def _slab_bias(head_slab, slopes, key_pos, key_ok, n_slab, t_new):
    n_heads, n_keys = len(head_slab), key_pos.shape[0]
    slab = np.arange(n_keys * n_slab) % n_slab
    out = np.full((n_heads, t_new, n_keys * n_slab), MASKED, np.float64)
    for h in range(n_heads):
        ok = np.repeat(key_ok, n_slab, axis=1) & (slab == head_slab[h])[None, :]
        out[h] = np.where(ok, slopes[h] * LOG2E * np.repeat(key_pos, n_slab)[None, :], MASKED)
    return jnp.asarray(out.reshape(n_heads * t_new, n_keys * n_slab), F32)


def _moba_sample_kernel(pt_ref, *refs, n_pages, page, t_new):
    page_refs = refs[:n_pages]
    q_ref, new_ref, bias_ref, o_ref, kv_scr, mean_scr = refs[n_pages:]
    n_slab = 2 * MOBA_HEADS
    rows_pp = page * n_slab
    past = n_pages * page
    pages_per_block = MOBA_BLOCK // page
    n_past = n_pages // pages_per_block
    block_w = MOBA_BLOCK * n_slab

    mean_scr[...] = jnp.zeros(mean_scr.shape, F32)
    for n in range(n_past):
        total = None
        for p in range(n * pages_per_block, (n + 1) * pages_per_block):
            x = page_refs[p][...]
            kv_scr[p * rows_pp:(p + 1) * rows_pp, :] = x.astype(BF16)
            part = jnp.sum(x.reshape(page, n_slab, LANES), axis=0)
            total = part if total is None else total + part
        mean_scr[n * n_slab:(n + 1) * n_slab, :] = total * (1.0 / MOBA_BLOCK)
    x_new = new_ref[...]
    kv_scr[past * n_slab:, :] = _padded_tile(x_new, LANES)
    mean_scr[n_past * n_slab:(n_past + 1) * n_slab, :] = (
        jnp.sum(x_new.reshape(t_new, n_slab, LANES), axis=0) * (1.0 / MOBA_BLOCK))

    tpos = past + lax.broadcasted_iota(jnp.int32, (t_new, 1), 0)
    cur = tpos // MOBA_BLOCK
    blk = lax.broadcasted_iota(jnp.int32, (t_new, LANES), 1)
    sel_bias = []
    for h in range(MOBA_HEADS):
        q = q_ref[:, h * HEAD_DIM:(h + 1) * HEAD_DIM].astype(BF16)
        kmean = mean_scr[pl.ds(h, LANES // n_slab, stride=n_slab), :]
        kmean = jnp.concatenate([kmean, jnp.zeros((LANES - kmean.shape[0], LANES), F32)], axis=0)
        gate = _dot_nt(q, kmean.astype(BF16))
        selected = jnp.where(blk == cur, 1.0, _moba_select(gate, cur, n_past))
        sel_bias.append((1.0 - selected) * MASKED)
    sel_bias = jnp.concatenate(sel_bias, axis=0)

    bias = jnp.concatenate(
        [bias_ref[:, n * block_w:(n + 1) * block_w] + sel_bias[:, n:n + 1] for n in range(n_past)]
        + [bias_ref[:, n_past * block_w:] + sel_bias[:, n_past:n_past + 1]], axis=1)
    o = _slab_attend(_stack_heads(q_ref, 0, t_new, MOBA_HEADS), kv_scr[...], bias, MOBA_HEADS)
    for h in range(MOBA_HEADS):
        o_ref[:, h * HEAD_DIM:(h + 1) * HEAD_DIM] = o[h * t_new:(h + 1) * t_new]


def moba_sample(q_all, q_col, moba_pool, page_table, page, new_kv, t_new):
    nb, n_pages = page_table.shape
    past = n_pages * page
    width = MOBA_HEADS * HEAD_DIM
    n_slab = 2 * MOBA_HEADS
    assert past % MOBA_BLOCK == 0 and t_new * n_slab <= LANES and past // MOBA_BLOCK < LANES // n_slab
    new_keys = LANES // n_slab
    key_pos = np.arange(past + new_keys)
    t_pos = past + np.arange(t_new)
    key_ok = (key_pos[None, :] <= t_pos[:, None]) & (key_pos[None, :] < past + t_new)
    bias = _slab_bias(list(range(MOBA_HEADS)), _alibi_slopes(MOBA_HEADS), key_pos, key_ok, n_slab, t_new)
    grid_spec = pltpu.PrefetchScalarGridSpec(
        num_scalar_prefetch=1,
        grid=(nb,),
        in_specs=_page_specs(n_pages, n_slab * page) + [
            pl.BlockSpec((t_new, width), lambda b, pt: (b, q_col)),
            pl.BlockSpec((t_new * n_slab, LANES), lambda b, pt: (b, 0)),
            pl.BlockSpec(bias.shape, lambda b, pt: (0, 0))],
        out_specs=pl.BlockSpec((t_new, width), lambda b, pt: (b, 0)),
        scratch_shapes=[pltpu.VMEM((past * n_slab + LANES, HEAD_DIM), BF16), pltpu.VMEM((LANES, LANES), F32)],
    )
    return pl.pallas_call(
        functools.partial(_moba_sample_kernel, n_pages=n_pages, page=page, t_new=t_new),
        grid_spec=grid_spec,
        out_shape=jax.ShapeDtypeStruct((nb * t_new, width), F32),
        compiler_params=_params(1, VMEM_LIMIT),
        name="moba_sample",
    )(page_table.reshape(-1), *([moba_pool] * n_pages), q_all, new_kv, bias)


def _cross_kernel(q_ref, kv_ref, o_ref, *, n_mem):
    n_slab = 2 * MEM_HEADS
    for h in range(MEM_HEADS):
        k = _kv_rows(kv_ref, 0, n_mem, h, n_slab).astype(BF16)
        v = _kv_rows(kv_ref, 0, n_mem, MEM_HEADS + h, n_slab).astype(BF16)
        s = _dot_nt(q_ref[:, h * HEAD_DIM:(h + 1) * HEAD_DIM], k) * SCALE
        e = jnp.exp(s - jnp.max(s, axis=-1, keepdims=True))
        p = e / jnp.sum(e, axis=-1, keepdims=True)
        o_ref[:, h * HEAD_DIM:(h + 1) * HEAD_DIM] = _dot(p.astype(BF16), v)


def cross_attend(q_all, q_col, mem_kv, batch, rows, tq):
    nq = rows // tq
    n_slab = 2 * MEM_HEADS
    width = MEM_HEADS * HEAD_DIM
    n_mem = mem_kv.shape[0] // (batch * n_slab)
    return pl.pallas_call(
        functools.partial(_cross_kernel, n_mem=n_mem),
        grid=(batch, nq),
        in_specs=[
            pl.BlockSpec((tq, width), lambda b, i: (b * nq + i, q_col)),
            pl.BlockSpec((n_mem * n_slab, LANES), lambda b, i: (b, 0)),
        ],
        out_specs=pl.BlockSpec((tq, width), lambda b, i: (b * nq + i, 0)),
        out_shape=jax.ShapeDtypeStruct((batch * rows, width), F32),
        compiler_params=_params(2, VMEM_LIMIT),
        name="cross_attend",
    )(q_all, mem_kv)


def _cross_sample_kernel(q_ref, kv_ref, bias_ref, o_ref, *, n_mem, t_new, bb):
    n_slab = 2 * MEM_HEADS
    rows = n_mem * n_slab
    for i in range(bb):
        q = _stack_heads(q_ref, i * t_new, t_new, MEM_HEADS)
        o = _slab_attend(q, kv_ref[i * rows:(i + 1) * rows, :].astype(BF16), bias_ref[...], MEM_HEADS)
        for h in range(MEM_HEADS):
            o_ref[i * t_new:(i + 1) * t_new, h * HEAD_DIM:(h + 1) * HEAD_DIM] = o[h * t_new:(h + 1) * t_new]


def cross_attend_sample(q_all, q_col, mem_kv, batch, t_new, bb):
    assert batch % bb == 0
    n_slab = 2 * MEM_HEADS
    width = MEM_HEADS * HEAD_DIM
    n_mem = mem_kv.shape[0] // (batch * n_slab)
    bias = _slab_bias(list(range(MEM_HEADS)), [0.0] * MEM_HEADS, np.arange(n_mem), np.ones((t_new, n_mem), bool),
                      n_slab, t_new)
    return pl.pallas_call(
        functools.partial(_cross_sample_kernel, n_mem=n_mem, t_new=t_new, bb=bb),
        grid=(batch // bb,),
        in_specs=[
            pl.BlockSpec((bb * t_new, width), lambda b: (b, q_col)),
            pl.BlockSpec((bb * n_mem * n_slab, LANES), lambda b: (b, 0)),
            pl.BlockSpec(bias.shape, lambda b: (0, 0)),
        ],
        out_specs=pl.BlockSpec((bb * t_new, width), lambda b: (b, 0)),
        out_shape=jax.ShapeDtypeStruct((batch * t_new, width), F32),
        compiler_params=_params(1, VMEM_LIMIT),
        name="cross_sample",
    )(q_all, mem_kv, bias)


def _merge_kernel(x_ref, on_ref, om_ref, oc_ref, z_ref, gl_ref, wn_ref, wm_ref, wc_ref, wo_ref, fg_ref, y_ref, mix_scr,
                  *, tn):
    d = x_ref.shape[1]
    wn_k, wm_k = wn_ref.shape[0], wm_ref.shape[0]
    z = z_ref[...]
    act = z * _sigmoid(z)
    un = (on_ref[...] * act[:, :wn_k]).astype(BF16)
    um = (om_ref[...] * act[:, wn_k:wn_k + wm_k]).astype(BF16)
    uc = (oc_ref[...] * act[:, wn_k + wm_k:]).astype(BF16)
    for c in range(d // tn):
        cols = slice(c * tn, (c + 1) * tn)
        mix = (_sigmoid(gl_ref[:, c * tn:(c + 1) * tn]) * _dot(un, wn_ref[:, cols])
               + _sigmoid(gl_ref[:, d + c * tn:d + (c + 1) * tn]) * _dot(um, wm_ref[:, cols])
               + _sigmoid(gl_ref[:, 2 * d + c * tn:2 * d + (c + 1) * tn]) * _dot(uc, wc_ref[:, cols]))
        mix_scr[:, cols] = mix.astype(BF16)
    y = x_ref[...] + _dot(mix_scr[...], wo_ref[...])
    inv = lax.rsqrt(jnp.mean(y * y, axis=-1, keepdims=True) + RMS_EPS)
    y_ref[...] = y * inv * fg_ref[...]


def merge_branches(x, o_nsa, o_moba, o_mem, z_all, gate_logits, w_nsa, w_moba, w_mem, w_out, final_g, tm=256, tn=512):
    m, d = x.shape
    tm = min(tm, m)
    row = lambda a: pl.BlockSpec((tm, a.shape[1]), lambda i: (i, 0))
    whole = lambda a: pl.BlockSpec(a.shape, lambda i: (0, 0), pipeline_mode=pl.Buffered(1))
    final_g = final_g.reshape(1, d)
    return pl.pallas_call(
        functools.partial(_merge_kernel, tn=tn),
        grid=(m // tm,),
        in_specs=[row(x), row(o_nsa), row(o_moba), row(o_mem), row(z_all), row(gate_logits),
                  whole(w_nsa), whole(w_moba), whole(w_mem), whole(w_out), whole(final_g)],
        out_specs=row(x),
        out_shape=jax.ShapeDtypeStruct((m, d), F32),
        scratch_shapes=[pltpu.VMEM((tm, d), BF16)],
        compiler_params=_params(1, VMEM_LIMIT),
        name="merge",
    )(x, o_nsa, o_moba, o_mem, z_all, gate_logits, w_nsa, w_moba, w_mem, w_out, final_g)


def _in_weights(w_in, d_model):
    splits = (NSA_HEADS * HEAD_DIM, 4 * HEAD_DIM, 4 * HEAD_DIM, 4 * HEAD_DIM, 3 * NSA_HEADS, NSA_HEADS * HEAD_DIM,
              MOBA_HEADS * HEAD_DIM, 2 * MOBA_HEADS * HEAD_DIM, MOBA_HEADS * HEAD_DIM, MEM_HEADS * HEAD_DIM,
              MEM_HEADS * HEAD_DIM, N_BRANCH * d_model)
    offs = np.cumsum((0,) + splits)
    assert offs[-1] == w_in.shape[1]
    (nsa_q, cmp_kv, sel_kv, win_kv, nsa_g, nsa_z, moba_q, moba_kv, moba_z, mem_q, mem_z, merge_g) = (
        w_in[:, offs[i]:offs[i + 1]] for i in range(len(splits)))
    per_group = 3 * NSA_GROUP
    gate_w = jnp.pad(nsa_g.reshape(-1, NSA_KV_HEADS, per_group), ((0, 0), (0, 0), (0, LANES - per_group)))
    cast = lambda w: w.astype(BF16)
    return dict(q=cast(jnp.concatenate([nsa_q, moba_q, mem_q], axis=1)), cmp=cast(cmp_kv), sel=cast(sel_kv),
                win=cast(win_kv), moba=cast(moba_kv), gate=cast(gate_w.reshape(-1, NSA_KV_HEADS * LANES)),
                z=cast(jnp.concatenate([nsa_z, moba_z, mem_z], axis=1)), merge=cast(merge_g))


def _in_projection(x, norm_g, w, q_dtype):
    tm = min(1024, x.shape[0] // 2)
    xn = rmsnorm_bf16(x, norm_g, 512)
    mm = lambda name, dtype: matmul(xn, w[name], dtype, 2 * tm, 1024, "in_" + name)
    rows = lambda name: matmul_rows(xn, w[name], tm, "in_" + name)
    return dict(q=mm("q", q_dtype), cmp=rows("cmp"), sel=rows("sel"), win=rows("win"), moba=rows("moba"),
                gate=mm("gate", F32), z=mm("z", F32), merge=mm("merge", F32))


def kernel(x_prompt, x_sample, cache_cmp_kv, cache_sel_kv, cache_moba_kv, cache_win_kv, cache_mem_kv, page_table,
           mem_prompt, norm_g, w_in, phi_pos, phi_w1, phi_w2, mem_norm_g, w_mem_kv, w_br_nsa, w_br_moba, w_br_mem,
           w_out, final_norm_g):
    batch, seq, d_model = x_prompt.shape
    nb, t_new, _ = x_sample.shape
    n_mem = mem_prompt.shape[1]
    page = cache_cmp_kv.shape[2]
    n_win = cache_win_kv.shape[2]
    assert norm_g.shape[0] == 1, "one layer"
    assert seq % MOBA_BLOCK == 0 and (page_table.shape[1] * page) % MOBA_BLOCK == 0 and MOBA_BLOCK % page == 0
    l = 0
    q_moba_col = NSA_HEADS * HEAD_DIM // (MOBA_HEADS * HEAD_DIM)
    q_mem_col = (NSA_HEADS + MOBA_HEADS) * HEAD_DIM // (MEM_HEADS * HEAD_DIM)
    lane_rows = lambda a: a.reshape(-1, LANES)

    w = _in_weights(w_in[l], d_model)
    w1cat, pos8, w2 = compress_weights(phi_pos[l], phi_w1[l], phi_w2[l])
    br = [a[l].astype(BF16) for a in (w_br_nsa, w_br_moba, w_br_mem, w_out)]

    xp = x_prompt.reshape(batch * seq, d_model)
    hp = _in_projection(xp, norm_g[l], w, BF16)
    mem_n = rmsnorm_bf16(mem_prompt.reshape(batch * n_mem, d_model), mem_norm_g[l], 512)
    mem_kv_p = matmul_rows(mem_n, w_mem_kv[l].astype(BF16), 1024, "mem_kv")
    pages_p = seq // page
    identity_table = jnp.arange(batch * pages_p, dtype=jnp.int32).reshape(batch, pages_p)
    kcvc_p = compress_tokens(hp["cmp"], identity_table, page, w1cat, pos8, w2)
    o_nsa_p = nsa_prompt(hp["q"], hp["gate"], kcvc_p, hp["sel"], hp["win"], batch, seq)
    o_moba_p = moba_prompt(hp["q"], q_moba_col, hp["moba"], batch, seq)
    o_mem_p = cross_attend(hp["q"], q_mem_col, mem_kv_p, batch, seq, min(seq, 1024))
    y_prompt = merge_branches(xp, o_nsa_p, o_moba_p, o_mem_p, hp["z"], hp["merge"], *br, final_norm_g)

    xs = x_sample.reshape(nb * t_new, d_model)
    hs = _in_projection(xs, norm_g[l], w, F32)
    kcvc_s = compress_tokens(lane_rows(cache_cmp_kv[l]), page_table, page, w1cat, pos8, w2)
    o_nsa_s, win_s = nsa_sample(hs["q"], hs["gate"], kcvc_s, lane_rows(cache_sel_kv[l]), page_table, page, hs["sel"],
                                lane_rows(cache_win_kv[l]), hs["win"], t_new)
    o_moba_s = moba_sample(hs["q"], q_moba_col, lane_rows(cache_moba_kv[l]), page_table, page, hs["moba"], t_new)
    o_mem_s = cross_attend_sample(hs["q"], q_mem_col, lane_rows(cache_mem_kv[l]), nb, t_new, 4 if nb % 4 == 0 else 1)
    y_sample = merge_branches(xs, o_nsa_s, o_moba_s, o_mem_s, hs["z"], hs["merge"], *br, final_norm_g)

    nsa_shape = lambda a, b_, t_: a.reshape(1, b_, t_, 2, NSA_KV_HEADS, HEAD_DIM)
    moba_shape = lambda a, b_, t_: a.reshape(1, b_, t_, 2, MOBA_HEADS, HEAD_DIM)
    keep_p = min(WINDOW, seq)
    keep_s = min(WINDOW, n_win + t_new)
    return (y_prompt.reshape(batch, seq, d_model), y_sample.reshape(nb, t_new, d_model),
            nsa_shape(hp["cmp"], batch, seq), nsa_shape(hp["sel"], batch, seq), moba_shape(hp["moba"], batch, seq),
            nsa_shape(hp["win"], batch, seq)[:, :, seq - keep_p:],
            mem_kv_p.reshape(1, batch, n_mem, 2, MEM_HEADS, HEAD_DIM),
            nsa_shape(hs["cmp"], nb, t_new), nsa_shape(hs["sel"], nb, t_new), moba_shape(hs["moba"], nb, t_new),
            nsa_shape(win_s, nb, keep_s))
```

```python
import functools

import numpy as np
import jax
import jax.numpy as jnp
from jax import lax
from jax.experimental import pallas as pl
from jax.experimental.pallas import tpu as pltpu

HEAD_DIM = 128
NSA_HEADS = 8
NSA_KV_HEADS = 2
NSA_GROUP = NSA_HEADS // NSA_KV_HEADS
CMP_LEN = 32
CMP_STRIDE = 16
SEL_BLOCK = 64
SEL_TOP_N = 16
WINDOW = 512
FORCE_BONUS = 1.0e4
MOBA_HEADS = 4
MOBA_BLOCK = 256
MOBA_TOP_K = 3
MEM_HEADS = 4
N_BRANCH = 3
RMS_EPS = 1e-6
NEG_INF = -1e30
SCALE = HEAD_DIM ** -0.5
LOG2E = 1.4426950408889634
MASKED = -(2.0 ** 100)

LANES = 128
STAGE_PITCH = 24
VMEM_LIMIT = 56 * 1024 * 1024

F32 = jnp.float32
BF16 = jnp.bfloat16


def _params(n_axes, vmem=None):
    return pltpu.CompilerParams(dimension_semantics=("parallel",) * n_axes, vmem_limit_bytes=vmem)


def _dot(a, b):
    return jnp.dot(a, b, preferred_element_type=F32)


def _dot_nt(a, b):
    return lax.dot_general(a, b, (((1,), (1,)), ((), ())), preferred_element_type=F32)


def _sigmoid(x):
    return 1.0 / (1.0 + jnp.exp(-x))


def _rms_kernel(x_ref, g_ref, o_ref):
    x = x_ref[...]
    inv = lax.rsqrt(jnp.mean(x * x, axis=-1, keepdims=True) + RMS_EPS)
    o_ref[...] = (x * inv * g_ref[...]).astype(o_ref.dtype)


def rmsnorm_bf16(x, g, tm):
    n, d = x.shape
    return pl.pallas_call(
        _rms_kernel,
        grid=(n // tm,),
        in_specs=[pl.BlockSpec((tm, d), lambda i: (i, 0)), pl.BlockSpec((1, d), lambda i: (0, 0))],
        out_specs=pl.BlockSpec((tm, d), lambda i: (i, 0)),
        out_shape=jax.ShapeDtypeStruct((n, d), BF16),
        compiler_params=_params(1, VMEM_LIMIT),
        name="rmsnorm",
    )(x, g.reshape(1, d))


def _mm_kernel(x_ref, w_ref, o_ref):
    o_ref[...] = _dot(x_ref[...], w_ref[...]).astype(o_ref.dtype)


def matmul(x, w, out_dtype, tm, tn, name):
    m, k = x.shape
    n = w.shape[1]
    tm, tn = min(tm, m), min(tn, n)
    return pl.pallas_call(
        _mm_kernel,
        grid=(m // tm, n // tn),
        in_specs=[pl.BlockSpec((tm, k), lambda i, j: (i, 0)), pl.BlockSpec((k, tn), lambda i, j: (0, j))],
        out_specs=pl.BlockSpec((tm, tn), lambda i, j: (i, j)),
        out_shape=jax.ShapeDtypeStruct((m, n), out_dtype),
        compiler_params=_params(2, VMEM_LIMIT),
        name=name,
    )(x, w)


def _mm_rows_kernel(x_ref, w_ref, o_ref):
    tm = x_ref.shape[0]
    n_slab = w_ref.shape[1] // LANES
    res = _dot(x_ref[...], w_ref[...])
    for j in range(n_slab):
        o_ref[pl.ds(j, tm, stride=n_slab), :] = res[:, j * LANES:(j + 1) * LANES]


def matmul_rows(x, w, tm, name):
    m, k = x.shape
    n = w.shape[1]
    n_slab = n // LANES
    tm = min(tm, m)
    return pl.pallas_call(
        _mm_rows_kernel,
        grid=(m // tm,),
        in_specs=[pl.BlockSpec((tm, k), lambda i: (i, 0)), pl.BlockSpec((k, n), lambda i: (0, 0))],
        out_specs=pl.BlockSpec((tm * n_slab, LANES), lambda i: (i, 0)),
        out_shape=jax.ShapeDtypeStruct((m * n_slab, LANES), F32),
        compiler_params=_params(1, VMEM_LIMIT),
        name=name,
    )(x, w)


def _compress_kernel(pt_ref, *refs, n_pages, page, bb):
    n_sub_one = n_pages * page // CMP_STRIDE
    n_pages = bb * n_pages
    page_refs = refs[:n_pages]
    w1_ref, pos_ref, w2_ref, out_ref, lhs_scr, stage_scr = refs[n_pages:]
    sub_per_page = page // CMP_STRIDE
    n_sub = n_pages * sub_per_page
    for p in range(n_pages):
        for cg in range(4):
            stage = stage_scr.at[(p % 2) * 4 + cg]
            for j in range(sub_per_page):
                stage[j * STAGE_PITCH:j * STAGE_PITCH + CMP_STRIDE, :] = (
                    page_refs[p][pl.ds(4 * CMP_STRIDE * j + cg, CMP_STRIDE, stride=4), :])
            for s in range(CMP_STRIDE):
                lhs_scr[cg, p * sub_per_page:(p + 1) * sub_per_page, s * HEAD_DIM:(s + 1) * HEAD_DIM] = (
                    stage[pl.ds(s, sub_per_page, stride=STAGE_PITCH), :])
    row = lax.broadcasted_iota(jnp.int32, (n_sub, HEAD_DIM), 0)
    for c in range(2):
        w1 = w1_ref[c]
        bias = _dot(pos_ref[c], w1)
        b0 = bias[0:1, :HEAD_DIM]
        b1 = bias[1:2, HEAD_DIM:]
        for g in range(2):
            a = _dot(lhs_scr[c * 2 + g].astype(BF16), w1)
            a0 = a[:, :HEAD_DIM] + b0
            a1 = a[:, HEAD_DIM:] + b1
            hid = a0 + pltpu.roll(a1, shift=n_sub - 1, axis=0)
            act = hid * _sigmoid(hid)
            o = _dot(act.astype(BF16), w2_ref[c])
            o = jnp.where(row % n_sub_one < n_sub_one - 1, o, 0.0)
            for i in range(bb):
                out_ref[i, c, g] = o[i * n_sub_one:(i + 1) * n_sub_one]


def _page_specs(n_pages, rows, bb=1):
    return [pl.BlockSpec((rows, LANES),
                         functools.partial(lambda b, pt, i, p: (pt[(b * bb + i) * n_pages + p], 0), i=i, p=p))
            for i in range(bb) for p in range(n_pages)]


def compress_tokens(pool, page_table, page, w1cat, pos8, w2):
    nb, n_pages = page_table.shape
    n_sub = n_pages * page // CMP_STRIDE
    bb = 2 if nb % 2 == 0 else 1
    const = lambda shape: pl.BlockSpec(shape, lambda b, pt: (0,) * len(shape))
    grid_spec = pltpu.PrefetchScalarGridSpec(
        num_scalar_prefetch=1,
        grid=(nb // bb,),
        in_specs=_page_specs(n_pages, 4 * page, bb) + [const(w1cat.shape), const(pos8.shape), const(w2.shape)],
        out_specs=pl.BlockSpec((bb, 2, 2, n_sub, HEAD_DIM), lambda b, pt: (b, 0, 0, 0, 0)),
        scratch_shapes=[pltpu.VMEM((4, bb * n_sub, CMP_STRIDE * HEAD_DIM), F32),
                        pltpu.VMEM((8, page // CMP_STRIDE * STAGE_PITCH, LANES), F32)],
    )
    return pl.pallas_call(
        functools.partial(_compress_kernel, n_pages=n_pages, page=page, bb=bb),
        grid_spec=grid_spec,
        out_shape=jax.ShapeDtypeStruct((nb, 2, 2, n_sub, HEAD_DIM), F32),
        compiler_params=_params(1, VMEM_LIMIT),
        name="compress",
    )(page_table.reshape(-1), *([pool] * (bb * n_pages)), w1cat, pos8, w2)


def compress_weights(phi_pos, phi_w1, phi_w2):
    n_half = CMP_LEN // CMP_STRIDE
    w1 = phi_w1.reshape(2, n_half, CMP_STRIDE, HEAD_DIM, HEAD_DIM).transpose(0, 2, 3, 1, 4)
    w1cat = w1.reshape(2, CMP_STRIDE * HEAD_DIM, n_half * HEAD_DIM).astype(BF16)
    pos = phi_pos.reshape(2, n_half, CMP_STRIDE * HEAD_DIM)
    pos8 = jnp.pad(pos, ((0, 0), (0, 8 - n_half), (0, 0))).astype(BF16)
    return w1cat, pos8, phi_w2.astype(BF16)


def _softmax_attend(s, v):
    e = jnp.exp2(s - jnp.max(s, axis=-1, keepdims=True))
    d = jnp.sum(e, axis=-1, keepdims=True)
    return _dot(e.astype(BF16), v) / jnp.where(d > 0, d, 1.0)


def _biased_attend(q, k, v, bias, key_slope):
    return _softmax_attend(_dot_nt(q, k) * (SCALE * LOG2E) + key_slope + bias, v)


def _top_rank(score, n_blocks):
    blk = lax.broadcasted_iota(jnp.int32, score.shape, 1)
    rank = jnp.zeros(score.shape, F32)
    for n2 in range(n_blocks):
        col = score[:, n2:n2 + 1]
        rank = rank + jnp.where(col > score, 1.0, 0.0) + jnp.where((col == score) & (blk > n2), 1.0, 0.0)
    return rank


def _alibi_slopes(n_heads):
    return [float(2.0 ** (-8.0 * (h + 1) / n_heads)) for h in range(n_heads)]


def _masked_softmax(s, valid, axis):
    s = jnp.where(valid, s, NEG_INF)
    e = jnp.where(valid, jnp.exp(s - jnp.max(s, axis=axis, keepdims=True)), 0.0)
    d = jnp.sum(e, axis=axis, keepdims=True)
    return e / jnp.where(d > 0, d, 1.0)


def _cmp_branch(q, kc, vc, cover, valid):
    imp = None
    outs = []
    for r in range(NSA_GROUP):
        p = _masked_softmax(_dot_nt(q[:, r * HEAD_DIM:(r + 1) * HEAD_DIM], kc) * SCALE, valid, -1)
        p_hi = p.astype(BF16)
        p_lo = (p - p_hi.astype(F32)).astype(BF16)
        outs.append(_dot(p_hi, vc))
        part = _dot(p_hi, cover) + _dot(p_lo, cover)
        imp = part if imp is None else imp + part
    return outs, imp.T


def _select_blocks_t(imp_t, tpos_row, n_sel):
    blk = lax.broadcasted_iota(jnp.int32, imp_t.shape, 0)
    cur = tpos_row // SEL_BLOCK
    forced = (blk == 0) | (blk == cur) | (blk == cur - 1)
    score = jnp.where(blk * SEL_BLOCK <= tpos_row, imp_t + FORCE_BONUS * jnp.where(forced, 1.0, 0.0), NEG_INF)
    rank = jnp.zeros(score.shape, F32)
    for n2 in range(n_sel):
        row = score[n2:n2 + 1, :]
        rank = rank + jnp.where(row > score, 1.0, 0.0) + jnp.where((row == score) & (blk > n2), 1.0, 0.0)
    return jnp.where(rank < min(SEL_TOP_N, n_sel), 1.0, 0.0)


def _select_blocks(imp, tpos, n_sel):
    blk = lax.broadcasted_iota(jnp.int32, imp.shape, 1)
    cur = tpos // SEL_BLOCK
    forced = (blk == 0) | (blk == cur) | (blk == cur - 1)
    allowed = (blk * SEL_BLOCK <= tpos) & (blk < n_sel)
    score = jnp.where(allowed, imp + FORCE_BONUS * jnp.where(forced, 1.0, 0.0), NEG_INF)
    rank = _top_rank(score, n_sel)
    return jnp.where((rank < min(SEL_TOP_N, n_sel)) & (blk < n_sel), 1.0, 0.0)


def _kv_rows(ref, token0, n_tokens, slab, n_slab):
    return ref[pl.ds(token0 * n_slab + slab, n_tokens, stride=n_slab), :]


def _causal_bias(bias, tq):
    w = bias.shape[1]
    row = lax.broadcasted_iota(jnp.int32, (tq, tq), 0)
    col = lax.broadcasted_iota(jnp.int32, (tq, tq), 1)
    diag = jnp.where(row >= col, bias[:, w - tq:], MASKED)
    return diag if w == tq else jnp.concatenate([bias[:, :w - tq], diag], axis=1)


def _nsa_prompt_kernel(q_ref, gate_ref, kcvc_ref, sel_ref, win_ref, cover_ref, expand_ref, wbias_ref, o_ref,
                       ks_scr, vs_scr, kw_scr, vw_scr, *, tq, seq):
    g = pl.program_id(1)
    qi = pl.program_id(2)
    nq = seq // tq

    @pl.when(qi == 0)
    def _():
        ks_scr[...] = _kv_rows(sel_ref, 0, seq, g, 4).astype(BF16)
        vs_scr[...] = _kv_rows(sel_ref, 0, seq, 2 + g, 4).astype(BF16)
        kw_scr[...] = _kv_rows(win_ref, 0, seq, g, 4).astype(BF16)
        vw_scr[...] = _kv_rows(win_ref, 0, seq, 2 + g, 4).astype(BF16)

    n_sel = seq // SEL_BLOCK
    n_sub = seq // CMP_STRIDE
    n_win = WINDOW // tq
    tpos = qi * tq + lax.broadcasted_iota(jnp.int32, (tq, 1), 0)
    tpos_row = qi * tq + lax.broadcasted_iota(jnp.int32, (1, tq), 1)
    q = q_ref[...]

    n_idx = lax.broadcasted_iota(jnp.int32, (1, n_sub), 1)
    valid = (n_idx * CMP_STRIDE + (CMP_LEN - 1) <= tpos) & (n_idx < n_sub - 1)
    o_cmp, imp_t = _cmp_branch(q, kcvc_ref[0, 0, 0].astype(BF16), kcvc_ref[0, 1, 0].astype(BF16), cover_ref[...],
                               valid)
    sel_t = _select_blocks_t(imp_t[:n_sel], tpos_row, n_sel)
    selected = jnp.concatenate([sel_t, jnp.zeros((LANES - n_sel, tq), F32)], axis=0).T.astype(BF16)

    slopes = _alibi_slopes(NSA_HEADS)
    slope = [jnp.where(g == 0, jnp.float32(slopes[r] * LOG2E), jnp.float32(slopes[NSA_GROUP + r] * LOG2E))
             for r in range(NSA_GROUP)]
    gates = _sigmoid(gate_ref[...])

    for c in range(nq):
        @pl.when(qi == c)
        def _(c=c):
            w = (c + 1) * tq
            bias = _causal_bias(_dot(selected, expand_ref[:, :w]) + MASKED, tq)
            k = ks_scr[:w, :]
            v = vs_scr[:w, :]
            kpos = lax.broadcasted_iota(jnp.int32, (1, w), 1).astype(F32)
            c0 = max(c - n_win, 0)
            ww = (c + 1 - c0) * tq
            kw = kw_scr[c0 * tq:(c + 1) * tq, :]
            vw = vw_scr[c0 * tq:(c + 1) * tq, :]
            wbias = wbias_ref[:, (n_win + 1) * tq - ww:]
            kpos_w = (c0 * tq + lax.broadcasted_iota(jnp.int32, (1, ww), 1)).astype(F32)
            for r in range(NSA_GROUP):
                qr = q[:, r * HEAD_DIM:(r + 1) * HEAD_DIM]
                o_sel = _biased_attend(qr, k, v, bias, slope[r] * kpos)
                o_win = _biased_attend(qr, kw, vw, wbias, slope[r] * kpos_w)
                o_ref[:, r * HEAD_DIM:(r + 1) * HEAD_DIM] = (gates[:, 3 * r:3 * r + 1] * o_cmp[r]
                                                              + gates[:, 3 * r + 1:3 * r + 2] * o_sel
                                                              + gates[:, 3 * r + 2:3 * r + 3] * o_win)


def nsa_prompt_constants(n_sub, n_keys, tq):
    cover, _ = nsa_constants(n_sub, LANES)
    expand = np.where(np.arange(n_keys)[None, :] // SEL_BLOCK == np.arange(LANES)[:, None], -MASKED, 0.0)
    dist = WINDOW + np.arange(tq)[:, None] - np.arange(WINDOW + tq)[None, :]
    wbias = np.where((dist >= 0) & (dist < WINDOW), 0.0, MASKED)
    return cover, jnp.asarray(expand, BF16), jnp.asarray(wbias, F32)


def nsa_constants(n_sub, n_keys):
    c_start = np.arange(n_sub)[:, None] * CMP_STRIDE
    s_start = np.arange(LANES)[None, :] * SEL_BLOCK
    cover = ((c_start < s_start + SEL_BLOCK) & (c_start + CMP_LEN > s_start)).astype(np.float32)
    cover[n_sub - 1:] = 0.0
    expand = (np.arange(n_keys)[None, :] // SEL_BLOCK == np.arange(LANES)[:, None]).astype(np.float32)
    return jnp.asarray(cover, BF16), jnp.asarray(expand, BF16)


def nsa_prompt(q_all, gates, kcvc, sel_kv, win_kv, batch, seq, tq=256):
    assert WINDOW % tq == 0 and seq % tq == 0 and (seq // SEL_BLOCK) % 8 == 0
    nq = seq // tq
    n_sub = seq // CMP_STRIDE
    cover, expand, wbias = nsa_prompt_constants(n_sub, seq, tq)
    gw = NSA_GROUP * HEAD_DIM
    kv_spec = pl.BlockSpec((seq * 4, LANES), lambda b, g, i: (b, 0))
    const = lambda a: pl.BlockSpec(a.shape, lambda b, g, i: (0, 0))
    return pl.pallas_call(
        functools.partial(_nsa_prompt_kernel, tq=tq, seq=seq),
        grid=(batch, NSA_KV_HEADS, nq),
        in_specs=[
            pl.BlockSpec((tq, gw), lambda b, g, i: (b * nq + i, g)),
            pl.BlockSpec((tq, LANES), lambda b, g, i: (b * nq + i, g)),
            pl.BlockSpec((1, 2, 1, n_sub, HEAD_DIM), lambda b, g, i: (b, 0, g, 0, 0)),
            kv_spec, kv_spec, const(cover), const(expand), const(wbias),
        ],
        out_specs=pl.BlockSpec((tq, gw), lambda b, g, i: (b * nq + i, g)),
        out_shape=jax.ShapeDtypeStruct((batch * seq, NSA_HEADS * HEAD_DIM), F32),
        scratch_shapes=[pltpu.VMEM((seq, HEAD_DIM), BF16)] * 4,
        compiler_params=pltpu.CompilerParams(dimension_semantics=("parallel", "parallel", "arbitrary"),
                                             vmem_limit_bytes=VMEM_LIMIT),
        name="nsa_prompt",
    )(q_all, gates, kcvc, sel_kv, win_kv, cover, expand, wbias)


def _moba_select(gate, n_past, n_blocks):
    blk = lax.broadcasted_iota(jnp.int32, gate.shape, 1)
    past = blk < n_past
    score = jnp.where(past, gate, NEG_INF)
    rank = _top_rank(score, n_blocks)
    return jnp.where(past & (rank < min(MOBA_TOP_K, n_blocks)), 1.0, 0.0)


def _moba_select_t(gate_t, n_past, n_blocks):
    blk = lax.broadcasted_iota(jnp.int32, gate_t.shape, 0)
    past = blk < n_past
    score = jnp.where(past, gate_t, NEG_INF)
    rank = jnp.zeros(score.shape, F32)
    for n2 in range(n_blocks):
        row = score[n2:n2 + 1, :]
        rank = rank + jnp.where(row > score, 1.0, 0.0) + jnp.where((row == score) & (blk > n2), 1.0, 0.0)
    return jnp.where(past & (rank < min(MOBA_TOP_K, n_blocks)), 1.0, 0.0)


def _moba_prompt_kernel(q_ref, kv_ref, o_ref, kmean_scr, k_scr, v_scr, *, seq):
    tq = MOBA_BLOCK
    n_slab = 2 * MOBA_HEADS
    qi = pl.program_id(1)
    nb = seq // tq
    slopes = [s * LOG2E for s in _alibi_slopes(MOBA_HEADS)]

    @pl.when(qi == 0)
    def _():
        kmean_scr[...] = jnp.zeros(kmean_scr.shape, F32)
        for h in range(MOBA_HEADS):
            k = _kv_rows(kv_ref, 0, seq, h, n_slab)
            k_scr[h] = k.astype(BF16)
            v_scr[h] = _kv_rows(kv_ref, 0, seq, MOBA_HEADS + h, n_slab).astype(BF16)
            for n in range(nb):
                kmean_scr[h, n:n + 1, :] = jnp.mean(k[n * tq:(n + 1) * tq], axis=0, keepdims=True)

    row = lax.broadcasted_iota(jnp.int32, (tq, tq), 0)
    col = lax.broadcasted_iota(jnp.int32, (tq, tq), 1)
    own = jnp.where(row >= col, 0.0, MASKED)
    q, sel_bias = [], []
    nb8 = -(-nb // 8) * 8
    for h in range(MOBA_HEADS):
        q.append(q_ref[:, h * HEAD_DIM:(h + 1) * HEAD_DIM])
        gate_t = _dot_nt(kmean_scr[h].astype(BF16), q[h])[:nb8]
        bias_t = (1.0 - _moba_select_t(gate_t, qi, nb - 1)) * MASKED
        sel_bias.append(jnp.concatenate([bias_t, jnp.zeros((LANES - nb8, tq), F32)], axis=0).T)

    for c in range(nb):
        @pl.when(qi == c)
        def _(c=c):
            w = (c + 1) * tq
            kpos = lax.broadcasted_iota(jnp.int32, (1, w), 1).astype(F32)
            for h in range(MOBA_HEADS):
                bias = jnp.concatenate(
                    [jnp.broadcast_to(sel_bias[h][:, n:n + 1], (tq, tq)) for n in range(c)] + [own], axis=1)
                o_ref[:, h * HEAD_DIM:(h + 1) * HEAD_DIM] = _biased_attend(q[h], k_scr[h, :w, :], v_scr[h, :w, :], bias,
                                                                            slopes[h] * kpos)


def moba_prompt(q_all, q_col, moba_kv, batch, seq):
    tq = MOBA_BLOCK
    nq = seq // tq
    width = MOBA_HEADS * HEAD_DIM
    return pl.pallas_call(
        functools.partial(_moba_prompt_kernel, seq=seq),
        grid=(batch, nq),
        in_specs=[
            pl.BlockSpec((tq, width), lambda b, i: (b * nq + i, q_col)),
            pl.BlockSpec((seq * 2 * MOBA_HEADS, LANES), lambda b, i: (b, 0)),
        ],
        out_specs=pl.BlockSpec((tq, width), lambda b, i: (b * nq + i, 0)),
        out_shape=jax.ShapeDtypeStruct((batch * seq, width), F32),
        scratch_shapes=[pltpu.VMEM((MOBA_HEADS, LANES, HEAD_DIM), F32), pltpu.VMEM((MOBA_HEADS, seq, HEAD_DIM), BF16),
                        pltpu.VMEM((MOBA_HEADS, seq, HEAD_DIM), BF16)],
        compiler_params=pltpu.CompilerParams(dimension_semantics=("parallel", "arbitrary"),
                                             vmem_limit_bytes=VMEM_LIMIT),
        name="moba_prompt",
    )(q_all, moba_kv)


def _padded_tile(rows, n):
    return jnp.concatenate([rows, jnp.zeros((n - rows.shape[0], rows.shape[1]), F32)], axis=0).astype(BF16)


def _nsa_sample_kernel(pt_ref, *refs, n_pages, page, t_new, n_win, bb):
    page_refs = refs[:bb * n_pages]
    (q_ref, gate_ref, kcvc_ref, new_sel_ref, cwin_ref, new_win_ref, cover_ref, expand_ref, sbias_ref, wbias_ref,
     o_ref, win_out_ref, ks_scr, vs_scr, kw_scr, vw_scr) = refs[bb * n_pages:]
    n_slab = 2 * NSA_KV_HEADS
    past = n_pages * page
    n_sub = past // CMP_STRIDE
    n_sel = -(-(past + t_new) // SEL_BLOCK)
    rows = NSA_GROUP * t_new
    keep = win_out_ref.shape[0] // (bb * n_slab)
    chains = [(i, g) for i in range(bb) for g in range(NSA_KV_HEADS)]

    for i in range(bb):
        out0, kept = i * keep * n_slab, (keep - t_new) * n_slab
        win_out_ref[out0:out0 + kept, :] = cwin_ref[(i + 1) * n_win * n_slab - kept:(i + 1) * n_win * n_slab, :]
        win_out_ref[out0 + kept:out0 + keep * n_slab, :] = new_win_ref[i * t_new * n_slab:(i + 1) * t_new * n_slab, :]

    for c, (i, g) in enumerate(chains):
        for p in range(n_pages):
            ks_scr[c, p * page:(p + 1) * page, :] = _kv_rows(page_refs[i * n_pages + p], 0, page, g, n_slab).astype(BF16)
            vs_scr[c, p * page:(p + 1) * page, :] = (
                _kv_rows(page_refs[i * n_pages + p], 0, page, 2 + g, n_slab).astype(BF16))
        ks_scr[c, past:, :] = _padded_tile(_kv_rows(new_sel_ref, i * t_new, t_new, g, n_slab), LANES)
        vs_scr[c, past:, :] = _padded_tile(_kv_rows(new_sel_ref, i * t_new, t_new, 2 + g, n_slab), LANES)
        kw_scr[c, :n_win, :] = _kv_rows(cwin_ref, i * n_win, n_win, g, n_slab).astype(BF16)
        vw_scr[c, :n_win, :] = _kv_rows(cwin_ref, i * n_win, n_win, 2 + g, n_slab).astype(BF16)
        kw_scr[c, n_win:, :] = _padded_tile(_kv_rows(new_win_ref, i * t_new, t_new, g, n_slab), LANES)
        vw_scr[c, n_win:, :] = _padded_tile(_kv_rows(new_win_ref, i * t_new, t_new, 2 + g, n_slab), LANES)

    q_all = [_stack_heads(q_ref, i * t_new, t_new, NSA_HEADS) for i in range(bb)]
    q = [q_all[i][g * rows:(g + 1) * rows] for i, g in chains]
    tpos = past + lax.broadcasted_iota(jnp.int32, (rows, 1), 0) % t_new
    tpos_q = past + lax.broadcasted_iota(jnp.int32, (t_new, 1), 0)
    n_idx = lax.broadcasted_iota(jnp.int32, (1, n_sub), 1)
    valid = (n_idx * CMP_STRIDE + (CMP_LEN - 1) <= tpos) & (n_idx < n_sub - 1)
    cmp_out = [_cmp_branch_rows(q[c], kcvc_ref[i, 0, g].astype(BF16), kcvc_ref[i, 1, g].astype(BF16),
                                cover_ref[...], valid) for c, (i, g) in enumerate(chains)]
    selected = []
    for c in range(len(chains)):
        imp = cmp_out[c][1]
        imp_t = imp[0:t_new]
        for r in range(1, NSA_GROUP):
            imp_t = imp_t + imp[r * t_new:(r + 1) * t_new]
        selected.append(jnp.concatenate([_select_blocks(imp_t, tpos_q, n_sel)] * NSA_GROUP, axis=0).astype(BF16))
    o_sel, o_win = [], []
    for c, (i, g) in enumerate(chains):
        bias = (_dot(selected[c], expand_ref[...]) + MASKED) + sbias_ref[g]
        o_sel.append(_softmax_attend(_dot_nt(q[c], ks_scr[c]) * (SCALE * LOG2E) + bias, vs_scr[c]))
        o_win.append(_softmax_attend(_dot_nt(q[c], kw_scr[c]) * (SCALE * LOG2E) + wbias_ref[g], vw_scr[c]))

    gates = _sigmoid(gate_ref[...])
    for c, (i, g) in enumerate(chains):
        out_rows = slice(i * t_new, (i + 1) * t_new)
        for r in range(NSA_GROUP):
            h = g * NSA_GROUP + r
            lane = g * LANES + 3 * r
            rs = slice(r * t_new, (r + 1) * t_new)
            o_ref[out_rows, h * HEAD_DIM:(h + 1) * HEAD_DIM] = (
                gates[out_rows, lane:lane + 1] * cmp_out[c][0][rs]
                + gates[out_rows, lane + 1:lane + 2] * o_sel[c][rs]
                + gates[out_rows, lane + 2:lane + 3] * o_win[c][rs])


def _cmp_branch_rows(q, kc, vc, cover, valid):
    s = jnp.where(valid, _dot_nt(q, kc) * SCALE, NEG_INF)
    e = jnp.where(valid, jnp.exp(s - jnp.max(s, axis=-1, keepdims=True)), 0.0)
    d = jnp.sum(e, axis=-1, keepdims=True)
    p = e / jnp.where(d > 0, d, 1.0)
    p_hi = p.astype(BF16)
    p_lo = (p - p_hi.astype(F32)).astype(BF16)
    return _dot(p_hi, vc), _dot(p_hi, cover) + _dot(p_lo, cover)


def nsa_sample(q_all, gates, kcvc, sel_pool, page_table, page, new_sel, cache_win, new_win, t_new):
    nb, n_pages = page_table.shape
    past = n_pages * page
    n_sub = past // CMP_STRIDE
    n_slab = 2 * NSA_KV_HEADS
    n_win = cache_win.shape[0] // (n_slab * nb)
    assert past % SEL_BLOCK == 0 and t_new <= SEL_BLOCK and past // SEL_BLOCK + 2 <= LANES
    cover, expand = nsa_constants(n_sub, past + LANES)
    expand = expand * jnp.asarray(-MASKED, BF16)
    slopes = _alibi_slopes(NSA_HEADS)
    t_pos = past + np.arange(t_new)

    def alibi_bias(pos, ok):
        per_head = [np.where(ok, s * LOG2E * pos[None, :], MASKED) for s in slopes]
        return jnp.asarray(np.stack(per_head).reshape(NSA_KV_HEADS, NSA_GROUP * t_new, pos.shape[0]), F32)

    sel_pos = np.arange(past + LANES)
    sbias = alibi_bias(sel_pos, (sel_pos[None, :] <= t_pos[:, None]) & (sel_pos[None, :] < past + t_new))
    win_pos = past - n_win + np.arange(n_win + LANES)
    dist = t_pos[:, None] - win_pos[None, :]
    wbias = alibi_bias(win_pos, (dist >= 0) & (dist < WINDOW) & (win_pos[None, :] >= 0)
                       & (win_pos[None, :] < past + t_new))
    keep = min(WINDOW, n_win + t_new)
    assert keep >= t_new and (n_slab * (keep - t_new)) % 8 == 0
    bb = 2 if nb % 2 == 0 else 1
    n_chain = bb * NSA_KV_HEADS
    width = NSA_HEADS * HEAD_DIM
    const = lambda a: pl.BlockSpec(a.shape, lambda b, pt: (0,) * a.ndim)
    rows = lambda n: pl.BlockSpec((bb * n_slab * n, LANES), lambda b, pt: (b, 0))
    grid_spec = pltpu.PrefetchScalarGridSpec(
        num_scalar_prefetch=1,
        grid=(nb // bb,),
        in_specs=_page_specs(n_pages, n_slab * page, bb) + [
            pl.BlockSpec((bb * t_new, width), lambda b, pt: (b, 0)),
            pl.BlockSpec((bb * t_new, NSA_KV_HEADS * LANES), lambda b, pt: (b, 0)),
            pl.BlockSpec((bb, 2, 2, n_sub, HEAD_DIM), lambda b, pt: (b, 0, 0, 0, 0)),
            rows(t_new), rows(n_win), rows(t_new),
            const(cover), const(expand), const(sbias), const(wbias)],
        out_specs=[pl.BlockSpec((bb * t_new, width), lambda b, pt: (b, 0)), rows(keep)],
        scratch_shapes=[pltpu.VMEM((n_chain, past + LANES, HEAD_DIM), BF16),
                        pltpu.VMEM((n_chain, past + LANES, HEAD_DIM), BF16),
                        pltpu.VMEM((n_chain, n_win + LANES, HEAD_DIM), BF16),
                        pltpu.VMEM((n_chain, n_win + LANES, HEAD_DIM), BF16)],
    )
    return pl.pallas_call(
        functools.partial(_nsa_sample_kernel, n_pages=n_pages, page=page, t_new=t_new, n_win=n_win, bb=bb),
        grid_spec=grid_spec,
        out_shape=[jax.ShapeDtypeStruct((nb * t_new, width), F32),
                   jax.ShapeDtypeStruct((nb * keep * n_slab, LANES), F32)],
        compiler_params=_params(1, VMEM_LIMIT),
        name="nsa_sample",
    )(page_table.reshape(-1), *([sel_pool] * (bb * n_pages)), q_all, gates, kcvc, new_sel, cache_win, new_win, cover,
      expand, sbias, wbias)


def _slab_attend(q, kv, bias, shift):
    s = _dot_nt(q, kv) * (SCALE * LOG2E) + bias
    e = jnp.exp2(s - jnp.max(s, axis=-1, keepdims=True))
    d = jnp.sum(e, axis=-1, keepdims=True)
    return _dot(pltpu.roll(e, shift, axis=1).astype(BF16), kv) / jnp.where(d > 0, d, 1.0)


def _stack_heads(ref, row0, n_rows, n_heads):
    return jnp.concatenate([ref[row0:row0 + n_rows, h * HEAD_DIM:(h + 1) * HEAD_DIM] for h in range(n_heads)],
                           axis=0).astype(BF16)


def _slab_bias(head_slab, slopes, key_pos, key_ok, n_slab, t_new):
    n_heads, n_keys = len(head_slab), key_pos.shape[0]
    slab = np.arange(n_keys * n_slab) % n_slab
    out = np.full((n_heads, t_new, n_keys * n_slab), MASKED, np.float64)
    for h in range(n_heads):
        ok = np.repeat(key_ok, n_slab, axis=1) & (slab == head_slab[h])[None, :]
        out[h] = np.where(ok, slopes[h] * LOG2E * np.repeat(key_pos, n_slab)[None, :], MASKED)
    return jnp.asarray(out.reshape(n_heads * t_new, n_keys * n_slab), F32)


def _moba_sample_kernel(pt_ref, *refs, n_pages, page, t_new):
    page_refs = refs[:n_pages]
    q_ref, new_ref, bias_ref, o_ref, kv_scr, mean_scr = refs[n_pages:]
    n_slab = 2 * MOBA_HEADS
    rows_pp = page * n_slab
    past = n_pages * page
    pages_per_block = MOBA_BLOCK // page
    n_past = n_pages // pages_per_block
    block_w = MOBA_BLOCK * n_slab

    mean_scr[...] = jnp.zeros(mean_scr.shape, F32)
    for n in range(n_past):
        total = None
        for p in range(n * pages_per_block, (n + 1) * pages_per_block):
            x = page_refs[p][...]
            kv_scr[p * rows_pp:(p + 1) * rows_pp, :] = x.astype(BF16)
            part = jnp.sum(x.reshape(page, n_slab, LANES), axis=0)
            total = part if total is None else total + part
        mean_scr[n * n_slab:(n + 1) * n_slab, :] = total * (1.0 / MOBA_BLOCK)
    x_new = new_ref[...]
    kv_scr[past * n_slab:, :] = _padded_tile(x_new, LANES)
    mean_scr[n_past * n_slab:(n_past + 1) * n_slab, :] = (
        jnp.sum(x_new.reshape(t_new, n_slab, LANES), axis=0) * (1.0 / MOBA_BLOCK))

    tpos = past + lax.broadcasted_iota(jnp.int32, (t_new, 1), 0)
    cur = tpos // MOBA_BLOCK
    blk = lax.broadcasted_iota(jnp.int32, (t_new, LANES), 1)
    sel_bias = []
    for h in range(MOBA_HEADS):
        q = q_ref[:, h * HEAD_DIM:(h + 1) * HEAD_DIM].astype(BF16)
        kmean = mean_scr[pl.ds(h, LANES // n_slab, stride=n_slab), :]
        kmean = jnp.concatenate([kmean, jnp.zeros((LANES - kmean.shape[0], LANES), F32)], axis=0)
        gate = _dot_nt(q, kmean.astype(BF16))
        selected = jnp.where(blk == cur, 1.0, _moba_select(gate, cur, n_past))
        sel_bias.append((1.0 - selected) * MASKED)
    sel_bias = jnp.concatenate(sel_bias, axis=0)

    bias = jnp.concatenate(
        [bias_ref[:, n * block_w:(n + 1) * block_w] + sel_bias[:, n:n + 1] for n in range(n_past)]
        + [bias_ref[:, n_past * block_w:] + sel_bias[:, n_past:n_past + 1]], axis=1)
    o = _slab_attend(_stack_heads(q_ref, 0, t_new, MOBA_HEADS), kv_scr[...], bias, MOBA_HEADS)
    for h in range(MOBA_HEADS):
        o_ref[:, h * HEAD_DIM:(h + 1) * HEAD_DIM] = o[h * t_new:(h + 1) * t_new]


def moba_sample(q_all, q_col, moba_pool, page_table, page, new_kv, t_new):
    nb, n_pages = page_table.shape
    past = n_pages * page
    width = MOBA_HEADS * HEAD_DIM
    n_slab = 2 * MOBA_HEADS
    assert past % MOBA_BLOCK == 0 and t_new * n_slab <= LANES and past // MOBA_BLOCK < LANES // n_slab
    new_keys = LANES // n_slab
    key_pos = np.arange(past + new_keys)
    t_pos = past + np.arange(t_new)
    key_ok = (key_pos[None, :] <= t_pos[:, None]) & (key_pos[None, :] < past + t_new)
    bias = _slab_bias(list(range(MOBA_HEADS)), _alibi_slopes(MOBA_HEADS), key_pos, key_ok, n_slab, t_new)
    grid_spec = pltpu.PrefetchScalarGridSpec(
        num_scalar_prefetch=1,
        grid=(nb,),
        in_specs=_page_specs(n_pages, n_slab * page) + [
            pl.BlockSpec((t_new, width), lambda b, pt: (b, q_col)),
            pl.BlockSpec((t_new * n_slab, LANES), lambda b, pt: (b, 0)),
            pl.BlockSpec(bias.shape, lambda b, pt: (0, 0))],
        out_specs=pl.BlockSpec((t_new, width), lambda b, pt: (b, 0)),
        scratch_shapes=[pltpu.VMEM((past * n_slab + LANES, HEAD_DIM), BF16), pltpu.VMEM((LANES, LANES), F32)],
    )
    return pl.pallas_call(
        functools.partial(_moba_sample_kernel, n_pages=n_pages, page=page, t_new=t_new),
        grid_spec=grid_spec,
        out_shape=jax.ShapeDtypeStruct((nb * t_new, width), F32),
        compiler_params=_params(1, VMEM_LIMIT),
        name="moba_sample",
    )(page_table.reshape(-1), *([moba_pool] * n_pages), q_all, new_kv, bias)


def _cross_kernel(q_ref, kv_ref, o_ref, *, n_mem):
    n_slab = 2 * MEM_HEADS
    for h in range(MEM_HEADS):
        k = _kv_rows(kv_ref, 0, n_mem, h, n_slab).astype(BF16)
        v = _kv_rows(kv_ref, 0, n_mem, MEM_HEADS + h, n_slab).astype(BF16)
        s = _dot_nt(q_ref[:, h * HEAD_DIM:(h + 1) * HEAD_DIM], k) * SCALE
        e = jnp.exp(s - jnp.max(s, axis=-1, keepdims=True))
        p = e / jnp.sum(e, axis=-1, keepdims=True)
        o_ref[:, h * HEAD_DIM:(h + 1) * HEAD_DIM] = _dot(p.astype(BF16), v)


def cross_attend(q_all, q_col, mem_kv, batch, rows, tq):
    nq = rows // tq
    n_slab = 2 * MEM_HEADS
    width = MEM_HEADS * HEAD_DIM
    n_mem = mem_kv.shape[0] // (batch * n_slab)
    return pl.pallas_call(
        functools.partial(_cross_kernel, n_mem=n_mem),
        grid=(batch, nq),
        in_specs=[
            pl.BlockSpec((tq, width), lambda b, i: (b * nq + i, q_col)),
            pl.BlockSpec((n_mem * n_slab, LANES), lambda b, i: (b, 0)),
        ],
        out_specs=pl.BlockSpec((tq, width), lambda b, i: (b * nq + i, 0)),
        out_shape=jax.ShapeDtypeStruct((batch * rows, width), F32),
        compiler_params=_params(2, VMEM_LIMIT),
        name="cross_attend",
    )(q_all, mem_kv)


def _cross_sample_kernel(q_ref, kv_ref, bias_ref, o_ref, *, n_mem, t_new, bb):
    n_slab = 2 * MEM_HEADS
    rows = n_mem * n_slab
    for i in range(bb):
        q = _stack_heads(q_ref, i * t_new, t_new, MEM_HEADS)
        o = _slab_attend(q, kv_ref[i * rows:(i + 1) * rows, :].astype(BF16), bias_ref[...], MEM_HEADS)
        for h in range(MEM_HEADS):
            o_ref[i * t_new:(i + 1) * t_new, h * HEAD_DIM:(h + 1) * HEAD_DIM] = o[h * t_new:(h + 1) * t_new]


def cross_attend_sample(q_all, q_col, mem_kv, batch, t_new, bb):
    assert batch % bb == 0
    n_slab = 2 * MEM_HEADS
    width = MEM_HEADS * HEAD_DIM
    n_mem = mem_kv.shape[0] // (batch * n_slab)
    bias = _slab_bias(list(range(MEM_HEADS)), [0.0] * MEM_HEADS, np.arange(n_mem), np.ones((t_new, n_mem), bool),
                      n_slab, t_new)
    return pl.pallas_call(
        functools.partial(_cross_sample_kernel, n_mem=n_mem, t_new=t_new, bb=bb),
        grid=(batch // bb,),
        in_specs=[
            pl.BlockSpec((bb * t_new, width), lambda b: (b, q_col)),
            pl.BlockSpec((bb * n_mem * n_slab, LANES), lambda b: (b, 0)),
            pl.BlockSpec(bias.shape, lambda b: (0, 0)),
        ],
        out_specs=pl.BlockSpec((bb * t_new, width), lambda b: (b, 0)),
        out_shape=jax.ShapeDtypeStruct((batch * t_new, width), F32),
        compiler_params=_params(1, VMEM_LIMIT),
        name="cross_sample",
    )(q_all, mem_kv, bias)


def _mix_kernel(on_ref, om_ref, oc_ref, z_ref, g0_ref, g1_ref, g2_ref, wn_ref, wm_ref, wc_ref, o_ref,
                un_scr, um_scr, uc_scr):
    @pl.when(pl.program_id(1) == 0)
    def _():
        wn_k, wm_k = un_scr.shape[1], um_scr.shape[1]
        z = z_ref[...]
        act = z * _sigmoid(z)
        un_scr[...] = (on_ref[...] * act[:, :wn_k]).astype(BF16)
        um_scr[...] = (om_ref[...] * act[:, wn_k:wn_k + wm_k]).astype(BF16)
        uc_scr[...] = (oc_ref[...] * act[:, wn_k + wm_k:]).astype(BF16)

    mix = (_sigmoid(g0_ref[...]) * _dot(un_scr[...], wn_ref[...])
           + _sigmoid(g1_ref[...]) * _dot(um_scr[...], wm_ref[...])
           + _sigmoid(g2_ref[...]) * _dot(uc_scr[...], wc_ref[...]))
    o_ref[...] = mix.astype(o_ref.dtype)


def _out_kernel(x_ref, mix_ref, wo_ref, fg_ref, y_ref):
    y = x_ref[...] + _dot(mix_ref[...], wo_ref[...])
    inv = lax.rsqrt(jnp.mean(y * y, axis=-1, keepdims=True) + RMS_EPS)
    y_ref[...] = y * inv * fg_ref[...]


def merge_branches(x, o_nsa, o_moba, o_mem, z_all, gate_logits, w_nsa, w_moba, w_mem, w_out, final_g, tm=512, tn=512):
    m, d = x.shape
    tm = min(tm, m)
    nj = d // tn
    row = lambda a: pl.BlockSpec((tm, a.shape[1]), lambda i, j: (i, 0))
    gate = lambda k: pl.BlockSpec((tm, tn), lambda i, j: (i, k * nj + j))
    wcol = lambda a: pl.BlockSpec((a.shape[0], tn), lambda i, j: (0, j))
    mix = pl.pallas_call(
        _mix_kernel,
        grid=(m // tm, nj),
        in_specs=[row(o_nsa), row(o_moba), row(o_mem), row(z_all), gate(0), gate(1), gate(2),
                  wcol(w_nsa), wcol(w_moba), wcol(w_mem)],
        out_specs=pl.BlockSpec((tm, tn), lambda i, j: (i, j)),
        out_shape=jax.ShapeDtypeStruct((m, d), BF16),
        scratch_shapes=[pltpu.VMEM((tm, a.shape[1]), BF16) for a in (o_nsa, o_moba, o_mem)],
        compiler_params=pltpu.CompilerParams(dimension_semantics=("parallel", "arbitrary"),
                                             vmem_limit_bytes=VMEM_LIMIT),
        name="merge_mix",
    )(o_nsa, o_moba, o_mem, z_all, gate_logits, gate_logits, gate_logits, w_nsa, w_moba, w_mem)
    rows = lambda a: pl.BlockSpec((tm, a.shape[1]), lambda i: (i, 0))
    whole = lambda a: pl.BlockSpec(a.shape, lambda i: (0, 0), pipeline_mode=pl.Buffered(1))
    final_g = final_g.reshape(1, d)
    return pl.pallas_call(
        _out_kernel,
        grid=(m // tm,),
        in_specs=[rows(x), rows(mix), whole(w_out), whole(final_g)],
        out_specs=rows(x),
        out_shape=jax.ShapeDtypeStruct((m, d), F32),
        compiler_params=_params(1, VMEM_LIMIT),
        name="merge_out",
    )(x, mix, w_out, final_g)


def _in_weights(w_in, d_model):
    splits = (NSA_HEADS * HEAD_DIM, 4 * HEAD_DIM, 4 * HEAD_DIM, 4 * HEAD_DIM, 3 * NSA_HEADS, NSA_HEADS * HEAD_DIM,
              MOBA_HEADS * HEAD_DIM, 2 * MOBA_HEADS * HEAD_DIM, MOBA_HEADS * HEAD_DIM, MEM_HEADS * HEAD_DIM,
              MEM_HEADS * HEAD_DIM, N_BRANCH * d_model)
    offs = np.cumsum((0,) + splits)
    assert offs[-1] == w_in.shape[1]
    (nsa_q, cmp_kv, sel_kv, win_kv, nsa_g, nsa_z, moba_q, moba_kv, moba_z, mem_q, mem_z, merge_g) = (
        w_in[:, offs[i]:offs[i + 1]] for i in range(len(splits)))
    per_group = 3 * NSA_GROUP
    gate_w = jnp.pad(nsa_g.reshape(-1, NSA_KV_HEADS, per_group), ((0, 0), (0, 0), (0, LANES - per_group)))
    cast = lambda w: w.astype(BF16)
    return dict(q=cast(jnp.concatenate([nsa_q, moba_q, mem_q], axis=1)), cmp=cast(cmp_kv), sel=cast(sel_kv),
                win=cast(win_kv), moba=cast(moba_kv), gate=cast(gate_w.reshape(-1, NSA_KV_HEADS * LANES)),
                z=cast(jnp.concatenate([nsa_z, moba_z, mem_z], axis=1)), merge=cast(merge_g))


def _in_projection(x, norm_g, w, q_dtype):
    tm = min(1024, x.shape[0] // 2)
    xn = rmsnorm_bf16(x, norm_g, 512)
    mm = lambda name, dtype: matmul(xn, w[name], dtype, 2 * tm, 1024, "in_" + name)
    rows = lambda name: matmul_rows(xn, w[name], tm, "in_" + name)
    return dict(q=mm("q", q_dtype), cmp=rows("cmp"), sel=rows("sel"), win=rows("win"), moba=rows("moba"),
                gate=mm("gate", F32), z=mm("z", F32), merge=mm("merge", F32))


def kernel(x_prompt, x_sample, cache_cmp_kv, cache_sel_kv, cache_moba_kv, cache_win_kv, cache_mem_kv, page_table,
           mem_prompt, norm_g, w_in, phi_pos, phi_w1, phi_w2, mem_norm_g, w_mem_kv, w_br_nsa, w_br_moba, w_br_mem,
           w_out, final_norm_g):
    batch, seq, d_model = x_prompt.shape
    nb, t_new, _ = x_sample.shape
    n_mem = mem_prompt.shape[1]
    page = cache_cmp_kv.shape[2]
    n_win = cache_win_kv.shape[2]
    assert norm_g.shape[0] == 1, "one layer"
    assert seq % MOBA_BLOCK == 0 and (page_table.shape[1] * page) % MOBA_BLOCK == 0 and MOBA_BLOCK % page == 0
    l = 0
    q_moba_col = NSA_HEADS * HEAD_DIM // (MOBA_HEADS * HEAD_DIM)
    q_mem_col = (NSA_HEADS + MOBA_HEADS) * HEAD_DIM // (MEM_HEADS * HEAD_DIM)
    lane_rows = lambda a: a.reshape(-1, LANES)

    w = _in_weights(w_in[l], d_model)
    w1cat, pos8, w2 = compress_weights(phi_pos[l], phi_w1[l], phi_w2[l])
    br = [a[l].astype(BF16) for a in (w_br_nsa, w_br_moba, w_br_mem, w_out)]

    xp = x_prompt.reshape(batch * seq, d_model)
    hp = _in_projection(xp, norm_g[l], w, BF16)
    mem_n = rmsnorm_bf16(mem_prompt.reshape(batch * n_mem, d_model), mem_norm_g[l], 512)
    mem_kv_p = matmul_rows(mem_n, w_mem_kv[l].astype(BF16), 1024, "mem_kv")
    pages_p = seq // page
    identity_table = jnp.arange(batch * pages_p, dtype=jnp.int32).reshape(batch, pages_p)
    kcvc_p = compress_tokens(hp["cmp"], identity_table, page, w1cat, pos8, w2)
    o_nsa_p = nsa_prompt(hp["q"], hp["gate"], kcvc_p, hp["sel"], hp["win"], batch, seq)
    o_moba_p = moba_prompt(hp["q"], q_moba_col, hp["moba"], batch, seq)
    o_mem_p = cross_attend(hp["q"], q_mem_col, mem_kv_p, batch, seq, min(seq, 1024))
    y_prompt = merge_branches(xp, o_nsa_p, o_moba_p, o_mem_p, hp["z"], hp["merge"], *br, final_norm_g)

    xs = x_sample.reshape(nb * t_new, d_model)
    hs = _in_projection(xs, norm_g[l], w, F32)
    kcvc_s = compress_tokens(lane_rows(cache_cmp_kv[l]), page_table, page, w1cat, pos8, w2)
    o_nsa_s, win_s = nsa_sample(hs["q"], hs["gate"], kcvc_s, lane_rows(cache_sel_kv[l]), page_table, page, hs["sel"],
                                lane_rows(cache_win_kv[l]), hs["win"], t_new)
    o_moba_s = moba_sample(hs["q"], q_moba_col, lane_rows(cache_moba_kv[l]), page_table, page, hs["moba"], t_new)
    o_mem_s = cross_attend_sample(hs["q"], q_mem_col, lane_rows(cache_mem_kv[l]), nb, t_new, 4 if nb % 4 == 0 else 1)
    y_sample = merge_branches(xs, o_nsa_s, o_moba_s, o_mem_s, hs["z"], hs["merge"], *br, final_norm_g)

    nsa_shape = lambda a, b_, t_: a.reshape(1, b_, t_, 2, NSA_KV_HEADS, HEAD_DIM)
    moba_shape = lambda a, b_, t_: a.reshape(1, b_, t_, 2, MOBA_HEADS, HEAD_DIM)
    keep_p = min(WINDOW, seq)
    keep_s = min(WINDOW, n_win + t_new)
    return (y_prompt.reshape(batch, seq, d_model), y_sample.reshape(nb, t_new, d_model),
            nsa_shape(hp["cmp"], batch, seq), nsa_shape(hp["sel"], batch, seq), moba_shape(hp["moba"], batch, seq),
            nsa_shape(hp["win"], batch, seq)[:, :, seq - keep_p:],
            mem_kv_p.reshape(1, batch, n_mem, 2, MEM_HEADS, HEAD_DIM),
            nsa_shape(hs["cmp"], nb, t_new), nsa_shape(hs["sel"], nb, t_new), moba_shape(hs["moba"], nb, t_new),
            nsa_shape(win_s, nb, keep_s))
```

```python
import functools

import numpy as np
import jax
import jax.numpy as jnp
from jax import lax
from jax.experimental import pallas as pl
from jax.experimental.pallas import tpu as pltpu

HEAD_DIM = 128
NSA_HEADS = 8
NSA_KV_HEADS = 2
NSA_GROUP = NSA_HEADS // NSA_KV_HEADS
CMP_LEN = 32
CMP_STRIDE = 16
SEL_BLOCK = 64
SEL_TOP_N = 16
WINDOW = 512
FORCE_BONUS = 1.0e4
MOBA_HEADS = 4
MOBA_BLOCK = 256
MOBA_TOP_K = 3
MEM_HEADS = 4
N_BRANCH = 3
RMS_EPS = 1e-6
NEG_INF = -1e30
SCALE = HEAD_DIM ** -0.5
LOG2E = 1.4426950408889634
MASKED = -(2.0 ** 100)

LANES = 128
STAGE_PITCH = 24
VMEM_LIMIT = 56 * 1024 * 1024

F32 = jnp.float32
BF16 = jnp.bfloat16


def _params(n_axes, vmem=None):
    return pltpu.CompilerParams(dimension_semantics=("parallel",) * n_axes, vmem_limit_bytes=vmem)


def _dot(a, b):
    return jnp.dot(a, b, preferred_element_type=F32)


def _dot_nt(a, b):
    return lax.dot_general(a, b, (((1,), (1,)), ((), ())), preferred_element_type=F32)


def _sigmoid(x):
    return 1.0 / (1.0 + jnp.exp(-x))


def _rms_kernel(x_ref, g_ref, o_ref):
    x = x_ref[...]
    inv = lax.rsqrt(jnp.mean(x * x, axis=-1, keepdims=True) + RMS_EPS)
    o_ref[...] = (x * inv * g_ref[...]).astype(o_ref.dtype)


def rmsnorm_bf16(x, g, tm):
    n, d = x.shape
    return pl.pallas_call(
        _rms_kernel,
        grid=(n // tm,),
        in_specs=[pl.BlockSpec((tm, d), lambda i: (i, 0)), pl.BlockSpec((1, d), lambda i: (0, 0))],
        out_specs=pl.BlockSpec((tm, d), lambda i: (i, 0)),
        out_shape=jax.ShapeDtypeStruct((n, d), BF16),
        compiler_params=_params(1, VMEM_LIMIT),
        name="rmsnorm",
    )(x, g.reshape(1, d))


def _mm_kernel(x_ref, w_ref, o_ref):
    o_ref[...] = _dot(x_ref[...], w_ref[...]).astype(o_ref.dtype)


def matmul(x, w, out_dtype, tm, tn, name):
    m, k = x.shape
    n = w.shape[1]
    tm, tn = min(tm, m), min(tn, n)
    return pl.pallas_call(
        _mm_kernel,
        grid=(m // tm, n // tn),
        in_specs=[pl.BlockSpec((tm, k), lambda i, j: (i, 0)), pl.BlockSpec((k, tn), lambda i, j: (0, j))],
        out_specs=pl.BlockSpec((tm, tn), lambda i, j: (i, j)),
        out_shape=jax.ShapeDtypeStruct((m, n), out_dtype),
        compiler_params=_params(2, VMEM_LIMIT),
        name=name,
    )(x, w)


def _mm_rows_kernel(x_ref, w_ref, o_ref):
    tm = x_ref.shape[0]
    n_slab = w_ref.shape[1] // LANES
    res = _dot(x_ref[...], w_ref[...])
    for j in range(n_slab):
        o_ref[pl.ds(j, tm, stride=n_slab), :] = res[:, j * LANES:(j + 1) * LANES]


def matmul_rows(x, w, tm, name):
    m, k = x.shape
    n = w.shape[1]
    n_slab = n // LANES
    tm = min(tm, m)
    return pl.pallas_call(
        _mm_rows_kernel,
        grid=(m // tm,),
        in_specs=[pl.BlockSpec((tm, k), lambda i: (i, 0)), pl.BlockSpec((k, n), lambda i: (0, 0))],
        out_specs=pl.BlockSpec((tm * n_slab, LANES), lambda i: (i, 0)),
        out_shape=jax.ShapeDtypeStruct((m * n_slab, LANES), F32),
        compiler_params=_params(1, VMEM_LIMIT),
        name=name,
    )(x, w)


def _compress_kernel(pt_ref, *refs, n_pages, page, bb):
    n_sub_one = n_pages * page // CMP_STRIDE
    n_pages = bb * n_pages
    page_refs = refs[:n_pages]
    w1_ref, pos_ref, w2_ref, out_ref, lhs_scr, stage_scr = refs[n_pages:]
    sub_per_page = page // CMP_STRIDE
    n_sub = n_pages * sub_per_page
    for p in range(n_pages):
        for cg in range(4):
            stage = stage_scr.at[(p % 2) * 4 + cg]
            for j in range(sub_per_page):
                stage[j * STAGE_PITCH:j * STAGE_PITCH + CMP_STRIDE, :] = (
                    page_refs[p][pl.ds(4 * CMP_STRIDE * j + cg, CMP_STRIDE, stride=4), :])
            for s in range(CMP_STRIDE):
                lhs_scr[cg, p * sub_per_page:(p + 1) * sub_per_page, s * HEAD_DIM:(s + 1) * HEAD_DIM] = (
                    stage[pl.ds(s, sub_per_page, stride=STAGE_PITCH), :])
    row = lax.broadcasted_iota(jnp.int32, (n_sub, HEAD_DIM), 0)
    for c in range(2):
        w1 = w1_ref[c]
        bias = _dot(pos_ref[c], w1)
        b0 = bias[0:1, :HEAD_DIM]
        b1 = bias[1:2, HEAD_DIM:]
        for g in range(2):
            a = _dot(lhs_scr[c * 2 + g].astype(BF16), w1)
            a0 = a[:, :HEAD_DIM] + b0
            a1 = a[:, HEAD_DIM:] + b1
            hid = a0 + pltpu.roll(a1, shift=n_sub - 1, axis=0)
            act = hid * _sigmoid(hid)
            o = _dot(act.astype(BF16), w2_ref[c])
            o = jnp.where(row % n_sub_one < n_sub_one - 1, o, 0.0)
            for i in range(bb):
                out_ref[i, c, g] = o[i * n_sub_one:(i + 1) * n_sub_one]


def _page_specs(n_pages, rows, bb=1):
    return [pl.BlockSpec((rows, LANES),
                         functools.partial(lambda b, pt, i, p: (pt[(b * bb + i) * n_pages + p], 0), i=i, p=p))
            for i in range(bb) for p in range(n_pages)]


def compress_tokens(pool, page_table, page, w1cat, pos8, w2):
    nb, n_pages = page_table.shape
    n_sub = n_pages * page // CMP_STRIDE
    bb = 2 if nb % 2 == 0 else 1
    const = lambda shape: pl.BlockSpec(shape, lambda b, pt: (0,) * len(shape))
    grid_spec = pltpu.PrefetchScalarGridSpec(
        num_scalar_prefetch=1,
        grid=(nb // bb,),
        in_specs=_page_specs(n_pages, 4 * page, bb) + [const(w1cat.shape), const(pos8.shape), const(w2.shape)],
        out_specs=pl.BlockSpec((bb, 2, 2, n_sub, HEAD_DIM), lambda b, pt: (b, 0, 0, 0, 0)),
        scratch_shapes=[pltpu.VMEM((4, bb * n_sub, CMP_STRIDE * HEAD_DIM), F32),
                        pltpu.VMEM((8, page // CMP_STRIDE * STAGE_PITCH, LANES), F32)],
    )
    return pl.pallas_call(
        functools.partial(_compress_kernel, n_pages=n_pages, page=page, bb=bb),
        grid_spec=grid_spec,
        out_shape=jax.ShapeDtypeStruct((nb, 2, 2, n_sub, HEAD_DIM), F32),
        compiler_params=_params(1, VMEM_LIMIT),
        name="compress",
    )(page_table.reshape(-1), *([pool] * (bb * n_pages)), w1cat, pos8, w2)


def compress_weights(phi_pos, phi_w1, phi_w2):
    n_half = CMP_LEN // CMP_STRIDE
    w1 = phi_w1.reshape(2, n_half, CMP_STRIDE, HEAD_DIM, HEAD_DIM).transpose(0, 2, 3, 1, 4)
    w1cat = w1.reshape(2, CMP_STRIDE * HEAD_DIM, n_half * HEAD_DIM).astype(BF16)
    pos = phi_pos.reshape(2, n_half, CMP_STRIDE * HEAD_DIM)
    pos8 = jnp.pad(pos, ((0, 0), (0, 8 - n_half), (0, 0))).astype(BF16)
    return w1cat, pos8, phi_w2.astype(BF16)


def _softmax_attend(s, v):
    e = jnp.exp2(s - jnp.max(s, axis=-1, keepdims=True))
    d = jnp.sum(e, axis=-1, keepdims=True)
    return _dot(e.astype(BF16), v) / jnp.where(d > 0, d, 1.0)


def _biased_attend(q, k, v, bias, key_slope):
    return _softmax_attend(_dot_nt(q, k) * (SCALE * LOG2E) + key_slope + bias, v)


def _top_rank(score, n_blocks):
    blk = lax.broadcasted_iota(jnp.int32, score.shape, 1)
    rank = jnp.zeros(score.shape, F32)
    for n2 in range(n_blocks):
        col = score[:, n2:n2 + 1]
        rank = rank + jnp.where(col > score, 1.0, 0.0) + jnp.where((col == score) & (blk > n2), 1.0, 0.0)
    return rank


def _alibi_slopes(n_heads):
    return [float(2.0 ** (-8.0 * (h + 1) / n_heads)) for h in range(n_heads)]


def _masked_softmax(s, valid, axis):
    s = jnp.where(valid, s, NEG_INF)
    e = jnp.where(valid, jnp.exp(s - jnp.max(s, axis=axis, keepdims=True)), 0.0)
    d = jnp.sum(e, axis=axis, keepdims=True)
    return e / jnp.where(d > 0, d, 1.0)


def _cmp_branch(q, kc, vc, cover, valid):
    imp = None
    outs = []
    for r in range(NSA_GROUP):
        p = _masked_softmax(_dot_nt(q[:, r * HEAD_DIM:(r + 1) * HEAD_DIM], kc) * SCALE, valid, -1)
        p_hi = p.astype(BF16)
        p_lo = (p - p_hi.astype(F32)).astype(BF16)
        outs.append(_dot(p_hi, vc))
        part = _dot(p_hi, cover) + _dot(p_lo, cover)
        imp = part if imp is None else imp + part
    return outs, imp.T


def _select_blocks_t(imp_t, tpos_row, n_sel):
    blk = lax.broadcasted_iota(jnp.int32, imp_t.shape, 0)
    cur = tpos_row // SEL_BLOCK
    forced = (blk == 0) | (blk == cur) | (blk == cur - 1)
    score = jnp.where(blk * SEL_BLOCK <= tpos_row, imp_t + FORCE_BONUS * jnp.where(forced, 1.0, 0.0), NEG_INF)
    rank = jnp.zeros(score.shape, F32)
    for n2 in range(n_sel):
        row = score[n2:n2 + 1, :]
        rank = rank + jnp.where(row > score, 1.0, 0.0) + jnp.where((row == score) & (blk > n2), 1.0, 0.0)
    return jnp.where(rank < min(SEL_TOP_N, n_sel), 1.0, 0.0)


def _select_blocks(imp, tpos, n_sel):
    blk = lax.broadcasted_iota(jnp.int32, imp.shape, 1)
    cur = tpos // SEL_BLOCK
    forced = (blk == 0) | (blk == cur) | (blk == cur - 1)
    allowed = (blk * SEL_BLOCK <= tpos) & (blk < n_sel)
    score = jnp.where(allowed, imp + FORCE_BONUS * jnp.where(forced, 1.0, 0.0), NEG_INF)
    rank = _top_rank(score, n_sel)
    return jnp.where((rank < min(SEL_TOP_N, n_sel)) & (blk < n_sel), 1.0, 0.0)


def _kv_rows(ref, token0, n_tokens, slab, n_slab):
    return ref[pl.ds(token0 * n_slab + slab, n_tokens, stride=n_slab), :]


def _causal_bias(bias, tq):
    w = bias.shape[1]
    row = lax.broadcasted_iota(jnp.int32, (tq, tq), 0)
    col = lax.broadcasted_iota(jnp.int32, (tq, tq), 1)
    diag = jnp.where(row >= col, bias[:, w - tq:], MASKED)
    return diag if w == tq else jnp.concatenate([bias[:, :w - tq], diag], axis=1)


def _nsa_prompt_kernel(q_ref, gate_ref, kcvc_ref, sel_ref, win_ref, cover_ref, expand_ref, wbias_ref, o_ref,
                       ks_scr, vs_scr, kw_scr, vw_scr, *, tq, seq):
    g = pl.program_id(1)
    qi = pl.program_id(2)
    nq = seq // tq

    @pl.when(qi == 0)
    def _():
        ks_scr[...] = _kv_rows(sel_ref, 0, seq, g, 4).astype(BF16)
        vs_scr[...] = _kv_rows(sel_ref, 0, seq, 2 + g, 4).astype(BF16)
        kw_scr[...] = _kv_rows(win_ref, 0, seq, g, 4).astype(BF16)
        vw_scr[...] = _kv_rows(win_ref, 0, seq, 2 + g, 4).astype(BF16)

    n_sel = seq // SEL_BLOCK
    n_sub = seq // CMP_STRIDE
    n_win = WINDOW // tq
    tpos = qi * tq + lax.broadcasted_iota(jnp.int32, (tq, 1), 0)
    tpos_row = qi * tq + lax.broadcasted_iota(jnp.int32, (1, tq), 1)
    q = q_ref[...]

    n_idx = lax.broadcasted_iota(jnp.int32, (1, n_sub), 1)
    valid = (n_idx * CMP_STRIDE + (CMP_LEN - 1) <= tpos) & (n_idx < n_sub - 1)
    o_cmp, imp_t = _cmp_branch(q, kcvc_ref[0, 0, 0].astype(BF16), kcvc_ref[0, 1, 0].astype(BF16), cover_ref[...],
                               valid)
    sel_t = _select_blocks_t(imp_t[:n_sel], tpos_row, n_sel)
    selected = jnp.concatenate([sel_t, jnp.zeros((LANES - n_sel, tq), F32)], axis=0).T.astype(BF16)

    slopes = _alibi_slopes(NSA_HEADS)
    slope = [jnp.where(g == 0, jnp.float32(slopes[r] * LOG2E), jnp.float32(slopes[NSA_GROUP + r] * LOG2E))
             for r in range(NSA_GROUP)]
    gates = _sigmoid(gate_ref[...])

    for c in range(nq):
        @pl.when(qi == c)
        def _(c=c):
            w = (c + 1) * tq
            bias = _causal_bias(_dot(selected, expand_ref[:, :w]) + MASKED, tq)
            k = ks_scr[:w, :]
            v = vs_scr[:w, :]
            kpos = lax.broadcasted_iota(jnp.int32, (1, w), 1).astype(F32)
            c0 = max(c - n_win, 0)
            ww = (c + 1 - c0) * tq
            kw = kw_scr[c0 * tq:(c + 1) * tq, :]
            vw = vw_scr[c0 * tq:(c + 1) * tq, :]
            wbias = wbias_ref[:, (n_win + 1) * tq - ww:]
            kpos_w = (c0 * tq + lax.broadcasted_iota(jnp.int32, (1, ww), 1)).astype(F32)
            for r in range(NSA_GROUP):
                qr = q[:, r * HEAD_DIM:(r + 1) * HEAD_DIM]
                o_sel = _biased_attend(qr, k, v, bias, slope[r] * kpos)
                o_win = _biased_attend(qr, kw, vw, wbias, slope[r] * kpos_w)
                o_ref[:, r * HEAD_DIM:(r + 1) * HEAD_DIM] = (gates[:, 3 * r:3 * r + 1] * o_cmp[r]
                                                              + gates[:, 3 * r + 1:3 * r + 2] * o_sel
                                                              + gates[:, 3 * r + 2:3 * r + 3] * o_win)


def nsa_prompt_constants(n_sub, n_keys, tq):
    cover, _ = nsa_constants(n_sub, LANES)
    expand = np.where(np.arange(n_keys)[None, :] // SEL_BLOCK == np.arange(LANES)[:, None], -MASKED, 0.0)
    dist = WINDOW + np.arange(tq)[:, None] - np.arange(WINDOW + tq)[None, :]
    wbias = np.where((dist >= 0) & (dist < WINDOW), 0.0, MASKED)
    return cover, jnp.asarray(expand, BF16), jnp.asarray(wbias, F32)


def nsa_constants(n_sub, n_keys):
    c_start = np.arange(n_sub)[:, None] * CMP_STRIDE
    s_start = np.arange(LANES)[None, :] * SEL_BLOCK
    cover = ((c_start < s_start + SEL_BLOCK) & (c_start + CMP_LEN > s_start)).astype(np.float32)
    cover[n_sub - 1:] = 0.0
    expand = (np.arange(n_keys)[None, :] // SEL_BLOCK == np.arange(LANES)[:, None]).astype(np.float32)
    return jnp.asarray(cover, BF16), jnp.asarray(expand, BF16)


def nsa_prompt(q_all, gates, kcvc, sel_kv, win_kv, batch, seq, tq=256):
    assert WINDOW % tq == 0 and seq % tq == 0 and (seq // SEL_BLOCK) % 8 == 0
    nq = seq // tq
    n_sub = seq // CMP_STRIDE
    cover, expand, wbias = nsa_prompt_constants(n_sub, seq, tq)
    gw = NSA_GROUP * HEAD_DIM
    kv_spec = pl.BlockSpec((seq * 4, LANES), lambda b, g, i: (b, 0))
    const = lambda a: pl.BlockSpec(a.shape, lambda b, g, i: (0, 0))
    return pl.pallas_call(
        functools.partial(_nsa_prompt_kernel, tq=tq, seq=seq),
        grid=(batch, NSA_KV_HEADS, nq),
        in_specs=[
            pl.BlockSpec((tq, gw), lambda b, g, i: (b * nq + i, g)),
            pl.BlockSpec((tq, LANES), lambda b, g, i: (b * nq + i, g)),
            pl.BlockSpec((1, 2, 1, n_sub, HEAD_DIM), lambda b, g, i: (b, 0, g, 0, 0)),
            kv_spec, kv_spec, const(cover), const(expand), const(wbias),
        ],
        out_specs=pl.BlockSpec((tq, gw), lambda b, g, i: (b * nq + i, g)),
        out_shape=jax.ShapeDtypeStruct((batch * seq, NSA_HEADS * HEAD_DIM), F32),
        scratch_shapes=[pltpu.VMEM((seq, HEAD_DIM), BF16)] * 4,
        compiler_params=pltpu.CompilerParams(dimension_semantics=("parallel", "parallel", "arbitrary"),
                                             vmem_limit_bytes=VMEM_LIMIT),
        name="nsa_prompt",
    )(q_all, gates, kcvc, sel_kv, win_kv, cover, expand, wbias)


def _moba_select(gate, n_past, n_blocks):
    blk = lax.broadcasted_iota(jnp.int32, gate.shape, 1)
    past = blk < n_past
    score = jnp.where(past, gate, NEG_INF)
    rank = _top_rank(score, n_blocks)
    return jnp.where(past & (rank < min(MOBA_TOP_K, n_blocks)), 1.0, 0.0)


def _moba_select_t(gate_t, n_past, n_blocks):
    blk = lax.broadcasted_iota(jnp.int32, gate_t.shape, 0)
    past = blk < n_past
    score = jnp.where(past, gate_t, NEG_INF)
    rank = jnp.zeros(score.shape, F32)
    for n2 in range(n_blocks):
        row = score[n2:n2 + 1, :]
        rank = rank + jnp.where(row > score, 1.0, 0.0) + jnp.where((row == score) & (blk > n2), 1.0, 0.0)
    return jnp.where(past & (rank < min(MOBA_TOP_K, n_blocks)), 1.0, 0.0)


def _moba_prompt_kernel(q_ref, kv_ref, o_ref, kmean_scr, k_scr, v_scr, *, seq):
    tq = MOBA_BLOCK
    n_slab = 2 * MOBA_HEADS
    qi = pl.program_id(1)
    nb = seq // tq
    slopes = [s * LOG2E for s in _alibi_slopes(MOBA_HEADS)]

    @pl.when(qi == 0)
    def _():
        kmean_scr[...] = jnp.zeros(kmean_scr.shape, F32)
        for h in range(MOBA_HEADS):
            k = _kv_rows(kv_ref, 0, seq, h, n_slab)
            k_scr[h] = k.astype(BF16)
            v_scr[h] = _kv_rows(kv_ref, 0, seq, MOBA_HEADS + h, n_slab).astype(BF16)
            for n in range(nb):
                kmean_scr[h, n:n + 1, :] = jnp.mean(k[n * tq:(n + 1) * tq], axis=0, keepdims=True)

    row = lax.broadcasted_iota(jnp.int32, (tq, tq), 0)
    col = lax.broadcasted_iota(jnp.int32, (tq, tq), 1)
    own = jnp.where(row >= col, 0.0, MASKED)
    q, sel_bias = [], []
    nb8 = -(-nb // 8) * 8
    for h in range(MOBA_HEADS):
        q.append(q_ref[:, h * HEAD_DIM:(h + 1) * HEAD_DIM])
        gate_t = _dot_nt(kmean_scr[h].astype(BF16), q[h])[:nb8]
        bias_t = (1.0 - _moba_select_t(gate_t, qi, nb - 1)) * MASKED
        sel_bias.append(jnp.concatenate([bias_t, jnp.zeros((LANES - nb8, tq), F32)], axis=0).T)

    for c in range(nb):
        @pl.when(qi == c)
        def _(c=c):
            w = (c + 1) * tq
            kpos = lax.broadcasted_iota(jnp.int32, (1, w), 1).astype(F32)
            for h in range(MOBA_HEADS):
                bias = jnp.concatenate(
                    [jnp.broadcast_to(sel_bias[h][:, n:n + 1], (tq, tq)) for n in range(c)] + [own], axis=1)
                o_ref[:, h * HEAD_DIM:(h + 1) * HEAD_DIM] = _biased_attend(q[h], k_scr[h, :w, :], v_scr[h, :w, :], bias,
                                                                            slopes[h] * kpos)


def moba_prompt(q_all, q_col, moba_kv, batch, seq):
    tq = MOBA_BLOCK
    nq = seq // tq
    width = MOBA_HEADS * HEAD_DIM
    return pl.pallas_call(
        functools.partial(_moba_prompt_kernel, seq=seq),
        grid=(batch, nq),
        in_specs=[
            pl.BlockSpec((tq, width), lambda b, i: (b * nq + i, q_col)),
            pl.BlockSpec((seq * 2 * MOBA_HEADS, LANES), lambda b, i: (b, 0)),
        ],
        out_specs=pl.BlockSpec((tq, width), lambda b, i: (b * nq + i, 0)),
        out_shape=jax.ShapeDtypeStruct((batch * seq, width), F32),
        scratch_shapes=[pltpu.VMEM((MOBA_HEADS, LANES, HEAD_DIM), F32), pltpu.VMEM((MOBA_HEADS, seq, HEAD_DIM), BF16),
                        pltpu.VMEM((MOBA_HEADS, seq, HEAD_DIM), BF16)],
        compiler_params=pltpu.CompilerParams(dimension_semantics=("parallel", "arbitrary"),
                                             vmem_limit_bytes=VMEM_LIMIT),
        name="moba_prompt",
    )(q_all, moba_kv)


def _padded_tile(rows, n):
    return jnp.concatenate([rows, jnp.zeros((n - rows.shape[0], rows.shape[1]), F32)], axis=0).astype(BF16)


def _nsa_sample_kernel(pt_ref, *refs, n_pages, page, t_new, n_win, bb):
    page_refs = refs[:bb * n_pages]
    (q_ref, gate_ref, kcvc_ref, new_sel_ref, cwin_ref, new_win_ref, cover_ref, expand_ref, sbias_ref, wbias_ref,
     o_ref, win_out_ref, ks_scr, vs_scr, kw_scr, vw_scr) = refs[bb * n_pages:]
    n_slab = 2 * NSA_KV_HEADS
    past = n_pages * page
    n_sub = past // CMP_STRIDE
    n_sel = -(-(past + t_new) // SEL_BLOCK)
    rows = NSA_GROUP * t_new
    keep = win_out_ref.shape[0] // (bb * n_slab)
    chains = [(i, g) for i in range(bb) for g in range(NSA_KV_HEADS)]

    for i in range(bb):
        out0, kept = i * keep * n_slab, (keep - t_new) * n_slab
        win_out_ref[out0:out0 + kept, :] = cwin_ref[(i + 1) * n_win * n_slab - kept:(i + 1) * n_win * n_slab, :]
        win_out_ref[out0 + kept:out0 + keep * n_slab, :] = new_win_ref[i * t_new * n_slab:(i + 1) * t_new * n_slab, :]

    for c, (i, g) in enumerate(chains):
        for p in range(n_pages):
            ks_scr[c, p * page:(p + 1) * page, :] = _kv_rows(page_refs[i * n_pages + p], 0, page, g, n_slab).astype(BF16)
            vs_scr[c, p * page:(p + 1) * page, :] = (
                _kv_rows(page_refs[i * n_pages + p], 0, page, 2 + g, n_slab).astype(BF16))
        ks_scr[c, past:, :] = _padded_tile(_kv_rows(new_sel_ref, i * t_new, t_new, g, n_slab), LANES)
        vs_scr[c, past:, :] = _padded_tile(_kv_rows(new_sel_ref, i * t_new, t_new, 2 + g, n_slab), LANES)
        kw_scr[c, :n_win, :] = _kv_rows(cwin_ref, i * n_win, n_win, g, n_slab).astype(BF16)
        vw_scr[c, :n_win, :] = _kv_rows(cwin_ref, i * n_win, n_win, 2 + g, n_slab).astype(BF16)
        kw_scr[c, n_win:, :] = _padded_tile(_kv_rows(new_win_ref, i * t_new, t_new, g, n_slab), LANES)
        vw_scr[c, n_win:, :] = _padded_tile(_kv_rows(new_win_ref, i * t_new, t_new, 2 + g, n_slab), LANES)

    q_all = [_stack_heads(q_ref, i * t_new, t_new, NSA_HEADS) for i in range(bb)]
    q = [q_all[i][g * rows:(g + 1) * rows] for i, g in chains]
    tpos = past + lax.broadcasted_iota(jnp.int32, (rows, 1), 0) % t_new
    tpos_q = past + lax.broadcasted_iota(jnp.int32, (t_new, 1), 0)
    n_idx = lax.broadcasted_iota(jnp.int32, (1, n_sub), 1)
    valid = (n_idx * CMP_STRIDE + (CMP_LEN - 1) <= tpos) & (n_idx < n_sub - 1)
    cmp_out = [_cmp_branch_rows(q[c], kcvc_ref[i, 0, g].astype(BF16), kcvc_ref[i, 1, g].astype(BF16),
                                cover_ref[...], valid) for c, (i, g) in enumerate(chains)]
    selected = []
    for c in range(len(chains)):
        imp = cmp_out[c][1]
        imp_t = imp[0:t_new]
        for r in range(1, NSA_GROUP):
            imp_t = imp_t + imp[r * t_new:(r + 1) * t_new]
        selected.append(jnp.concatenate([_select_blocks(imp_t, tpos_q, n_sel)] * NSA_GROUP, axis=0).astype(BF16))
    o_sel, o_win = [], []
    for c, (i, g) in enumerate(chains):
        bias = (_dot(selected[c], expand_ref[...]) + MASKED) + sbias_ref[g]
        o_sel.append(_softmax_attend(_dot_nt(q[c], ks_scr[c]) * (SCALE * LOG2E) + bias, vs_scr[c]))
        o_win.append(_softmax_attend(_dot_nt(q[c], kw_scr[c]) * (SCALE * LOG2E) + wbias_ref[g], vw_scr[c]))

    gates = _sigmoid(gate_ref[...])
    for c, (i, g) in enumerate(chains):
        out_rows = slice(i * t_new, (i + 1) * t_new)
        for r in range(NSA_GROUP):
            h = g * NSA_GROUP + r
            lane = g * LANES + 3 * r
            rs = slice(r * t_new, (r + 1) * t_new)
            o_ref[out_rows, h * HEAD_DIM:(h + 1) * HEAD_DIM] = (
                gates[out_rows, lane:lane + 1] * cmp_out[c][0][rs]
                + gates[out_rows, lane + 1:lane + 2] * o_sel[c][rs]
                + gates[out_rows, lane + 2:lane + 3] * o_win[c][rs])


def _cmp_branch_rows(q, kc, vc, cover, valid):
    s = jnp.where(valid, _dot_nt(q, kc) * SCALE, NEG_INF)
    e = jnp.where(valid, jnp.exp(s - jnp.max(s, axis=-1, keepdims=True)), 0.0)
    d = jnp.sum(e, axis=-1, keepdims=True)
    p = e / jnp.where(d > 0, d, 1.0)
    p_hi = p.astype(BF16)
    p_lo = (p - p_hi.astype(F32)).astype(BF16)
    return _dot(p_hi, vc), _dot(p_hi, cover) + _dot(p_lo, cover)


def nsa_sample(q_all, gates, kcvc, sel_pool, page_table, page, new_sel, cache_win, new_win, t_new):
    nb, n_pages = page_table.shape
    past = n_pages * page
    n_sub = past // CMP_STRIDE
    n_slab = 2 * NSA_KV_HEADS
    n_win = cache_win.shape[0] // (n_slab * nb)
    assert past % SEL_BLOCK == 0 and t_new <= SEL_BLOCK and past // SEL_BLOCK + 2 <= LANES
    cover, expand = nsa_constants(n_sub, past + LANES)
    expand = expand * jnp.asarray(-MASKED, BF16)
    slopes = _alibi_slopes(NSA_HEADS)
    t_pos = past + np.arange(t_new)

    def alibi_bias(pos, ok):
        per_head = [np.where(ok, s * LOG2E * pos[None, :], MASKED) for s in slopes]
        return jnp.asarray(np.stack(per_head).reshape(NSA_KV_HEADS, NSA_GROUP * t_new, pos.shape[0]), F32)

    sel_pos = np.arange(past + LANES)
    sbias = alibi_bias(sel_pos, (sel_pos[None, :] <= t_pos[:, None]) & (sel_pos[None, :] < past + t_new))
    win_pos = past - n_win + np.arange(n_win + LANES)
    dist = t_pos[:, None] - win_pos[None, :]
    wbias = alibi_bias(win_pos, (dist >= 0) & (dist < WINDOW) & (win_pos[None, :] >= 0)
                       & (win_pos[None, :] < past + t_new))
    keep = min(WINDOW, n_win + t_new)
    assert keep >= t_new and (n_slab * (keep - t_new)) % 8 == 0
    bb = 2 if nb % 2 == 0 else 1
    n_chain = bb * NSA_KV_HEADS
    width = NSA_HEADS * HEAD_DIM
    const = lambda a: pl.BlockSpec(a.shape, lambda b, pt: (0,) * a.ndim)
    rows = lambda n: pl.BlockSpec((bb * n_slab * n, LANES), lambda b, pt: (b, 0))
    grid_spec = pltpu.PrefetchScalarGridSpec(
        num_scalar_prefetch=1,
        grid=(nb // bb,),
        in_specs=_page_specs(n_pages, n_slab * page, bb) + [
            pl.BlockSpec((bb * t_new, width), lambda b, pt: (b, 0)),
            pl.BlockSpec((bb * t_new, NSA_KV_HEADS * LANES), lambda b, pt: (b, 0)),
            pl.BlockSpec((bb, 2, 2, n_sub, HEAD_DIM), lambda b, pt: (b, 0, 0, 0, 0)),
            rows(t_new), rows(n_win), rows(t_new),
            const(cover), const(expand), const(sbias), const(wbias)],
        out_specs=[pl.BlockSpec((bb * t_new, width), lambda b, pt: (b, 0)), rows(keep)],
        scratch_shapes=[pltpu.VMEM((n_chain, past + LANES, HEAD_DIM), BF16),
                        pltpu.VMEM((n_chain, past + LANES, HEAD_DIM), BF16),
                        pltpu.VMEM((n_chain, n_win + LANES, HEAD_DIM), BF16),
                        pltpu.VMEM((n_chain, n_win + LANES, HEAD_DIM), BF16)],
    )
    return pl.pallas_call(
        functools.partial(_nsa_sample_kernel, n_pages=n_pages, page=page, t_new=t_new, n_win=n_win, bb=bb),
        grid_spec=grid_spec,
        out_shape=[jax.ShapeDtypeStruct((nb * t_new, width), F32),
                   jax.ShapeDtypeStruct((nb * keep * n_slab, LANES), F32)],
        compiler_params=_params(1, VMEM_LIMIT),
        name="nsa_sample",
    )(page_table.reshape(-1), *([sel_pool] * (bb * n_pages)), q_all, gates, kcvc, new_sel, cache_win, new_win, cover,
      expand, sbias, wbias)


def _slab_attend(q, kv, bias, shift):
    s = _dot_nt(q, kv) * (SCALE * LOG2E) + bias
    e = jnp.exp2(s - jnp.max(s, axis=-1, keepdims=True))
    d = jnp.sum(e, axis=-1, keepdims=True)
    return _dot(pltpu.roll(e, shift, axis=1).astype(BF16), kv) / jnp.where(d > 0, d, 1.0)


def _stack_heads(ref, row0, n_rows, n_heads):
    return jnp.concatenate([ref[row0:row0 + n_rows, h * HEAD_DIM:(h + 1) * HEAD_DIM] for h in range(n_heads)],
                           axis=0).astype(BF16)


def _slab_bias(head_slab, slopes, key_pos, key_ok, n_slab, t_new):
    n_heads, n_keys = len(head_slab), key_pos.shape[0]
    slab = np.arange(n_keys * n_slab) % n_slab
    out = np.full((n_heads, t_new, n_keys * n_slab), MASKED, np.float64)
    for h in range(n_heads):
        ok = np.repeat(key_ok, n_slab, axis=1) & (slab == head_slab[h])[None, :]
        out[h] = np.where(ok, slopes[h] * LOG2E * np.repeat(key_pos, n_slab)[None, :], MASKED)
    return jnp.asarray(out.reshape(n_heads * t_new, n_keys * n_slab), F32)


def _moba_sample_kernel(pt_ref, *refs, n_pages, page, t_new):
    page_refs = refs[:n_pages]
    q_ref, new_ref, bias_ref, o_ref, kv_scr, mean_scr = refs[n_pages:]
    n_slab = 2 * MOBA_HEADS
    rows_pp = page * n_slab
    past = n_pages * page
    pages_per_block = MOBA_BLOCK // page
    n_past = n_pages // pages_per_block
    block_w = MOBA_BLOCK * n_slab

    mean_scr[...] = jnp.zeros(mean_scr.shape, F32)
    for n in range(n_past):
        total = None
        for p in range(n * pages_per_block, (n + 1) * pages_per_block):
            x = page_refs[p][...]
            kv_scr[p * rows_pp:(p + 1) * rows_pp, :] = x.astype(BF16)
            part = jnp.sum(x.reshape(page, n_slab, LANES), axis=0)
            total = part if total is None else total + part
        mean_scr[n * n_slab:(n + 1) * n_slab, :] = total * (1.0 / MOBA_BLOCK)
    x_new = new_ref[...]
    kv_scr[past * n_slab:, :] = _padded_tile(x_new, LANES)
    mean_scr[n_past * n_slab:(n_past + 1) * n_slab, :] = (
        jnp.sum(x_new.reshape(t_new, n_slab, LANES), axis=0) * (1.0 / MOBA_BLOCK))

    tpos = past + lax.broadcasted_iota(jnp.int32, (t_new, 1), 0)
    cur = tpos // MOBA_BLOCK
    blk = lax.broadcasted_iota(jnp.int32, (t_new, LANES), 1)
    sel_bias = []
    for h in range(MOBA_HEADS):
        q = q_ref[:, h * HEAD_DIM:(h + 1) * HEAD_DIM].astype(BF16)
        kmean = mean_scr[pl.ds(h, LANES // n_slab, stride=n_slab), :]
        kmean = jnp.concatenate([kmean, jnp.zeros((LANES - kmean.shape[0], LANES), F32)], axis=0)
        gate = _dot_nt(q, kmean.astype(BF16))
        selected = jnp.where(blk == cur, 1.0, _moba_select(gate, cur, n_past))
        sel_bias.append((1.0 - selected) * MASKED)
    sel_bias = jnp.concatenate(sel_bias, axis=0)

    bias = jnp.concatenate(
        [bias_ref[:, n * block_w:(n + 1) * block_w] + sel_bias[:, n:n + 1] for n in range(n_past)]
        + [bias_ref[:, n_past * block_w:] + sel_bias[:, n_past:n_past + 1]], axis=1)
    o = _slab_attend(_stack_heads(q_ref, 0, t_new, MOBA_HEADS), kv_scr[...], bias, MOBA_HEADS)
    for h in range(MOBA_HEADS):
        o_ref[:, h * HEAD_DIM:(h + 1) * HEAD_DIM] = o[h * t_new:(h + 1) * t_new]


def moba_sample(q_all, q_col, moba_pool, page_table, page, new_kv, t_new):
    nb, n_pages = page_table.shape
    past = n_pages * page
    width = MOBA_HEADS * HEAD_DIM
    n_slab = 2 * MOBA_HEADS
    assert past % MOBA_BLOCK == 0 and t_new * n_slab <= LANES and past // MOBA_BLOCK < LANES // n_slab
    new_keys = LANES // n_slab
    key_pos = np.arange(past + new_keys)
    t_pos = past + np.arange(t_new)
    key_ok = (key_pos[None, :] <= t_pos[:, None]) & (key_pos[None, :] < past + t_new)
    bias = _slab_bias(list(range(MOBA_HEADS)), _alibi_slopes(MOBA_HEADS), key_pos, key_ok, n_slab, t_new)
    grid_spec = pltpu.PrefetchScalarGridSpec(
        num_scalar_prefetch=1,
        grid=(nb,),
        in_specs=_page_specs(n_pages, n_slab * page) + [
            pl.BlockSpec((t_new, width), lambda b, pt: (b, q_col)),
            pl.BlockSpec((t_new * n_slab, LANES), lambda b, pt: (b, 0)),
            pl.BlockSpec(bias.shape, lambda b, pt: (0, 0))],
        out_specs=pl.BlockSpec((t_new, width), lambda b, pt: (b, 0)),
        scratch_shapes=[pltpu.VMEM((past * n_slab + LANES, HEAD_DIM), BF16), pltpu.VMEM((LANES, LANES), F32)],
    )
    return pl.pallas_call(
        functools.partial(_moba_sample_kernel, n_pages=n_pages, page=page, t_new=t_new),
        grid_spec=grid_spec,
        out_shape=jax.ShapeDtypeStruct((nb * t_new, width), F32),
        compiler_params=_params(1, VMEM_LIMIT),
        name="moba_sample",
    )(page_table.reshape(-1), *([moba_pool] * n_pages), q_all, new_kv, bias)


def _cross_kernel(q_ref, kv_ref, o_ref, *, n_mem):
    n_slab = 2 * MEM_HEADS
    for h in range(MEM_HEADS):
        k = _kv_rows(kv_ref, 0, n_mem, h, n_slab).astype(BF16)
        v = _kv_rows(kv_ref, 0, n_mem, MEM_HEADS + h, n_slab).astype(BF16)
        s = _dot_nt(q_ref[:, h * HEAD_DIM:(h + 1) * HEAD_DIM], k) * SCALE
        e = jnp.exp(s - jnp.max(s, axis=-1, keepdims=True))
        p = e / jnp.sum(e, axis=-1, keepdims=True)
        o_ref[:, h * HEAD_DIM:(h + 1) * HEAD_DIM] = _dot(p.astype(BF16), v)


def cross_attend(q_all, q_col, mem_kv, batch, rows, tq):
    nq = rows // tq
    n_slab = 2 * MEM_HEADS
    width = MEM_HEADS * HEAD_DIM
    n_mem = mem_kv.shape[0] // (batch * n_slab)
    return pl.pallas_call(
        functools.partial(_cross_kernel, n_mem=n_mem),
        grid=(batch, nq),
        in_specs=[
            pl.BlockSpec((tq, width), lambda b, i: (b * nq + i, q_col)),
            pl.BlockSpec((n_mem * n_slab, LANES), lambda b, i: (b, 0)),
        ],
        out_specs=pl.BlockSpec((tq, width), lambda b, i: (b * nq + i, 0)),
        out_shape=jax.ShapeDtypeStruct((batch * rows, width), F32),
        compiler_params=_params(2, VMEM_LIMIT),
        name="cross_attend",
    )(q_all, mem_kv)


def _cross_sample_kernel(q_ref, kv_ref, bias_ref, o_ref, *, n_mem, t_new, bb):
    n_slab = 2 * MEM_HEADS
    rows = n_mem * n_slab
    for i in range(bb):
        q = _stack_heads(q_ref, i * t_new, t_new, MEM_HEADS)
        o = _slab_attend(q, kv_ref[i * rows:(i + 1) * rows, :].astype(BF16), bias_ref[...], MEM_HEADS)
        for h in range(MEM_HEADS):
            o_ref[i * t_new:(i + 1) * t_new, h * HEAD_DIM:(h + 1) * HEAD_DIM] = o[h * t_new:(h + 1) * t_new]


def cross_attend_sample(q_all, q_col, mem_kv, batch, t_new, bb):
    assert batch % bb == 0
    n_slab = 2 * MEM_HEADS
    width = MEM_HEADS * HEAD_DIM
    n_mem = mem_kv.shape[0] // (batch * n_slab)
    bias = _slab_bias(list(range(MEM_HEADS)), [0.0] * MEM_HEADS, np.arange(n_mem), np.ones((t_new, n_mem), bool),
                      n_slab, t_new)
    return pl.pallas_call(
        functools.partial(_cross_sample_kernel, n_mem=n_mem, t_new=t_new, bb=bb),
        grid=(batch // bb,),
        in_specs=[
            pl.BlockSpec((bb * t_new, width), lambda b: (b, q_col)),
            pl.BlockSpec((bb * n_mem * n_slab, LANES), lambda b: (b, 0)),
            pl.BlockSpec(bias.shape, lambda b: (0, 0)),
        ],
        out_specs=pl.BlockSpec((bb * t_new, width), lambda b: (b, 0)),
        out_shape=jax.ShapeDtypeStruct((batch * t_new, width), F32),
        compiler_params=_params(1, VMEM_LIMIT),
        name="cross_sample",
    )(q_all, mem_kv, bias)


def _merge_kernel(x_ref, on_ref, om_ref, oc_ref, z_ref, gl_ref, wn_ref, wm_ref, wc_ref, wo_ref, fg_ref, y_ref, mix_scr,
                  *, tn):
    d = x_ref.shape[1]
    wn_k, wm_k = wn_ref.shape[0], wm_ref.shape[0]
    z = z_ref[...]
    act = z * _sigmoid(z)
    un = (on_ref[...] * act[:, :wn_k]).astype(BF16)
    um = (om_ref[...] * act[:, wn_k:wn_k + wm_k]).astype(BF16)
    uc = (oc_ref[...] * act[:, wn_k + wm_k:]).astype(BF16)
    for c in range(d // tn):
        cols = slice(c * tn, (c + 1) * tn)
        mix = (_sigmoid(gl_ref[:, c * tn:(c + 1) * tn]) * _dot(un, wn_ref[:, cols])
               + _sigmoid(gl_ref[:, d + c * tn:d + (c + 1) * tn]) * _dot(um, wm_ref[:, cols])
               + _sigmoid(gl_ref[:, 2 * d + c * tn:2 * d + (c + 1) * tn]) * _dot(uc, wc_ref[:, cols]))
        mix_scr[:, cols] = mix.astype(BF16)
    y = x_ref[...] + _dot(mix_scr[...], wo_ref[...])
    inv = lax.rsqrt(jnp.mean(y * y, axis=-1, keepdims=True) + RMS_EPS)
    y_ref[...] = y * inv * fg_ref[...]


def merge_branches(x, o_nsa, o_moba, o_mem, z_all, gate_logits, w_nsa, w_moba, w_mem, w_out, final_g, tm=256, tn=512):
    m, d = x.shape
    tm = min(tm, m)
    row = lambda a: pl.BlockSpec((tm, a.shape[1]), lambda i: (i, 0))
    whole = lambda a: pl.BlockSpec(a.shape, lambda i: (0, 0), pipeline_mode=pl.Buffered(1))
    final_g = final_g.reshape(1, d)
    return pl.pallas_call(
        functools.partial(_merge_kernel, tn=tn),
        grid=(m // tm,),
        in_specs=[row(x), row(o_nsa), row(o_moba), row(o_mem), row(z_all), row(gate_logits),
                  whole(w_nsa), whole(w_moba), whole(w_mem), whole(w_out), whole(final_g)],
        out_specs=row(x),
        out_shape=jax.ShapeDtypeStruct((m, d), F32),
        scratch_shapes=[pltpu.VMEM((tm, d), BF16)],
        compiler_params=_params(1, VMEM_LIMIT),
        name="merge",
    )(x, o_nsa, o_moba, o_mem, z_all, gate_logits, w_nsa, w_moba, w_mem, w_out, final_g)


def _in_weights(w_in, d_model):
    splits = (NSA_HEADS * HEAD_DIM, 4 * HEAD_DIM, 4 * HEAD_DIM, 4 * HEAD_DIM, 3 * NSA_HEADS, NSA_HEADS * HEAD_DIM,
              MOBA_HEADS * HEAD_DIM, 2 * MOBA_HEADS * HEAD_DIM, MOBA_HEADS * HEAD_DIM, MEM_HEADS * HEAD_DIM,
              MEM_HEADS * HEAD_DIM, N_BRANCH * d_model)
    offs = np.cumsum((0,) + splits)
    assert offs[-1] == w_in.shape[1]
    (nsa_q, cmp_kv, sel_kv, win_kv, nsa_g, nsa_z, moba_q, moba_kv, moba_z, mem_q, mem_z, merge_g) = (
        w_in[:, offs[i]:offs[i + 1]] for i in range(len(splits)))
    per_group = 3 * NSA_GROUP
    gate_w = jnp.pad(nsa_g.reshape(-1, NSA_KV_HEADS, per_group), ((0, 0), (0, 0), (0, LANES - per_group)))
    cast = lambda w: w.astype(BF16)
    return dict(q=cast(jnp.concatenate([nsa_q, moba_q, mem_q], axis=1)), cmp=cast(cmp_kv), sel=cast(sel_kv),
                win=cast(win_kv), moba=cast(moba_kv), gate=cast(gate_w.reshape(-1, NSA_KV_HEADS * LANES)),
                z=cast(jnp.concatenate([nsa_z, moba_z, mem_z], axis=1)), merge=cast(merge_g))


def _in_projection(x, norm_g, w, q_dtype):
    tm = min(1024, x.shape[0] // 2)
    xn = rmsnorm_bf16(x, norm_g, 512)
    mm = lambda name, dtype: matmul(xn, w[name], dtype, 2 * tm, 1024, "in_" + name)
    rows = lambda name: matmul_rows(xn, w[name], tm, "in_" + name)
    return dict(q=mm("q", q_dtype), cmp=rows("cmp"), sel=rows("sel"), win=rows("win"), moba=rows("moba"),
                gate=mm("gate", F32), z=mm("z", F32), merge=mm("merge", F32))


def kernel(x_prompt, x_sample, cache_cmp_kv, cache_sel_kv, cache_moba_kv, cache_win_kv, cache_mem_kv, page_table,
           mem_prompt, norm_g, w_in, phi_pos, phi_w1, phi_w2, mem_norm_g, w_mem_kv, w_br_nsa, w_br_moba, w_br_mem,
           w_out, final_norm_g):
    batch, seq, d_model = x_prompt.shape
    nb, t_new, _ = x_sample.shape
    n_mem = mem_prompt.shape[1]
    page = cache_cmp_kv.shape[2]
    n_win = cache_win_kv.shape[2]
    assert norm_g.shape[0] == 1, "one layer"
    assert seq % MOBA_BLOCK == 0 and (page_table.shape[1] * page) % MOBA_BLOCK == 0 and MOBA_BLOCK % page == 0
    l = 0
    q_moba_col = NSA_HEADS * HEAD_DIM // (MOBA_HEADS * HEAD_DIM)
    q_mem_col = (NSA_HEADS + MOBA_HEADS) * HEAD_DIM // (MEM_HEADS * HEAD_DIM)
    lane_rows = lambda a: a.reshape(-1, LANES)

    w = _in_weights(w_in[l], d_model)
    w1cat, pos8, w2 = compress_weights(phi_pos[l], phi_w1[l], phi_w2[l])
    br = [a[l].astype(BF16) for a in (w_br_nsa, w_br_moba, w_br_mem, w_out)]

    xp = x_prompt.reshape(batch * seq, d_model)
    hp = _in_projection(xp, norm_g[l], w, BF16)
    mem_n = rmsnorm_bf16(mem_prompt.reshape(batch * n_mem, d_model), mem_norm_g[l], 512)
    mem_kv_p = matmul_rows(mem_n, w_mem_kv[l].astype(BF16), 1024, "mem_kv")
    pages_p = seq // page
    identity_table = jnp.arange(batch * pages_p, dtype=jnp.int32).reshape(batch, pages_p)
    kcvc_p = compress_tokens(hp["cmp"], identity_table, page, w1cat, pos8, w2)
    o_nsa_p = nsa_prompt(hp["q"], hp["gate"], kcvc_p, hp["sel"], hp["win"], batch, seq)
    o_moba_p = moba_prompt(hp["q"], q_moba_col, hp["moba"], batch, seq)
    o_mem_p = cross_attend(hp["q"], q_mem_col, mem_kv_p, batch, seq, min(seq, 1024))
    y_prompt = merge_branches(xp, o_nsa_p, o_moba_p, o_mem_p, hp["z"], hp["merge"], *br, final_norm_g)

    xs = x_sample.reshape(nb * t_new, d_model)
    hs = _in_projection(xs, norm_g[l], w, F32)
    kcvc_s = compress_tokens(lane_rows(cache_cmp_kv[l]), page_table, page, w1cat, pos8, w2)
    o_nsa_s, win_s = nsa_sample(hs["q"], hs["gate"], kcvc_s, lane_rows(cache_sel_kv[l]), page_table, page, hs["sel"],
                                lane_rows(cache_win_kv[l]), hs["win"], t_new)
    o_moba_s = moba_sample(hs["q"], q_moba_col, lane_rows(cache_moba_kv[l]), page_table, page, hs["moba"], t_new)
    o_mem_s = cross_attend_sample(hs["q"], q_mem_col, lane_rows(cache_mem_kv[l]), nb, t_new, 4 if nb % 4 == 0 else 1)
    y_sample = merge_branches(xs, o_nsa_s, o_moba_s, o_mem_s, hs["z"], hs["merge"], *br, final_norm_g)

    nsa_shape = lambda a, b_, t_: a.reshape(1, b_, t_, 2, NSA_KV_HEADS, HEAD_DIM)
    moba_shape = lambda a, b_, t_: a.reshape(1, b_, t_, 2, MOBA_HEADS, HEAD_DIM)
    keep_p = min(WINDOW, seq)
    keep_s = min(WINDOW, n_win + t_new)
    return (y_prompt.reshape(batch, seq, d_model), y_sample.reshape(nb, t_new, d_model),
            nsa_shape(hp["cmp"], batch, seq), nsa_shape(hp["sel"], batch, seq), moba_shape(hp["moba"], batch, seq),
            nsa_shape(hp["win"], batch, seq)[:, :, seq - keep_p:],
            mem_kv_p.reshape(1, batch, n_mem, 2, MEM_HEADS, HEAD_DIM),
            nsa_shape(hs["cmp"], nb, t_new), nsa_shape(hs["sel"], nb, t_new), moba_shape(hs["moba"], nb, t_new),
            nsa_shape(win_s, nb, keep_s))
```

```python
import functools

import numpy as np
import jax
import jax.numpy as jnp
from jax import lax
from jax.experimental import pallas as pl
from jax.experimental.pallas import tpu as pltpu

HEAD_DIM = 128
NSA_HEADS = 8
NSA_KV_HEADS = 2
NSA_GROUP = NSA_HEADS // NSA_KV_HEADS
CMP_LEN = 32
CMP_STRIDE = 16
SEL_BLOCK = 64
SEL_TOP_N = 16
WINDOW = 512
FORCE_BONUS = 1.0e4
MOBA_HEADS = 4
MOBA_BLOCK = 256
MOBA_TOP_K = 3
MEM_HEADS = 4
N_BRANCH = 3
RMS_EPS = 1e-6
NEG_INF = -1e30
SCALE = HEAD_DIM ** -0.5
LOG2E = 1.4426950408889634
MASKED = -(2.0 ** 100)

LANES = 128
STAGE_PITCH = 24
VMEM_LIMIT = 56 * 1024 * 1024

NORM_ROWS = 512
PROJ_ROWS, PROJ_COLS = 2048, 1024
KV_ROWS = 1024
NSA_Q_ROWS = 256
CROSS_Q_ROWS = 1024
MERGE_ROWS, MERGE_COLS = 256, 512
SAMPLE_STEP_BATCH = 2
CROSS_STEP_BATCH = 4

F32 = jnp.float32
BF16 = jnp.bfloat16


def _params(n_axes, vmem=None):
    return pltpu.CompilerParams(dimension_semantics=("parallel",) * n_axes, vmem_limit_bytes=vmem)


def _dot(a, b):
    return jnp.dot(a, b, preferred_element_type=F32)


def _dot_nt(a, b):
    return lax.dot_general(a, b, (((1,), (1,)), ((), ())), preferred_element_type=F32)


def _sigmoid(x):
    return 1.0 / (1.0 + jnp.exp(-x))


def _rms_kernel(x_ref, g_ref, o_ref):
    x = x_ref[...]
    inv = lax.rsqrt(jnp.mean(x * x, axis=-1, keepdims=True) + RMS_EPS)
    o_ref[...] = (x * inv * g_ref[...]).astype(o_ref.dtype)


def rmsnorm_bf16(x, g, tm):
    n, d = x.shape
    return pl.pallas_call(
        _rms_kernel,
        grid=(n // tm,),
        in_specs=[pl.BlockSpec((tm, d), lambda i: (i, 0)), pl.BlockSpec((1, d), lambda i: (0, 0))],
        out_specs=pl.BlockSpec((tm, d), lambda i: (i, 0)),
        out_shape=jax.ShapeDtypeStruct((n, d), BF16),
        compiler_params=_params(1, VMEM_LIMIT),
        name="rmsnorm",
    )(x, g.reshape(1, d))


def _mm_kernel(x_ref, w_ref, o_ref):
    o_ref[...] = _dot(x_ref[...], w_ref[...]).astype(o_ref.dtype)


def matmul(x, w, out_dtype, tm, tn, name):
    m, k = x.shape
    n = w.shape[1]
    tm, tn = min(tm, m), min(tn, n)
    return pl.pallas_call(
        _mm_kernel,
        grid=(m // tm, n // tn),
        in_specs=[pl.BlockSpec((tm, k), lambda i, j: (i, 0)), pl.BlockSpec((k, tn), lambda i, j: (0, j))],
        out_specs=pl.BlockSpec((tm, tn), lambda i, j: (i, j)),
        out_shape=jax.ShapeDtypeStruct((m, n), out_dtype),
        compiler_params=_params(2, VMEM_LIMIT),
        name=name,
    )(x, w)


def _mm_rows_kernel(x_ref, w_ref, *o_refs):
    tm = x_ref.shape[0]
    res = _dot(x_ref[...], w_ref[...])
    col = 0
    for o_ref in o_refs:
        n_slab = o_ref.shape[0] // tm
        for j in range(n_slab):
            o_ref[pl.ds(j, tm, stride=n_slab), :] = res[:, col:col + LANES]
            col += LANES


def matmul_rows(x, w, widths, tm, name):
    m, k = x.shape
    n = w.shape[1]
    assert sum(widths) == n and all(width % LANES == 0 for width in widths)
    tm = min(tm, m)
    return pl.pallas_call(
        _mm_rows_kernel,
        grid=(m // tm,),
        in_specs=[pl.BlockSpec((tm, k), lambda i: (i, 0)), pl.BlockSpec((k, n), lambda i: (0, 0))],
        out_specs=[pl.BlockSpec((tm * width // LANES, LANES), lambda i: (i, 0)) for width in widths],
        out_shape=[jax.ShapeDtypeStruct((m * width // LANES, LANES), F32) for width in widths],
        compiler_params=_params(1, VMEM_LIMIT),
        name=name,
    )(x, w)


def _compress_kernel(pt_ref, *refs, n_pages, page, bb):
    n_sub_one = n_pages * page // CMP_STRIDE
    n_pages = bb * n_pages
    page_refs = refs[:n_pages]
    w1_ref, pos_ref, w2_ref, out_ref, lhs_scr, stage_scr = refs[n_pages:]
    sub_per_page = page // CMP_STRIDE
    n_sub = n_pages * sub_per_page
    for p in range(n_pages):
        for cg in range(4):
            stage = stage_scr.at[(p % 2) * 4 + cg]
            for j in range(sub_per_page):
                stage[j * STAGE_PITCH:j * STAGE_PITCH + CMP_STRIDE, :] = (
                    page_refs[p][pl.ds(4 * CMP_STRIDE * j + cg, CMP_STRIDE, stride=4), :])
            for s in range(CMP_STRIDE):
                lhs_scr[cg, p * sub_per_page:(p + 1) * sub_per_page, s * HEAD_DIM:(s + 1) * HEAD_DIM] = (
                    stage[pl.ds(s, sub_per_page, stride=STAGE_PITCH), :])
    row = lax.broadcasted_iota(jnp.int32, (n_sub, HEAD_DIM), 0)
    for c in range(2):
        w1 = w1_ref[c]
        bias = _dot(pos_ref[c], w1)
        b0 = bias[0:1, :HEAD_DIM]
        b1 = bias[1:2, HEAD_DIM:]
        for g in range(2):
            a = _dot(lhs_scr[c * 2 + g].astype(BF16), w1)
            a0 = a[:, :HEAD_DIM] + b0
            a1 = a[:, HEAD_DIM:] + b1
            hid = a0 + pltpu.roll(a1, shift=n_sub - 1, axis=0)
            act = hid * _sigmoid(hid)
            o = _dot(act.astype(BF16), w2_ref[c])
            o = jnp.where(row % n_sub_one < n_sub_one - 1, o, 0.0)
            for i in range(bb):
                out_ref[i, c, g] = o[i * n_sub_one:(i + 1) * n_sub_one]


def _page_specs(n_pages, rows, bb=1):
    return [pl.BlockSpec((rows, LANES),
                         functools.partial(lambda b, pt, i, p: (pt[(b * bb + i) * n_pages + p], 0), i=i, p=p))
            for i in range(bb) for p in range(n_pages)]


def compress_tokens(pool, page_table, page, w1cat, pos8, w2):
    nb, n_pages = page_table.shape
    n_sub = n_pages * page // CMP_STRIDE
    bb = SAMPLE_STEP_BATCH if nb % SAMPLE_STEP_BATCH == 0 else 1
    const = lambda shape: pl.BlockSpec(shape, lambda b, pt: (0,) * len(shape))
    grid_spec = pltpu.PrefetchScalarGridSpec(
        num_scalar_prefetch=1,
        grid=(nb // bb,),
        in_specs=_page_specs(n_pages, 4 * page, bb) + [const(w1cat.shape), const(pos8.shape), const(w2.shape)],
        out_specs=pl.BlockSpec((bb, 2, 2, n_sub, HEAD_DIM), lambda b, pt: (b, 0, 0, 0, 0)),
        scratch_shapes=[pltpu.VMEM((4, bb * n_sub, CMP_STRIDE * HEAD_DIM), F32),
                        pltpu.VMEM((8, page // CMP_STRIDE * STAGE_PITCH, LANES), F32)],
    )
    return pl.pallas_call(
        functools.partial(_compress_kernel, n_pages=n_pages, page=page, bb=bb),
        grid_spec=grid_spec,
        out_shape=jax.ShapeDtypeStruct((nb, 2, 2, n_sub, HEAD_DIM), F32),
        compiler_params=_params(1, VMEM_LIMIT),
        name="compress",
    )(page_table.reshape(-1), *([pool] * (bb * n_pages)), w1cat, pos8, w2)


def compress_weights(phi_pos, phi_w1, phi_w2):
    n_half = CMP_LEN // CMP_STRIDE
    w1 = phi_w1.reshape(2, n_half, CMP_STRIDE, HEAD_DIM, HEAD_DIM).transpose(0, 2, 3, 1, 4)
    w1cat = w1.reshape(2, CMP_STRIDE * HEAD_DIM, n_half * HEAD_DIM).astype(BF16)
    pos = phi_pos.reshape(2, n_half, CMP_STRIDE * HEAD_DIM)
    pos8 = jnp.pad(pos, ((0, 0), (0, 8 - n_half), (0, 0))).astype(BF16)
    return w1cat, pos8, phi_w2.astype(BF16)


def _softmax_attend(s, v):
    e = jnp.exp2(s - jnp.max(s, axis=-1, keepdims=True))
    d = jnp.sum(e, axis=-1, keepdims=True)
    return _dot(e.astype(BF16), v) / jnp.where(d > 0, d, 1.0)


def _biased_attend(q, k, v, bias, key_slope):
    return _softmax_attend(_dot_nt(q, k) * (SCALE * LOG2E) + key_slope + bias, v)


def _top_rank(score, n_blocks):
    blk = lax.broadcasted_iota(jnp.int32, score.shape, 1)
    rank = jnp.zeros(score.shape, F32)
    for n2 in range(n_blocks):
        col = score[:, n2:n2 + 1]
        rank = rank + jnp.where(col > score, 1.0, 0.0) + jnp.where((col == score) & (blk > n2), 1.0, 0.0)
    return rank


def _alibi_slopes(n_heads):
    return [float(2.0 ** (-8.0 * (h + 1) / n_heads)) for h in range(n_heads)]


def _masked_softmax(s, valid, axis):
    s = jnp.where(valid, s, NEG_INF)
    e = jnp.where(valid, jnp.exp(s - jnp.max(s, axis=axis, keepdims=True)), 0.0)
    d = jnp.sum(e, axis=axis, keepdims=True)
    return e / jnp.where(d > 0, d, 1.0)


def _cmp_branch(q, kc, vc, cover, valid):
    imp = None
    outs = []
    for r in range(NSA_GROUP):
        p = _masked_softmax(_dot_nt(q[:, r * HEAD_DIM:(r + 1) * HEAD_DIM], kc) * SCALE, valid, -1)
        p_hi = p.astype(BF16)
        p_lo = (p - p_hi.astype(F32)).astype(BF16)
        outs.append(_dot(p_hi, vc))
        part = _dot(p_hi, cover) + _dot(p_lo, cover)
        imp = part if imp is None else imp + part
    return outs, imp.T


def _select_blocks_t(imp_t, tpos_row, n_sel):
    blk = lax.broadcasted_iota(jnp.int32, imp_t.shape, 0)
    cur = tpos_row // SEL_BLOCK
    forced = (blk == 0) | (blk == cur) | (blk == cur - 1)
    score = jnp.where(blk * SEL_BLOCK <= tpos_row, imp_t + FORCE_BONUS * jnp.where(forced, 1.0, 0.0), NEG_INF)
    rank = jnp.zeros(score.shape, F32)
    for n2 in range(n_sel):
        row = score[n2:n2 + 1, :]
        rank = rank + jnp.where(row > score, 1.0, 0.0) + jnp.where((row == score) & (blk > n2), 1.0, 0.0)
    return jnp.where(rank < min(SEL_TOP_N, n_sel), 1.0, 0.0)


def _select_blocks(imp, tpos, n_sel):
    blk = lax.broadcasted_iota(jnp.int32, imp.shape, 1)
    cur = tpos // SEL_BLOCK
    forced = (blk == 0) | (blk == cur) | (blk == cur - 1)
    allowed = (blk * SEL_BLOCK <= tpos) & (blk < n_sel)
    score = jnp.where(allowed, imp + FORCE_BONUS * jnp.where(forced, 1.0, 0.0), NEG_INF)
    rank = _top_rank(score, n_sel)
    return jnp.where((rank < min(SEL_TOP_N, n_sel)) & (blk < n_sel), 1.0, 0.0)


def _kv_rows(ref, token0, n_tokens, slab, n_slab):
    return ref[pl.ds(token0 * n_slab + slab, n_tokens, stride=n_slab), :]


def _causal_bias(bias, tq):
    w = bias.shape[1]
    row = lax.broadcasted_iota(jnp.int32, (tq, tq), 0)
    col = lax.broadcasted_iota(jnp.int32, (tq, tq), 1)
    diag = jnp.where(row >= col, bias[:, w - tq:], MASKED)
    return diag if w == tq else jnp.concatenate([bias[:, :w - tq], diag], axis=1)


def _nsa_prompt_kernel(q_ref, gate_ref, kcvc_ref, sel_ref, win_ref, cover_ref, expand_ref, wbias_ref, o_ref,
                       ks_scr, vs_scr, kw_scr, vw_scr, *, tq, seq):
    g = pl.program_id(1)
    qi = pl.program_id(2)
    nq = seq // tq

    @pl.when(qi == 0)
    def _():
        ks_scr[...] = _kv_rows(sel_ref, 0, seq, g, 4).astype(BF16)
        vs_scr[...] = _kv_rows(sel_ref, 0, seq, 2 + g, 4).astype(BF16)
        kw_scr[...] = _kv_rows(win_ref, 0, seq, g, 4).astype(BF16)
        vw_scr[...] = _kv_rows(win_ref, 0, seq, 2 + g, 4).astype(BF16)

    n_sel = seq // SEL_BLOCK
    n_sub = seq // CMP_STRIDE
    n_win = WINDOW // tq
    tpos = qi * tq + lax.broadcasted_iota(jnp.int32, (tq, 1), 0)
    tpos_row = qi * tq + lax.broadcasted_iota(jnp.int32, (1, tq), 1)
    q = q_ref[...]

    n_idx = lax.broadcasted_iota(jnp.int32, (1, n_sub), 1)
    valid = (n_idx * CMP_STRIDE + (CMP_LEN - 1) <= tpos) & (n_idx < n_sub - 1)
    o_cmp, imp_t = _cmp_branch(q, kcvc_ref[0, 0, 0].astype(BF16), kcvc_ref[0, 1, 0].astype(BF16), cover_ref[...],
                               valid)
    sel_t = _select_blocks_t(imp_t[:n_sel], tpos_row, n_sel)
    selected = jnp.concatenate([sel_t, jnp.zeros((LANES - n_sel, tq), F32)], axis=0).T.astype(BF16)

    slopes = _alibi_slopes(NSA_HEADS)
    slope = [jnp.where(g == 0, jnp.float32(slopes[r] * LOG2E), jnp.float32(slopes[NSA_GROUP + r] * LOG2E))
             for r in range(NSA_GROUP)]
    gates = _sigmoid(gate_ref[...])

    for c in range(nq):
        @pl.when(qi == c)
        def _(c=c):
            w = (c + 1) * tq
            bias = _causal_bias(_dot(selected, expand_ref[:, :w]) + MASKED, tq)
            k = ks_scr[:w, :]
            v = vs_scr[:w, :]
            kpos = lax.broadcasted_iota(jnp.int32, (1, w), 1).astype(F32)
            c0 = max(c - n_win, 0)
            ww = (c + 1 - c0) * tq
            kw = kw_scr[c0 * tq:(c + 1) * tq, :]
            vw = vw_scr[c0 * tq:(c + 1) * tq, :]
            wbias = wbias_ref[:, (n_win + 1) * tq - ww:]
            kpos_w = (c0 * tq + lax.broadcasted_iota(jnp.int32, (1, ww), 1)).astype(F32)
            for r in range(NSA_GROUP):
                qr = q[:, r * HEAD_DIM:(r + 1) * HEAD_DIM]
                o_sel = _biased_attend(qr, k, v, bias, slope[r] * kpos)
                o_win = _biased_attend(qr, kw, vw, wbias, slope[r] * kpos_w)
                o_ref[:, r * HEAD_DIM:(r + 1) * HEAD_DIM] = (gates[:, 3 * r:3 * r + 1] * o_cmp[r]
                                                              + gates[:, 3 * r + 1:3 * r + 2] * o_sel
                                                              + gates[:, 3 * r + 2:3 * r + 3] * o_win)


def nsa_prompt_constants(n_sub, n_keys, tq):
    cover, _ = nsa_constants(n_sub, LANES)
    expand = np.where(np.arange(n_keys)[None, :] // SEL_BLOCK == np.arange(LANES)[:, None], -MASKED, 0.0)
    dist = WINDOW + np.arange(tq)[:, None] - np.arange(WINDOW + tq)[None, :]
    wbias = np.where((dist >= 0) & (dist < WINDOW), 0.0, MASKED)
    return cover, jnp.asarray(expand, BF16), jnp.asarray(wbias, F32)


def nsa_constants(n_sub, n_keys):
    c_start = np.arange(n_sub)[:, None] * CMP_STRIDE
    s_start = np.arange(LANES)[None, :] * SEL_BLOCK
    cover = ((c_start < s_start + SEL_BLOCK) & (c_start + CMP_LEN > s_start)).astype(np.float32)
    cover[n_sub - 1:] = 0.0
    expand = (np.arange(n_keys)[None, :] // SEL_BLOCK == np.arange(LANES)[:, None]).astype(np.float32)
    return jnp.asarray(cover, BF16), jnp.asarray(expand, BF16)


def nsa_prompt(q_all, gates, kcvc, sel_kv, win_kv, batch, seq, tq=NSA_Q_ROWS):
    assert WINDOW % tq == 0 and seq % tq == 0 and (seq // SEL_BLOCK) % 8 == 0
    nq = seq // tq
    n_sub = seq // CMP_STRIDE
    cover, expand, wbias = nsa_prompt_constants(n_sub, seq, tq)
    gw = NSA_GROUP * HEAD_DIM
    kv_spec = pl.BlockSpec((seq * 4, LANES), lambda b, g, i: (b, 0))
    const = lambda a: pl.BlockSpec(a.shape, lambda b, g, i: (0, 0))
    return pl.pallas_call(
        functools.partial(_nsa_prompt_kernel, tq=tq, seq=seq),
        grid=(batch, NSA_KV_HEADS, nq),
        in_specs=[
            pl.BlockSpec((tq, gw), lambda b, g, i: (b * nq + i, g)),
            pl.BlockSpec((tq, LANES), lambda b, g, i: (b * nq + i, g)),
            pl.BlockSpec((1, 2, 1, n_sub, HEAD_DIM), lambda b, g, i: (b, 0, g, 0, 0)),
            kv_spec, kv_spec, const(cover), const(expand), const(wbias),
        ],
        out_specs=pl.BlockSpec((tq, gw), lambda b, g, i: (b * nq + i, g)),
        out_shape=jax.ShapeDtypeStruct((batch * seq, NSA_HEADS * HEAD_DIM), F32),
        scratch_shapes=[pltpu.VMEM((seq, HEAD_DIM), BF16)] * 4,
        compiler_params=pltpu.CompilerParams(dimension_semantics=("parallel", "parallel", "arbitrary"),
                                             vmem_limit_bytes=VMEM_LIMIT),
        name="nsa_prompt",
    )(q_all, gates, kcvc, sel_kv, win_kv, cover, expand, wbias)


def _moba_select(gate, n_past, n_blocks):
    blk = lax.broadcasted_iota(jnp.int32, gate.shape, 1)
    past = blk < n_past
    score = jnp.where(past, gate, NEG_INF)
    rank = _top_rank(score, n_blocks)
    return jnp.where(past & (rank < min(MOBA_TOP_K, n_blocks)), 1.0, 0.0)


def _moba_select_t(gate_t, n_past, n_blocks):
    blk = lax.broadcasted_iota(jnp.int32, gate_t.shape, 0)
    past = blk < n_past
    score = jnp.where(past, gate_t, NEG_INF)
    rank = jnp.zeros(score.shape, F32)
    for n2 in range(n_blocks):
        row = score[n2:n2 + 1, :]
        rank = rank + jnp.where(row > score, 1.0, 0.0) + jnp.where((row == score) & (blk > n2), 1.0, 0.0)
    return jnp.where(past & (rank < min(MOBA_TOP_K, n_blocks)), 1.0, 0.0)


def _moba_prompt_kernel(q_ref, kv_ref, o_ref, kmean_scr, k_scr, v_scr, *, seq):
    tq = MOBA_BLOCK
    n_slab = 2 * MOBA_HEADS
    qi = pl.program_id(1)
    nb = seq // tq
    slopes = [s * LOG2E for s in _alibi_slopes(MOBA_HEADS)]

    @pl.when(qi == 0)
    def _():
        kmean_scr[...] = jnp.zeros(kmean_scr.shape, F32)
        for h in range(MOBA_HEADS):
            k = _kv_rows(kv_ref, 0, seq, h, n_slab)
            k_scr[h] = k.astype(BF16)
            v_scr[h] = _kv_rows(kv_ref, 0, seq, MOBA_HEADS + h, n_slab).astype(BF16)
            for n in range(nb):
                kmean_scr[h, n:n + 1, :] = jnp.mean(k[n * tq:(n + 1) * tq], axis=0, keepdims=True)

    row = lax.broadcasted_iota(jnp.int32, (tq, tq), 0)
    col = lax.broadcasted_iota(jnp.int32, (tq, tq), 1)
    own = jnp.where(row >= col, 0.0, MASKED)
    q, sel_bias = [], []
    nb8 = -(-nb // 8) * 8
    for h in range(MOBA_HEADS):
        q.append(q_ref[:, h * HEAD_DIM:(h + 1) * HEAD_DIM])
        gate_t = _dot_nt(kmean_scr[h].astype(BF16), q[h])[:nb8]
        bias_t = (1.0 - _moba_select_t(gate_t, qi, nb - 1)) * MASKED
        sel_bias.append(jnp.concatenate([bias_t, jnp.zeros((LANES - nb8, tq), F32)], axis=0).T)

    for c in range(nb):
        @pl.when(qi == c)
        def _(c=c):
            w = (c + 1) * tq
            kpos = lax.broadcasted_iota(jnp.int32, (1, w), 1).astype(F32)
            for h in range(MOBA_HEADS):
                bias = jnp.concatenate(
                    [jnp.broadcast_to(sel_bias[h][:, n:n + 1], (tq, tq)) for n in range(c)] + [own], axis=1)
                o_ref[:, h * HEAD_DIM:(h + 1) * HEAD_DIM] = _biased_attend(q[h], k_scr[h, :w, :], v_scr[h, :w, :], bias,
                                                                            slopes[h] * kpos)


def moba_prompt(q_all, q_col, moba_kv, batch, seq):
    tq = MOBA_BLOCK
    nq = seq // tq
    width = MOBA_HEADS * HEAD_DIM
    return pl.pallas_call(
        functools.partial(_moba_prompt_kernel, seq=seq),
        grid=(batch, nq),
        in_specs=[
            pl.BlockSpec((tq, width), lambda b, i: (b * nq + i, q_col)),
            pl.BlockSpec((seq * 2 * MOBA_HEADS, LANES), lambda b, i: (b, 0)),
        ],
        out_specs=pl.BlockSpec((tq, width), lambda b, i: (b * nq + i, 0)),
        out_shape=jax.ShapeDtypeStruct((batch * seq, width), F32),
        scratch_shapes=[pltpu.VMEM((MOBA_HEADS, LANES, HEAD_DIM), F32), pltpu.VMEM((MOBA_HEADS, seq, HEAD_DIM), BF16),
                        pltpu.VMEM((MOBA_HEADS, seq, HEAD_DIM), BF16)],
        compiler_params=pltpu.CompilerParams(dimension_semantics=("parallel", "arbitrary"),
                                             vmem_limit_bytes=VMEM_LIMIT),
        name="moba_prompt",
    )(q_all, moba_kv)


def _padded_tile(rows, n):
    return jnp.concatenate([rows, jnp.zeros((n - rows.shape[0], rows.shape[1]), F32)], axis=0).astype(BF16)


def _nsa_sample_kernel(pt_ref, *refs, n_pages, page, t_new, n_win, bb):
    page_refs = refs[:bb * n_pages]
    (q_ref, gate_ref, kcvc_ref, new_sel_ref, cwin_ref, new_win_ref, cover_ref, expand_ref, sbias_ref, wbias_ref,
     o_ref, win_out_ref, ks_scr, vs_scr, kw_scr, vw_scr) = refs[bb * n_pages:]
    n_slab = 2 * NSA_KV_HEADS
    past = n_pages * page
    n_sub = past // CMP_STRIDE
    n_sel = -(-(past + t_new) // SEL_BLOCK)
    rows = NSA_GROUP * t_new
    keep = win_out_ref.shape[0] // (bb * n_slab)
    chains = [(i, g) for i in range(bb) for g in range(NSA_KV_HEADS)]

    for i in range(bb):
        out0, kept = i * keep * n_slab, (keep - t_new) * n_slab
        win_out_ref[out0:out0 + kept, :] = cwin_ref[(i + 1) * n_win * n_slab - kept:(i + 1) * n_win * n_slab, :]
        win_out_ref[out0 + kept:out0 + keep * n_slab, :] = new_win_ref[i * t_new * n_slab:(i + 1) * t_new * n_slab, :]

    for c, (i, g) in enumerate(chains):
        for p in range(n_pages):
            ks_scr[c, p * page:(p + 1) * page, :] = _kv_rows(page_refs[i * n_pages + p], 0, page, g, n_slab).astype(BF16)
            vs_scr[c, p * page:(p + 1) * page, :] = (
                _kv_rows(page_refs[i * n_pages + p], 0, page, 2 + g, n_slab).astype(BF16))
        ks_scr[c, past:, :] = _padded_tile(_kv_rows(new_sel_ref, i * t_new, t_new, g, n_slab), LANES)
        vs_scr[c, past:, :] = _padded_tile(_kv_rows(new_sel_ref, i * t_new, t_new, 2 + g, n_slab), LANES)
        kw_scr[c, :n_win, :] = _kv_rows(cwin_ref, i * n_win, n_win, g, n_slab).astype(BF16)
        vw_scr[c, :n_win, :] = _kv_rows(cwin_ref, i * n_win, n_win, 2 + g, n_slab).astype(BF16)
        kw_scr[c, n_win:, :] = _padded_tile(_kv_rows(new_win_ref, i * t_new, t_new, g, n_slab), LANES)
        vw_scr[c, n_win:, :] = _padded_tile(_kv_rows(new_win_ref, i * t_new, t_new, 2 + g, n_slab), LANES)

    q_all = [_stack_heads(q_ref, i * t_new, t_new, NSA_HEADS) for i in range(bb)]
    q = [q_all[i][g * rows:(g + 1) * rows] for i, g in chains]
    tpos = past + lax.broadcasted_iota(jnp.int32, (rows, 1), 0) % t_new
    tpos_q = past + lax.broadcasted_iota(jnp.int32, (t_new, 1), 0)
    n_idx = lax.broadcasted_iota(jnp.int32, (1, n_sub), 1)
    valid = (n_idx * CMP_STRIDE + (CMP_LEN - 1) <= tpos) & (n_idx < n_sub - 1)
    cmp_out = [_cmp_branch_rows(q[c], kcvc_ref[i, 0, g].astype(BF16), kcvc_ref[i, 1, g].astype(BF16),
                                cover_ref[...], valid) for c, (i, g) in enumerate(chains)]
    selected = []
    for c in range(len(chains)):
        imp = cmp_out[c][1]
        imp_t = imp[0:t_new]
        for r in range(1, NSA_GROUP):
            imp_t = imp_t + imp[r * t_new:(r + 1) * t_new]
        selected.append(jnp.concatenate([_select_blocks(imp_t, tpos_q, n_sel)] * NSA_GROUP, axis=0).astype(BF16))
    o_sel, o_win = [], []
    for c, (i, g) in enumerate(chains):
        bias = (_dot(selected[c], expand_ref[...]) + MASKED) + sbias_ref[g]
        o_sel.append(_softmax_attend(_dot_nt(q[c], ks_scr[c]) * (SCALE * LOG2E) + bias, vs_scr[c]))
        o_win.append(_softmax_attend(_dot_nt(q[c], kw_scr[c]) * (SCALE * LOG2E) + wbias_ref[g], vw_scr[c]))

    gates = _sigmoid(gate_ref[...])
    for c, (i, g) in enumerate(chains):
        out_rows = slice(i * t_new, (i + 1) * t_new)
        for r in range(NSA_GROUP):
            h = g * NSA_GROUP + r
            lane = g * LANES + 3 * r
            rs = slice(r * t_new, (r + 1) * t_new)
            o_ref[out_rows, h * HEAD_DIM:(h + 1) * HEAD_DIM] = (
                gates[out_rows, lane:lane + 1] * cmp_out[c][0][rs]
                + gates[out_rows, lane + 1:lane + 2] * o_sel[c][rs]
                + gates[out_rows, lane + 2:lane + 3] * o_win[c][rs])


def _cmp_branch_rows(q, kc, vc, cover, valid):
    s = jnp.where(valid, _dot_nt(q, kc) * SCALE, NEG_INF)
    e = jnp.where(valid, jnp.exp(s - jnp.max(s, axis=-1, keepdims=True)), 0.0)
    d = jnp.sum(e, axis=-1, keepdims=True)
    p = e / jnp.where(d > 0, d, 1.0)
    p_hi = p.astype(BF16)
    p_lo = (p - p_hi.astype(F32)).astype(BF16)
    return _dot(p_hi, vc), _dot(p_hi, cover) + _dot(p_lo, cover)


def nsa_sample(q_all, gates, kcvc, sel_pool, page_table, page, new_sel, cache_win, new_win, t_new):
    nb, n_pages = page_table.shape
    past = n_pages * page
    n_sub = past // CMP_STRIDE
    n_slab = 2 * NSA_KV_HEADS
    n_win = cache_win.shape[0] // (n_slab * nb)
    assert past % SEL_BLOCK == 0 and t_new <= SEL_BLOCK and past // SEL_BLOCK + 2 <= LANES
    cover, expand = nsa_constants(n_sub, past + LANES)
    expand = expand * jnp.asarray(-MASKED, BF16)
    slopes = _alibi_slopes(NSA_HEADS)
    t_pos = past + np.arange(t_new)

    def alibi_bias(pos, ok):
        per_head = [np.where(ok, s * LOG2E * pos[None, :], MASKED) for s in slopes]
        return jnp.asarray(np.stack(per_head).reshape(NSA_KV_HEADS, NSA_GROUP * t_new, pos.shape[0]), F32)

    sel_pos = np.arange(past + LANES)
    sbias = alibi_bias(sel_pos, (sel_pos[None, :] <= t_pos[:, None]) & (sel_pos[None, :] < past + t_new))
    win_pos = past - n_win + np.arange(n_win + LANES)
    dist = t_pos[:, None] - win_pos[None, :]
    wbias = alibi_bias(win_pos, (dist >= 0) & (dist < WINDOW) & (win_pos[None, :] >= 0)
                       & (win_pos[None, :] < past + t_new))
    keep = min(WINDOW, n_win + t_new)
    assert keep >= t_new and (n_slab * (keep - t_new)) % 8 == 0
    bb = SAMPLE_STEP_BATCH if nb % SAMPLE_STEP_BATCH == 0 else 1
    n_chain = bb * NSA_KV_HEADS
    width = NSA_HEADS * HEAD_DIM
    const = lambda a: pl.BlockSpec(a.shape, lambda b, pt: (0,) * a.ndim)
    rows = lambda n: pl.BlockSpec((bb * n_slab * n, LANES), lambda b, pt: (b, 0))
    grid_spec = pltpu.PrefetchScalarGridSpec(
        num_scalar_prefetch=1,
        grid=(nb // bb,),
        in_specs=_page_specs(n_pages, n_slab * page, bb) + [
            pl.BlockSpec((bb * t_new, width), lambda b, pt: (b, 0)),
            pl.BlockSpec((bb * t_new, NSA_KV_HEADS * LANES), lambda b, pt: (b, 0)),
            pl.BlockSpec((bb, 2, 2, n_sub, HEAD_DIM), lambda b, pt: (b, 0, 0, 0, 0)),
            rows(t_new), rows(n_win), rows(t_new),
            const(cover), const(expand), const(sbias), const(wbias)],
        out_specs=[pl.BlockSpec((bb * t_new, width), lambda b, pt: (b, 0)), rows(keep)],
        scratch_shapes=[pltpu.VMEM((n_chain, past + LANES, HEAD_DIM), BF16),
                        pltpu.VMEM((n_chain, past + LANES, HEAD_DIM), BF16),
                        pltpu.VMEM((n_chain, n_win + LANES, HEAD_DIM), BF16),
                        pltpu.VMEM((n_chain, n_win + LANES, HEAD_DIM), BF16)],
    )
    return pl.pallas_call(
        functools.partial(_nsa_sample_kernel, n_pages=n_pages, page=page, t_new=t_new, n_win=n_win, bb=bb),
        grid_spec=grid_spec,
        out_shape=[jax.ShapeDtypeStruct((nb * t_new, width), F32),
                   jax.ShapeDtypeStruct((nb * keep * n_slab, LANES), F32)],
        compiler_params=_params(1, VMEM_LIMIT),
        name="nsa_sample",
    )(page_table.reshape(-1), *([sel_pool] * (bb * n_pages)), q_all, gates, kcvc, new_sel, cache_win, new_win, cover,
      expand, sbias, wbias)


def _slab_attend(q, kv, bias, shift):
    s = _dot_nt(q, kv) * (SCALE * LOG2E) + bias
    e = jnp.exp2(s - jnp.max(s, axis=-1, keepdims=True))
    d = jnp.sum(e, axis=-1, keepdims=True)
    return _dot(pltpu.roll(e, shift, axis=1).astype(BF16), kv) / jnp.where(d > 0, d, 1.0)


def _stack_heads(ref, row0, n_rows, n_heads):
    return jnp.concatenate([ref[row0:row0 + n_rows, h * HEAD_DIM:(h + 1) * HEAD_DIM] for h in range(n_heads)],
                           axis=0).astype(BF16)


def _slab_bias(head_slab, slopes, key_pos, key_ok, n_slab, t_new):
    n_heads, n_keys = len(head_slab), key_pos.shape[0]
    slab = np.arange(n_keys * n_slab) % n_slab
    out = np.full((n_heads, t_new, n_keys * n_slab), MASKED, np.float64)
    for h in range(n_heads):
        ok = np.repeat(key_ok, n_slab, axis=1) & (slab == head_slab[h])[None, :]
        out[h] = np.where(ok, slopes[h] * LOG2E * np.repeat(key_pos, n_slab)[None, :], MASKED)
    return jnp.asarray(out.reshape(n_heads * t_new, n_keys * n_slab), F32)


def _moba_sample_kernel(pt_ref, *refs, n_pages, page, t_new):
    page_refs = refs[:n_pages]
    q_ref, new_ref, bias_ref, o_ref, kv_scr, mean_scr = refs[n_pages:]
    n_slab = 2 * MOBA_HEADS
    rows_pp = page * n_slab
    past = n_pages * page
    pages_per_block = MOBA_BLOCK // page
    n_past = n_pages // pages_per_block
    block_w = MOBA_BLOCK * n_slab

    mean_scr[...] = jnp.zeros(mean_scr.shape, F32)
    for n in range(n_past):
        total = None
        for p in range(n * pages_per_block, (n + 1) * pages_per_block):
            x = page_refs[p][...]
            kv_scr[p * rows_pp:(p + 1) * rows_pp, :] = x.astype(BF16)
            part = jnp.sum(x.reshape(page, n_slab, LANES), axis=0)
            total = part if total is None else total + part
        mean_scr[n * n_slab:(n + 1) * n_slab, :] = total * (1.0 / MOBA_BLOCK)
    x_new = new_ref[...]
    kv_scr[past * n_slab:, :] = _padded_tile(x_new, LANES)
    mean_scr[n_past * n_slab:(n_past + 1) * n_slab, :] = (
        jnp.sum(x_new.reshape(t_new, n_slab, LANES), axis=0) * (1.0 / MOBA_BLOCK))

    tpos = past + lax.broadcasted_iota(jnp.int32, (t_new, 1), 0)
    cur = tpos // MOBA_BLOCK
    blk = lax.broadcasted_iota(jnp.int32, (t_new, LANES), 1)
    sel_bias = []
    for h in range(MOBA_HEADS):
        q = q_ref[:, h * HEAD_DIM:(h + 1) * HEAD_DIM].astype(BF16)
        kmean = mean_scr[pl.ds(h, LANES // n_slab, stride=n_slab), :]
        kmean = jnp.concatenate([kmean, jnp.zeros((LANES - kmean.shape[0], LANES), F32)], axis=0)
        gate = _dot_nt(q, kmean.astype(BF16))
        selected = jnp.where(blk == cur, 1.0, _moba_select(gate, cur, n_past))
        sel_bias.append((1.0 - selected) * MASKED)
    sel_bias = jnp.concatenate(sel_bias, axis=0)

    bias = jnp.concatenate(
        [bias_ref[:, n * block_w:(n + 1) * block_w] + sel_bias[:, n:n + 1] for n in range(n_past)]
        + [bias_ref[:, n_past * block_w:] + sel_bias[:, n_past:n_past + 1]], axis=1)
    o = _slab_attend(_stack_heads(q_ref, 0, t_new, MOBA_HEADS), kv_scr[...], bias, MOBA_HEADS)
    for h in range(MOBA_HEADS):
        o_ref[:, h * HEAD_DIM:(h + 1) * HEAD_DIM] = o[h * t_new:(h + 1) * t_new]


def moba_sample(q_all, q_col, moba_pool, page_table, page, new_kv, t_new):
    nb, n_pages = page_table.shape
    past = n_pages * page
    width = MOBA_HEADS * HEAD_DIM
    n_slab = 2 * MOBA_HEADS
    assert past % MOBA_BLOCK == 0 and t_new * n_slab <= LANES and past // MOBA_BLOCK < LANES // n_slab
    new_keys = LANES // n_slab
    key_pos = np.arange(past + new_keys)
    t_pos = past + np.arange(t_new)
    key_ok = (key_pos[None, :] <= t_pos[:, None]) & (key_pos[None, :] < past + t_new)
    bias = _slab_bias(list(range(MOBA_HEADS)), _alibi_slopes(MOBA_HEADS), key_pos, key_ok, n_slab, t_new)
    grid_spec = pltpu.PrefetchScalarGridSpec(
        num_scalar_prefetch=1,
        grid=(nb,),
        in_specs=_page_specs(n_pages, n_slab * page) + [
            pl.BlockSpec((t_new, width), lambda b, pt: (b, q_col)),
            pl.BlockSpec((t_new * n_slab, LANES), lambda b, pt: (b, 0)),
            pl.BlockSpec(bias.shape, lambda b, pt: (0, 0))],
        out_specs=pl.BlockSpec((t_new, width), lambda b, pt: (b, 0)),
        scratch_shapes=[pltpu.VMEM((past * n_slab + LANES, HEAD_DIM), BF16), pltpu.VMEM((LANES, LANES), F32)],
    )
    return pl.pallas_call(
        functools.partial(_moba_sample_kernel, n_pages=n_pages, page=page, t_new=t_new),
        grid_spec=grid_spec,
        out_shape=jax.ShapeDtypeStruct((nb * t_new, width), F32),
        compiler_params=_params(1, VMEM_LIMIT),
        name="moba_sample",
    )(page_table.reshape(-1), *([moba_pool] * n_pages), q_all, new_kv, bias)


def _cross_kernel(q_ref, kv_ref, o_ref, *, n_mem):
    n_slab = 2 * MEM_HEADS
    for h in range(MEM_HEADS):
        k = _kv_rows(kv_ref, 0, n_mem, h, n_slab).astype(BF16)
        v = _kv_rows(kv_ref, 0, n_mem, MEM_HEADS + h, n_slab).astype(BF16)
        s = _dot_nt(q_ref[:, h * HEAD_DIM:(h + 1) * HEAD_DIM], k) * SCALE
        e = jnp.exp(s - jnp.max(s, axis=-1, keepdims=True))
        p = e / jnp.sum(e, axis=-1, keepdims=True)
        o_ref[:, h * HEAD_DIM:(h + 1) * HEAD_DIM] = _dot(p.astype(BF16), v)


def cross_attend(q_all, q_col, mem_kv, batch, rows, tq):
    nq = rows // tq
    n_slab = 2 * MEM_HEADS
    width = MEM_HEADS * HEAD_DIM
    n_mem = mem_kv.shape[0] // (batch * n_slab)
    return pl.pallas_call(
        functools.partial(_cross_kernel, n_mem=n_mem),
        grid=(batch, nq),
        in_specs=[
            pl.BlockSpec((tq, width), lambda b, i: (b * nq + i, q_col)),
            pl.BlockSpec((n_mem * n_slab, LANES), lambda b, i: (b, 0)),
        ],
        out_specs=pl.BlockSpec((tq, width), lambda b, i: (b * nq + i, 0)),
        out_shape=jax.ShapeDtypeStruct((batch * rows, width), F32),
        compiler_params=_params(2, VMEM_LIMIT),
        name="cross_attend",
    )(q_all, mem_kv)


def _cross_sample_kernel(q_ref, kv_ref, bias_ref, o_ref, *, n_mem, t_new, bb):
    n_slab = 2 * MEM_HEADS
    rows = n_mem * n_slab
    for i in range(bb):
        q = _stack_heads(q_ref, i * t_new, t_new, MEM_HEADS)
        o = _slab_attend(q, kv_ref[i * rows:(i + 1) * rows, :].astype(BF16), bias_ref[...], MEM_HEADS)
        for h in range(MEM_HEADS):
            o_ref[i * t_new:(i + 1) * t_new, h * HEAD_DIM:(h + 1) * HEAD_DIM] = o[h * t_new:(h + 1) * t_new]


def cross_attend_sample(q_all, q_col, mem_kv, batch, t_new, bb):
    assert batch % bb == 0
    n_slab = 2 * MEM_HEADS
    width = MEM_HEADS * HEAD_DIM
    n_mem = mem_kv.shape[0] // (batch * n_slab)
    bias = _slab_bias(list(range(MEM_HEADS)), [0.0] * MEM_HEADS, np.arange(n_mem), np.ones((t_new, n_mem), bool),
                      n_slab, t_new)
    return pl.pallas_call(
        functools.partial(_cross_sample_kernel, n_mem=n_mem, t_new=t_new, bb=bb),
        grid=(batch // bb,),
        in_specs=[
            pl.BlockSpec((bb * t_new, width), lambda b: (b, q_col)),
            pl.BlockSpec((bb * n_mem * n_slab, LANES), lambda b: (b, 0)),
            pl.BlockSpec(bias.shape, lambda b: (0, 0)),
        ],
        out_specs=pl.BlockSpec((bb * t_new, width), lambda b: (b, 0)),
        out_shape=jax.ShapeDtypeStruct((batch * t_new, width), F32),
        compiler_params=_params(1, VMEM_LIMIT),
        name="cross_sample",
    )(q_all, mem_kv, bias)


def _merge_kernel(x_ref, on_ref, om_ref, oc_ref, z_ref, gl_ref, wn_ref, wm_ref, wc_ref, wo_ref, fg_ref, y_ref, mix_scr,
                  *, tn):
    d = x_ref.shape[1]
    wn_k, wm_k = wn_ref.shape[0], wm_ref.shape[0]
    z = z_ref[...]
    act = z * _sigmoid(z)
    un = (on_ref[...] * act[:, :wn_k]).astype(BF16)
    um = (om_ref[...] * act[:, wn_k:wn_k + wm_k]).astype(BF16)
    uc = (oc_ref[...] * act[:, wn_k + wm_k:]).astype(BF16)
    for c in range(d // tn):
        cols = slice(c * tn, (c + 1) * tn)
        mix = (_sigmoid(gl_ref[:, c * tn:(c + 1) * tn]) * _dot(un, wn_ref[:, cols])
               + _sigmoid(gl_ref[:, d + c * tn:d + (c + 1) * tn]) * _dot(um, wm_ref[:, cols])
               + _sigmoid(gl_ref[:, 2 * d + c * tn:2 * d + (c + 1) * tn]) * _dot(uc, wc_ref[:, cols]))
        mix_scr[:, cols] = mix.astype(BF16)
    y = x_ref[...] + _dot(mix_scr[...], wo_ref[...])
    inv = lax.rsqrt(jnp.mean(y * y, axis=-1, keepdims=True) + RMS_EPS)
    y_ref[...] = y * inv * fg_ref[...]


def merge_branches(x, o_nsa, o_moba, o_mem, z_all, gate_logits, w_nsa, w_moba, w_mem, w_out, final_g,
                   tm=MERGE_ROWS, tn=MERGE_COLS):
    m, d = x.shape
    tm = min(tm, m)
    row = lambda a: pl.BlockSpec((tm, a.shape[1]), lambda i: (i, 0))
    whole = lambda a: pl.BlockSpec(a.shape, lambda i: (0, 0), pipeline_mode=pl.Buffered(1))
    final_g = final_g.reshape(1, d)
    return pl.pallas_call(
        functools.partial(_merge_kernel, tn=tn),
        grid=(m // tm,),
        in_specs=[row(x), row(o_nsa), row(o_moba), row(o_mem), row(z_all), row(gate_logits),
                  whole(w_nsa), whole(w_moba), whole(w_mem), whole(w_out), whole(final_g)],
        out_specs=row(x),
        out_shape=jax.ShapeDtypeStruct((m, d), F32),
        scratch_shapes=[pltpu.VMEM((tm, d), BF16)],
        compiler_params=_params(1, VMEM_LIMIT),
        name="merge",
    )(x, o_nsa, o_moba, o_mem, z_all, gate_logits, w_nsa, w_moba, w_mem, w_out, final_g)


def _in_weights(w_in, d_model):
    splits = (NSA_HEADS * HEAD_DIM, 4 * HEAD_DIM, 4 * HEAD_DIM, 4 * HEAD_DIM, 3 * NSA_HEADS, NSA_HEADS * HEAD_DIM,
              MOBA_HEADS * HEAD_DIM, 2 * MOBA_HEADS * HEAD_DIM, MOBA_HEADS * HEAD_DIM, MEM_HEADS * HEAD_DIM,
              MEM_HEADS * HEAD_DIM, N_BRANCH * d_model)
    offs = np.cumsum((0,) + splits)
    assert offs[-1] == w_in.shape[1]
    (nsa_q, cmp_kv, sel_kv, win_kv, nsa_g, nsa_z, moba_q, moba_kv, moba_z, mem_q, mem_z, merge_g) = (
        w_in[:, offs[i]:offs[i + 1]] for i in range(len(splits)))
    per_group = 3 * NSA_GROUP
    gate_w = jnp.pad(nsa_g.reshape(-1, NSA_KV_HEADS, per_group), ((0, 0), (0, 0), (0, LANES - per_group)))
    cast = lambda w: w.astype(BF16)
    return dict(q=cast(jnp.concatenate([nsa_q, moba_q, mem_q], axis=1)),
                nsa_kv=cast(jnp.concatenate([cmp_kv, sel_kv, win_kv], axis=1)), moba=cast(moba_kv),
                gate=cast(gate_w.reshape(-1, NSA_KV_HEADS * LANES)),
                z=cast(jnp.concatenate([nsa_z, moba_z, mem_z], axis=1)), merge=cast(merge_g))


def _in_projection(x, norm_g, w, q_dtype):
    xn = rmsnorm_bf16(x, norm_g, NORM_ROWS)
    tm = min(PROJ_ROWS, x.shape[0] // 2)
    mm = lambda name, dtype: matmul(xn, w[name], dtype, tm, PROJ_COLS, "in_" + name)
    nsa_w = 4 * HEAD_DIM
    cmp_kv, sel_kv, win_kv = matmul_rows(xn, w["nsa_kv"], [nsa_w] * 3, min(KV_ROWS, x.shape[0] // 2), "in_nsa_kv")
    moba_kv, = matmul_rows(xn, w["moba"], [2 * MOBA_HEADS * HEAD_DIM], min(KV_ROWS, x.shape[0] // 2), "in_moba")
    return dict(q=mm("q", q_dtype), cmp=cmp_kv, sel=sel_kv, win=win_kv, moba=moba_kv,
                gate=mm("gate", F32), z=mm("z", F32), merge=mm("merge", F32))


def kernel(x_prompt, x_sample, cache_cmp_kv, cache_sel_kv, cache_moba_kv, cache_win_kv, cache_mem_kv, page_table,
           mem_prompt, norm_g, w_in, phi_pos, phi_w1, phi_w2, mem_norm_g, w_mem_kv, w_br_nsa, w_br_moba, w_br_mem,
           w_out, final_norm_g):
    batch, seq, d_model = x_prompt.shape
    nb, t_new, _ = x_sample.shape
    n_mem = mem_prompt.shape[1]
    page = cache_cmp_kv.shape[2]
    n_win = cache_win_kv.shape[2]
    assert norm_g.shape[0] == 1, "one layer"
    assert seq % MOBA_BLOCK == 0 and (page_table.shape[1] * page) % MOBA_BLOCK == 0 and MOBA_BLOCK % page == 0
    l = 0
    q_moba_col = NSA_HEADS * HEAD_DIM // (MOBA_HEADS * HEAD_DIM)
    q_mem_col = (NSA_HEADS + MOBA_HEADS) * HEAD_DIM // (MEM_HEADS * HEAD_DIM)
    lane_rows = lambda a: a.reshape(-1, LANES)

    w = _in_weights(w_in[l], d_model)
    w1cat, pos8, w2 = compress_weights(phi_pos[l], phi_w1[l], phi_w2[l])
    br = [a[l].astype(BF16) for a in (w_br_nsa, w_br_moba, w_br_mem, w_out)]

    xp = x_prompt.reshape(batch * seq, d_model)
    hp = _in_projection(xp, norm_g[l], w, BF16)
    mem_n = rmsnorm_bf16(mem_prompt.reshape(batch * n_mem, d_model), mem_norm_g[l], NORM_ROWS)
    mem_kv_p, = matmul_rows(mem_n, w_mem_kv[l].astype(BF16), [w_mem_kv.shape[2]], KV_ROWS, "mem_kv")
    pages_p = seq // page
    identity_table = jnp.arange(batch * pages_p, dtype=jnp.int32).reshape(batch, pages_p)
    kcvc_p = compress_tokens(hp["cmp"], identity_table, page, w1cat, pos8, w2)
    o_nsa_p = nsa_prompt(hp["q"], hp["gate"], kcvc_p, hp["sel"], hp["win"], batch, seq)
    o_moba_p = moba_prompt(hp["q"], q_moba_col, hp["moba"], batch, seq)
    o_mem_p = cross_attend(hp["q"], q_mem_col, mem_kv_p, batch, seq, min(seq, CROSS_Q_ROWS))
    y_prompt = merge_branches(xp, o_nsa_p, o_moba_p, o_mem_p, hp["z"], hp["merge"], *br, final_norm_g)

    xs = x_sample.reshape(nb * t_new, d_model)
    hs = _in_projection(xs, norm_g[l], w, F32)
    kcvc_s = compress_tokens(lane_rows(cache_cmp_kv[l]), page_table, page, w1cat, pos8, w2)
    o_nsa_s, win_s = nsa_sample(hs["q"], hs["gate"], kcvc_s, lane_rows(cache_sel_kv[l]), page_table, page, hs["sel"],
                                lane_rows(cache_win_kv[l]), hs["win"], t_new)
    o_moba_s = moba_sample(hs["q"], q_moba_col, lane_rows(cache_moba_kv[l]), page_table, page, hs["moba"], t_new)
    o_mem_s = cross_attend_sample(hs["q"], q_mem_col, lane_rows(cache_mem_kv[l]), nb, t_new,
                                  CROSS_STEP_BATCH if nb % CROSS_STEP_BATCH == 0 else 1)
    y_sample = merge_branches(xs, o_nsa_s, o_moba_s, o_mem_s, hs["z"], hs["merge"], *br, final_norm_g)

    nsa_shape = lambda a, b_, t_: a.reshape(1, b_, t_, 2, NSA_KV_HEADS, HEAD_DIM)
    moba_shape = lambda a, b_, t_: a.reshape(1, b_, t_, 2, MOBA_HEADS, HEAD_DIM)
    keep_p = min(WINDOW, seq)
    keep_s = min(WINDOW, n_win + t_new)
    return (y_prompt.reshape(batch, seq, d_model), y_sample.reshape(nb, t_new, d_model),
            nsa_shape(hp["cmp"], batch, seq), nsa_shape(hp["sel"], batch, seq), moba_shape(hp["moba"], batch, seq),
            nsa_shape(hp["win"], batch, seq)[:, :, seq - keep_p:],
            mem_kv_p.reshape(1, batch, n_mem, 2, MEM_HEADS, HEAD_DIM),
            nsa_shape(hs["cmp"], nb, t_new), nsa_shape(hs["sel"], nb, t_new), moba_shape(hs["moba"], nb, t_new),
            nsa_shape(win_s, nb, keep_s))
```

```python
import functools

import numpy as np
import jax
import jax.numpy as jnp
from jax import lax
from jax.experimental import pallas as pl
from jax.experimental.pallas import tpu as pltpu

HEAD_DIM = 128
NSA_HEADS = 8
NSA_KV_HEADS = 2
NSA_GROUP = NSA_HEADS // NSA_KV_HEADS
CMP_LEN = 32
CMP_STRIDE = 16
SEL_BLOCK = 64
SEL_TOP_N = 16
WINDOW = 512
FORCE_BONUS = 1.0e4
MOBA_HEADS = 4
MOBA_BLOCK = 256
MOBA_TOP_K = 3
MEM_HEADS = 4
N_BRANCH = 3
RMS_EPS = 1e-6
NEG_INF = -1e30
SCALE = HEAD_DIM ** -0.5
LOG2E = 1.4426950408889634
MASKED = -(2.0 ** 100)

LANES = 128
STAGE_PITCH = 24
VMEM_LIMIT = 56 * 1024 * 1024

NORM_ROWS = 512
PROJ_ROWS, PROJ_COLS = 2048, 1024
KV_ROWS = 1024
NSA_Q_ROWS = 256
CROSS_Q_ROWS = 1024
MERGE_ROWS, MERGE_COLS = 256, 512
SAMPLE_STEP_BATCH = 2
CROSS_STEP_BATCH = 4

F32 = jnp.float32
BF16 = jnp.bfloat16


def _params(n_axes, vmem=None):
    return pltpu.CompilerParams(dimension_semantics=("parallel",) * n_axes, vmem_limit_bytes=vmem)


def _dot(a, b):
    return jnp.dot(a, b, preferred_element_type=F32)


def _dot_nt(a, b):
    return lax.dot_general(a, b, (((1,), (1,)), ((), ())), preferred_element_type=F32)


def _sigmoid(x):
    return 1.0 / (1.0 + jnp.exp(-x))


def _rms_kernel(x_ref, g_ref, o_ref):
    x = x_ref[...]
    inv = lax.rsqrt(jnp.mean(x * x, axis=-1, keepdims=True) + RMS_EPS)
    o_ref[...] = (x * inv * g_ref[...]).astype(o_ref.dtype)


def rmsnorm_bf16(x, g, tm):
    n, d = x.shape
    return pl.pallas_call(
        _rms_kernel,
        grid=(n // tm,),
        in_specs=[pl.BlockSpec((tm, d), lambda i: (i, 0)), pl.BlockSpec((1, d), lambda i: (0, 0))],
        out_specs=pl.BlockSpec((tm, d), lambda i: (i, 0)),
        out_shape=jax.ShapeDtypeStruct((n, d), BF16),
        compiler_params=_params(1, VMEM_LIMIT),
        name="rmsnorm",
    )(x, g.reshape(1, d))


def _mm_kernel(x_ref, w_ref, o_ref):
    o_ref[...] = _dot(x_ref[...], w_ref[...]).astype(o_ref.dtype)


def matmul(x, w, out_dtype, tm, tn, name):
    m, k = x.shape
    n = w.shape[1]
    tm, tn = min(tm, m), min(tn, n)
    return pl.pallas_call(
        _mm_kernel,
        grid=(m // tm, n // tn),
        in_specs=[pl.BlockSpec((tm, k), lambda i, j: (i, 0)), pl.BlockSpec((k, tn), lambda i, j: (0, j))],
        out_specs=pl.BlockSpec((tm, tn), lambda i, j: (i, j)),
        out_shape=jax.ShapeDtypeStruct((m, n), out_dtype),
        compiler_params=_params(2, VMEM_LIMIT),
        name=name,
    )(x, w)


def _mm_rows_kernel(x_ref, w_ref, *o_refs):
    tm = x_ref.shape[0]
    res = _dot(x_ref[...], w_ref[...])
    col = 0
    for o_ref in o_refs:
        n_slab = o_ref.shape[0] // tm
        for j in range(n_slab):
            o_ref[pl.ds(j, tm, stride=n_slab), :] = res[:, col:col + LANES]
            col += LANES


def matmul_rows(x, w, widths, tm, name):
    m, k = x.shape
    n = w.shape[1]
    assert sum(widths) == n and all(width % LANES == 0 for width in widths)
    tm = min(tm, m)
    return pl.pallas_call(
        _mm_rows_kernel,
        grid=(m // tm,),
        in_specs=[pl.BlockSpec((tm, k), lambda i: (i, 0)), pl.BlockSpec((k, n), lambda i: (0, 0))],
        out_specs=[pl.BlockSpec((tm * width // LANES, LANES), lambda i: (i, 0)) for width in widths],
        out_shape=[jax.ShapeDtypeStruct((m * width // LANES, LANES), F32) for width in widths],
        compiler_params=_params(1, VMEM_LIMIT),
        name=name,
    )(x, w)


def _compress_kernel(pt_ref, pool_ref, w1_ref, pos_ref, w2_ref, out_ref, pages_scr, sem, lhs_scr, stage_scr,
                     *, n_pages, page, bb):
    n_sub_one = n_pages * page // CMP_STRIDE
    n_pages = bb * n_pages
    rows_pp = 4 * page
    step = pl.program_id(0)
    slot = step % 2

    def page_copies(s, dst_slot):
        return [pltpu.make_async_copy(
            pool_ref.at[pl.ds(pl.multiple_of(pt_ref[s * n_pages + k] * rows_pp, rows_pp), rows_pp), :],
            pages_scr.at[dst_slot, pl.ds(k * rows_pp, rows_pp), :], sem.at[dst_slot]) for k in range(n_pages)]

    @pl.when(step == 0)
    def _():
        for c in page_copies(0, 0):
            c.start()

    @pl.when(step + 1 < pl.num_programs(0))
    def _():
        for c in page_copies(step + 1, 1 - slot):
            c.start()

    for c in page_copies(step, slot):
        c.wait()
    page_refs = [pages_scr.at[slot, pl.ds(k * rows_pp, rows_pp), :] for k in range(n_pages)]
    sub_per_page = page // CMP_STRIDE
    n_sub = n_pages * sub_per_page
    for p in range(n_pages):
        for cg in range(4):
            stage = stage_scr.at[(p % 2) * 4 + cg]
            for j in range(sub_per_page):
                stage[j * STAGE_PITCH:j * STAGE_PITCH + CMP_STRIDE, :] = (
                    page_refs[p][pl.ds(4 * CMP_STRIDE * j + cg, CMP_STRIDE, stride=4), :])
            for s in range(CMP_STRIDE):
                lhs_scr[cg, p * sub_per_page:(p + 1) * sub_per_page, s * HEAD_DIM:(s + 1) * HEAD_DIM] = (
                    stage[pl.ds(s, sub_per_page, stride=STAGE_PITCH), :])
    row = lax.broadcasted_iota(jnp.int32, (n_sub, HEAD_DIM), 0)
    for c in range(2):
        w1 = w1_ref[c]
        bias = _dot(pos_ref[c], w1)
        b0 = bias[0:1, :HEAD_DIM]
        b1 = bias[1:2, HEAD_DIM:]
        for g in range(2):
            a = _dot(lhs_scr[c * 2 + g].astype(BF16), w1)
            a0 = a[:, :HEAD_DIM] + b0
            a1 = a[:, HEAD_DIM:] + b1
            hid = a0 + pltpu.roll(a1, shift=n_sub - 1, axis=0)
            act = hid * _sigmoid(hid)
            o = _dot(act.astype(BF16), w2_ref[c])
            o = jnp.where(row % n_sub_one < n_sub_one - 1, o, 0.0)
            for i in range(bb):
                out_ref[i, c, g] = o[i * n_sub_one:(i + 1) * n_sub_one]


def _page_specs(n_pages, rows, bb=1):
    return [pl.BlockSpec((rows, LANES),
                         functools.partial(lambda b, pt, i, p: (pt[(b * bb + i) * n_pages + p], 0), i=i, p=p))
            for i in range(bb) for p in range(n_pages)]


def compress_tokens(pool, page_table, page, w1cat, pos8, w2):
    nb, n_pages = page_table.shape
    n_sub = n_pages * page // CMP_STRIDE
    bb = SAMPLE_STEP_BATCH if nb % SAMPLE_STEP_BATCH == 0 else 1
    const = lambda shape: pl.BlockSpec(shape, lambda b, pt: (0,) * len(shape))
    grid_spec = pltpu.PrefetchScalarGridSpec(
        num_scalar_prefetch=1,
        grid=(nb // bb,),
        in_specs=[pl.BlockSpec(memory_space=pl.ANY), const(w1cat.shape), const(pos8.shape), const(w2.shape)],
        out_specs=pl.BlockSpec((bb, 2, 2, n_sub, HEAD_DIM), lambda b, pt: (b, 0, 0, 0, 0)),
        scratch_shapes=[pltpu.VMEM((2, bb * n_pages * 4 * page, LANES), F32), pltpu.SemaphoreType.DMA((2,)),
                        pltpu.VMEM((4, bb * n_sub, CMP_STRIDE * HEAD_DIM), F32),
                        pltpu.VMEM((8, page // CMP_STRIDE * STAGE_PITCH, LANES), F32)],
    )
    return pl.pallas_call(
        functools.partial(_compress_kernel, n_pages=n_pages, page=page, bb=bb),
        grid_spec=grid_spec,
        out_shape=jax.ShapeDtypeStruct((nb, 2, 2, n_sub, HEAD_DIM), F32),
        compiler_params=pltpu.CompilerParams(dimension_semantics=("arbitrary",), vmem_limit_bytes=VMEM_LIMIT),
        name="compress",
    )(page_table.reshape(-1), pool, w1cat, pos8, w2)


def compress_weights(phi_pos, phi_w1, phi_w2):
    n_half = CMP_LEN // CMP_STRIDE
    w1 = phi_w1.reshape(2, n_half, CMP_STRIDE, HEAD_DIM, HEAD_DIM).transpose(0, 2, 3, 1, 4)
    w1cat = w1.reshape(2, CMP_STRIDE * HEAD_DIM, n_half * HEAD_DIM).astype(BF16)
    pos = phi_pos.reshape(2, n_half, CMP_STRIDE * HEAD_DIM)
    pos8 = jnp.pad(pos, ((0, 0), (0, 8 - n_half), (0, 0))).astype(BF16)
    return w1cat, pos8, phi_w2.astype(BF16)


def _softmax_attend(s, v):
    e = jnp.exp2(s - jnp.max(s, axis=-1, keepdims=True))
    d = jnp.sum(e, axis=-1, keepdims=True)
    return _dot(e.astype(BF16), v) / jnp.where(d > 0, d, 1.0)


def _biased_attend(q, k, v, bias, key_slope):
    return _softmax_attend(_dot_nt(q, k) * (SCALE * LOG2E) + key_slope + bias, v)


def _top_rank(score, n_blocks):
    blk = lax.broadcasted_iota(jnp.int32, score.shape, 1)
    rank = jnp.zeros(score.shape, F32)
    for n2 in range(n_blocks):
        col = score[:, n2:n2 + 1]
        rank = rank + jnp.where(col > score, 1.0, 0.0) + jnp.where((col == score) & (blk > n2), 1.0, 0.0)
    return rank


def _alibi_slopes(n_heads):
    return [float(2.0 ** (-8.0 * (h + 1) / n_heads)) for h in range(n_heads)]


def _masked_softmax(s, valid, axis):
    s = jnp.where(valid, s, NEG_INF)
    e = jnp.where(valid, jnp.exp(s - jnp.max(s, axis=axis, keepdims=True)), 0.0)
    d = jnp.sum(e, axis=axis, keepdims=True)
    return e / jnp.where(d > 0, d, 1.0)


def _cmp_branch(q, kc, vc, cover, valid):
    imp = None
    outs = []
    for r in range(NSA_GROUP):
        p = _masked_softmax(_dot_nt(q[:, r * HEAD_DIM:(r + 1) * HEAD_DIM], kc) * SCALE, valid, -1)
        p_hi = p.astype(BF16)
        p_lo = (p - p_hi.astype(F32)).astype(BF16)
        outs.append(_dot(p_hi, vc))
        part = _dot(p_hi, cover) + _dot(p_lo, cover)
        imp = part if imp is None else imp + part
    return outs, imp.T


def _select_blocks_t(imp_t, tpos_row, n_sel):
    blk = lax.broadcasted_iota(jnp.int32, imp_t.shape, 0)
    cur = tpos_row // SEL_BLOCK
    forced = (blk == 0) | (blk == cur) | (blk == cur - 1)
    score = jnp.where(blk * SEL_BLOCK <= tpos_row, imp_t + FORCE_BONUS * jnp.where(forced, 1.0, 0.0), NEG_INF)
    rank = jnp.zeros(score.shape, F32)
    for n2 in range(n_sel):
        row = score[n2:n2 + 1, :]
        rank = rank + jnp.where(row > score, 1.0, 0.0) + jnp.where((row == score) & (blk > n2), 1.0, 0.0)
    return jnp.where(rank < min(SEL_TOP_N, n_sel), 1.0, 0.0)


def _select_blocks(imp, tpos, n_sel):
    blk = lax.broadcasted_iota(jnp.int32, imp.shape, 1)
    cur = tpos // SEL_BLOCK
    forced = (blk == 0) | (blk == cur) | (blk == cur - 1)
    allowed = (blk * SEL_BLOCK <= tpos) & (blk < n_sel)
    score = jnp.where(allowed, imp + FORCE_BONUS * jnp.where(forced, 1.0, 0.0), NEG_INF)
    rank = _top_rank(score, n_sel)
    return jnp.where((rank < min(SEL_TOP_N, n_sel)) & (blk < n_sel), 1.0, 0.0)


def _kv_rows(ref, token0, n_tokens, slab, n_slab):
    return ref[pl.ds(token0 * n_slab + slab, n_tokens, stride=n_slab), :]


def _causal_bias(bias, tq):
    w = bias.shape[1]
    row = lax.broadcasted_iota(jnp.int32, (tq, tq), 0)
    col = lax.broadcasted_iota(jnp.int32, (tq, tq), 1)
    diag = jnp.where(row >= col, bias[:, w - tq:], MASKED)
    return diag if w == tq else jnp.concatenate([bias[:, :w - tq], diag], axis=1)


def _nsa_prompt_kernel(q_ref, gate_ref, kcvc_ref, sel_ref, win_ref, cover_ref, expand_ref, wbias_ref, o_ref,
                       ks_scr, vs_scr, kw_scr, vw_scr, *, tq, seq):
    g = pl.program_id(1)
    qi = pl.program_id(2)
    nq = seq // tq

    @pl.when(qi == 0)
    def _():
        ks_scr[...] = _kv_rows(sel_ref, 0, seq, g, 4).astype(BF16)
        vs_scr[...] = _kv_rows(sel_ref, 0, seq, 2 + g, 4).astype(BF16)
        kw_scr[...] = _kv_rows(win_ref, 0, seq, g, 4).astype(BF16)
        vw_scr[...] = _kv_rows(win_ref, 0, seq, 2 + g, 4).astype(BF16)

    n_sel = seq // SEL_BLOCK
    n_sub = seq // CMP_STRIDE
    n_win = WINDOW // tq
    tpos = qi * tq + lax.broadcasted_iota(jnp.int32, (tq, 1), 0)
    tpos_row = qi * tq + lax.broadcasted_iota(jnp.int32, (1, tq), 1)
    q = q_ref[...]

    n_idx = lax.broadcasted_iota(jnp.int32, (1, n_sub), 1)
    valid = (n_idx * CMP_STRIDE + (CMP_LEN - 1) <= tpos) & (n_idx < n_sub - 1)
    o_cmp, imp_t = _cmp_branch(q, kcvc_ref[0, 0, 0].astype(BF16), kcvc_ref[0, 1, 0].astype(BF16), cover_ref[...],
                               valid)
    sel_t = _select_blocks_t(imp_t[:n_sel], tpos_row, n_sel)
    selected = jnp.concatenate([sel_t, jnp.zeros((LANES - n_sel, tq), F32)], axis=0).T.astype(BF16)

    slopes = _alibi_slopes(NSA_HEADS)
    slope = [jnp.where(g == 0, jnp.float32(slopes[r] * LOG2E), jnp.float32(slopes[NSA_GROUP + r] * LOG2E))
             for r in range(NSA_GROUP)]
    gates = _sigmoid(gate_ref[...])

    for c in range(nq):
        @pl.when(qi == c)
        def _(c=c):
            w = (c + 1) * tq
            bias = _causal_bias(_dot(selected, expand_ref[:, :w]) + MASKED, tq)
            k = ks_scr[:w, :]
            v = vs_scr[:w, :]
            kpos = lax.broadcasted_iota(jnp.int32, (1, w), 1).astype(F32)
            c0 = max(c - n_win, 0)
            ww = (c + 1 - c0) * tq
            kw = kw_scr[c0 * tq:(c + 1) * tq, :]
            vw = vw_scr[c0 * tq:(c + 1) * tq, :]
            wbias = wbias_ref[:, (n_win + 1) * tq - ww:]
            kpos_w = (c0 * tq + lax.broadcasted_iota(jnp.int32, (1, ww), 1)).astype(F32)
            for r in range(NSA_GROUP):
                qr = q[:, r * HEAD_DIM:(r + 1) * HEAD_DIM]
                o_sel = _biased_attend(qr, k, v, bias, slope[r] * kpos)
                o_win = _biased_attend(qr, kw, vw, wbias, slope[r] * kpos_w)
                o_ref[:, r * HEAD_DIM:(r + 1) * HEAD_DIM] = (gates[:, 3 * r:3 * r + 1] * o_cmp[r]
                                                              + gates[:, 3 * r + 1:3 * r + 2] * o_sel
                                                              + gates[:, 3 * r + 2:3 * r + 3] * o_win)


def nsa_prompt_constants(n_sub, n_keys, tq):
    cover, _ = nsa_constants(n_sub, LANES)
    expand = np.where(np.arange(n_keys)[None, :] // SEL_BLOCK == np.arange(LANES)[:, None], -MASKED, 0.0)
    dist = WINDOW + np.arange(tq)[:, None] - np.arange(WINDOW + tq)[None, :]
    wbias = np.where((dist >= 0) & (dist < WINDOW), 0.0, MASKED)
    return cover, jnp.asarray(expand, BF16), jnp.asarray(wbias, F32)


def nsa_constants(n_sub, n_keys):
    c_start = np.arange(n_sub)[:, None] * CMP_STRIDE
    s_start = np.arange(LANES)[None, :] * SEL_BLOCK
    cover = ((c_start < s_start + SEL_BLOCK) & (c_start + CMP_LEN > s_start)).astype(np.float32)
    cover[n_sub - 1:] = 0.0
    expand = (np.arange(n_keys)[None, :] // SEL_BLOCK == np.arange(LANES)[:, None]).astype(np.float32)
    return jnp.asarray(cover, BF16), jnp.asarray(expand, BF16)


def nsa_prompt(q_all, gates, kcvc, sel_kv, win_kv, batch, seq, tq=NSA_Q_ROWS):
    assert WINDOW % tq == 0 and seq % tq == 0 and (seq // SEL_BLOCK) % 8 == 0
    nq = seq // tq
    n_sub = seq // CMP_STRIDE
    cover, expand, wbias = nsa_prompt_constants(n_sub, seq, tq)
    gw = NSA_GROUP * HEAD_DIM
    kv_spec = pl.BlockSpec((seq * 4, LANES), lambda b, g, i: (b, 0))
    const = lambda a: pl.BlockSpec(a.shape, lambda b, g, i: (0, 0))
    return pl.pallas_call(
        functools.partial(_nsa_prompt_kernel, tq=tq, seq=seq),
        grid=(batch, NSA_KV_HEADS, nq),
        in_specs=[
            pl.BlockSpec((tq, gw), lambda b, g, i: (b * nq + i, g)),
            pl.BlockSpec((tq, LANES), lambda b, g, i: (b * nq + i, g)),
            pl.BlockSpec((1, 2, 1, n_sub, HEAD_DIM), lambda b, g, i: (b, 0, g, 0, 0)),
            kv_spec, kv_spec, const(cover), const(expand), const(wbias),
        ],
        out_specs=pl.BlockSpec((tq, gw), lambda b, g, i: (b * nq + i, g)),
        out_shape=jax.ShapeDtypeStruct((batch * seq, NSA_HEADS * HEAD_DIM), F32),
        scratch_shapes=[pltpu.VMEM((seq, HEAD_DIM), BF16)] * 4,
        compiler_params=pltpu.CompilerParams(dimension_semantics=("parallel", "parallel", "arbitrary"),
                                             vmem_limit_bytes=VMEM_LIMIT),
        name="nsa_prompt",
    )(q_all, gates, kcvc, sel_kv, win_kv, cover, expand, wbias)


def _moba_select(gate, n_past, n_blocks):
    blk = lax.broadcasted_iota(jnp.int32, gate.shape, 1)
    past = blk < n_past
    score = jnp.where(past, gate, NEG_INF)
    rank = _top_rank(score, n_blocks)
    return jnp.where(past & (rank < min(MOBA_TOP_K, n_blocks)), 1.0, 0.0)


def _moba_select_t(gate_t, n_past, n_blocks):
    blk = lax.broadcasted_iota(jnp.int32, gate_t.shape, 0)
    past = blk < n_past
    score = jnp.where(past, gate_t, NEG_INF)
    rank = jnp.zeros(score.shape, F32)
    for n2 in range(n_blocks):
        row = score[n2:n2 + 1, :]
        rank = rank + jnp.where(row > score, 1.0, 0.0) + jnp.where((row == score) & (blk > n2), 1.0, 0.0)
    return jnp.where(past & (rank < min(MOBA_TOP_K, n_blocks)), 1.0, 0.0)


def _moba_prompt_kernel(q_ref, kv_ref, o_ref, kmean_scr, k_scr, v_scr, *, seq):
    tq = MOBA_BLOCK
    n_slab = 2 * MOBA_HEADS
    qi = pl.program_id(1)
    nb = seq // tq
    slopes = [s * LOG2E for s in _alibi_slopes(MOBA_HEADS)]

    @pl.when(qi == 0)
    def _():
        kmean_scr[...] = jnp.zeros(kmean_scr.shape, F32)
        for h in range(MOBA_HEADS):
            k = _kv_rows(kv_ref, 0, seq, h, n_slab)
            k_scr[h] = k.astype(BF16)
            v_scr[h] = _kv_rows(kv_ref, 0, seq, MOBA_HEADS + h, n_slab).astype(BF16)
            for n in range(nb):
                kmean_scr[h, n:n + 1, :] = jnp.mean(k[n * tq:(n + 1) * tq], axis=0, keepdims=True)

    row = lax.broadcasted_iota(jnp.int32, (tq, tq), 0)
    col = lax.broadcasted_iota(jnp.int32, (tq, tq), 1)
    own = jnp.where(row >= col, 0.0, MASKED)
    q, sel_bias = [], []
    nb8 = -(-nb // 8) * 8
    for h in range(MOBA_HEADS):
        q.append(q_ref[:, h * HEAD_DIM:(h + 1) * HEAD_DIM])
        gate_t = _dot_nt(kmean_scr[h].astype(BF16), q[h])[:nb8]
        bias_t = (1.0 - _moba_select_t(gate_t, qi, nb - 1)) * MASKED
        sel_bias.append(jnp.concatenate([bias_t, jnp.zeros((LANES - nb8, tq), F32)], axis=0).T)

    for c in range(nb):
        @pl.when(qi == c)
        def _(c=c):
            w = (c + 1) * tq
            kpos = lax.broadcasted_iota(jnp.int32, (1, w), 1).astype(F32)
            for h in range(MOBA_HEADS):
                bias = jnp.concatenate(
                    [jnp.broadcast_to(sel_bias[h][:, n:n + 1], (tq, tq)) for n in range(c)] + [own], axis=1)
                o_ref[:, h * HEAD_DIM:(h + 1) * HEAD_DIM] = _biased_attend(q[h], k_scr[h, :w, :], v_scr[h, :w, :], bias,
                                                                            slopes[h] * kpos)


def moba_prompt(q_all, q_col, moba_kv, batch, seq):
    tq = MOBA_BLOCK
    nq = seq // tq
    width = MOBA_HEADS * HEAD_DIM
    return pl.pallas_call(
        functools.partial(_moba_prompt_kernel, seq=seq),
        grid=(batch, nq),
        in_specs=[
            pl.BlockSpec((tq, width), lambda b, i: (b * nq + i, q_col)),
            pl.BlockSpec((seq * 2 * MOBA_HEADS, LANES), lambda b, i: (b, 0)),
        ],
        out_specs=pl.BlockSpec((tq, width), lambda b, i: (b * nq + i, 0)),
        out_shape=jax.ShapeDtypeStruct((batch * seq, width), F32),
        scratch_shapes=[pltpu.VMEM((MOBA_HEADS, LANES, HEAD_DIM), F32), pltpu.VMEM((MOBA_HEADS, seq, HEAD_DIM), BF16),
                        pltpu.VMEM((MOBA_HEADS, seq, HEAD_DIM), BF16)],
        compiler_params=pltpu.CompilerParams(dimension_semantics=("parallel", "arbitrary"),
                                             vmem_limit_bytes=VMEM_LIMIT),
        name="moba_prompt",
    )(q_all, moba_kv)


def _padded_tile(rows, n):
    return jnp.concatenate([rows, jnp.zeros((n - rows.shape[0], rows.shape[1]), F32)], axis=0).astype(BF16)


def _nsa_sample_kernel(pt_ref, *refs, n_pages, page, t_new, n_win, bb):
    page_refs = refs[:bb * n_pages]
    (q_ref, gate_ref, kcvc_ref, new_sel_ref, cwin_ref, new_win_ref, cover_ref, expand_ref, sbias_ref, wbias_ref,
     o_ref, win_out_ref, ks_scr, vs_scr, kw_scr, vw_scr) = refs[bb * n_pages:]
    n_slab = 2 * NSA_KV_HEADS
    past = n_pages * page
    n_sub = past // CMP_STRIDE
    n_sel = -(-(past + t_new) // SEL_BLOCK)
    rows = NSA_GROUP * t_new
    keep = win_out_ref.shape[0] // (bb * n_slab)
    chains = [(i, g) for i in range(bb) for g in range(NSA_KV_HEADS)]

    for i in range(bb):
        out0, kept = i * keep * n_slab, (keep - t_new) * n_slab
        win_out_ref[out0:out0 + kept, :] = cwin_ref[(i + 1) * n_win * n_slab - kept:(i + 1) * n_win * n_slab, :]
        win_out_ref[out0 + kept:out0 + keep * n_slab, :] = new_win_ref[i * t_new * n_slab:(i + 1) * t_new * n_slab, :]

    for c, (i, g) in enumerate(chains):
        for p in range(n_pages):
            ks_scr[c, p * page:(p + 1) * page, :] = _kv_rows(page_refs[i * n_pages + p], 0, page, g, n_slab).astype(BF16)
            vs_scr[c, p * page:(p + 1) * page, :] = (
                _kv_rows(page_refs[i * n_pages + p], 0, page, 2 + g, n_slab).astype(BF16))
        ks_scr[c, past:, :] = _padded_tile(_kv_rows(new_sel_ref, i * t_new, t_new, g, n_slab), LANES)
        vs_scr[c, past:, :] = _padded_tile(_kv_rows(new_sel_ref, i * t_new, t_new, 2 + g, n_slab), LANES)
        kw_scr[c, :n_win, :] = _kv_rows(cwin_ref, i * n_win, n_win, g, n_slab).astype(BF16)
        vw_scr[c, :n_win, :] = _kv_rows(cwin_ref, i * n_win, n_win, 2 + g, n_slab).astype(BF16)
        kw_scr[c, n_win:, :] = _padded_tile(_kv_rows(new_win_ref, i * t_new, t_new, g, n_slab), LANES)
        vw_scr[c, n_win:, :] = _padded_tile(_kv_rows(new_win_ref, i * t_new, t_new, 2 + g, n_slab), LANES)

    q_all = [_stack_heads(q_ref, i * t_new, t_new, NSA_HEADS) for i in range(bb)]
    q = [q_all[i][g * rows:(g + 1) * rows] for i, g in chains]
    tpos = past + lax.broadcasted_iota(jnp.int32, (rows, 1), 0) % t_new
    tpos_q = past + lax.broadcasted_iota(jnp.int32, (t_new, 1), 0)
    n_idx = lax.broadcasted_iota(jnp.int32, (1, n_sub), 1)
    valid = (n_idx * CMP_STRIDE + (CMP_LEN - 1) <= tpos) & (n_idx < n_sub - 1)
    cmp_out = [_cmp_branch_rows(q[c], kcvc_ref[i, 0, g].astype(BF16), kcvc_ref[i, 1, g].astype(BF16),
                                cover_ref[...], valid) for c, (i, g) in enumerate(chains)]
    selected = []
    for c in range(len(chains)):
        imp = cmp_out[c][1]
        imp_t = imp[0:t_new]
        for r in range(1, NSA_GROUP):
            imp_t = imp_t + imp[r * t_new:(r + 1) * t_new]
        selected.append(jnp.concatenate([_select_blocks(imp_t, tpos_q, n_sel)] * NSA_GROUP, axis=0).astype(BF16))
    o_sel, o_win = [], []
    for c, (i, g) in enumerate(chains):
        bias = (_dot(selected[c], expand_ref[...]) + MASKED) + sbias_ref[g]
        o_sel.append(_softmax_attend(_dot_nt(q[c], ks_scr[c]) * (SCALE * LOG2E) + bias, vs_scr[c]))
        o_win.append(_softmax_attend(_dot_nt(q[c], kw_scr[c]) * (SCALE * LOG2E) + wbias_ref[g], vw_scr[c]))

    gates = _sigmoid(gate_ref[...])
    for c, (i, g) in enumerate(chains):
        out_rows = slice(i * t_new, (i + 1) * t_new)
        for r in range(NSA_GROUP):
            h = g * NSA_GROUP + r
            lane = g * LANES + 3 * r
            rs = slice(r * t_new, (r + 1) * t_new)
            o_ref[out_rows, h * HEAD_DIM:(h + 1) * HEAD_DIM] = (
                gates[out_rows, lane:lane + 1] * cmp_out[c][0][rs]
                + gates[out_rows, lane + 1:lane + 2] * o_sel[c][rs]
                + gates[out_rows, lane + 2:lane + 3] * o_win[c][rs])


def _cmp_branch_rows(q, kc, vc, cover, valid):
    s = jnp.where(valid, _dot_nt(q, kc) * SCALE, NEG_INF)
    e = jnp.where(valid, jnp.exp(s - jnp.max(s, axis=-1, keepdims=True)), 0.0)
    d = jnp.sum(e, axis=-1, keepdims=True)
    p = e / jnp.where(d > 0, d, 1.0)
    p_hi = p.astype(BF16)
    p_lo = (p - p_hi.astype(F32)).astype(BF16)
    return _dot(p_hi, vc), _dot(p_hi, cover) + _dot(p_lo, cover)


def nsa_sample(q_all, gates, kcvc, sel_pool, page_table, page, new_sel, cache_win, new_win, t_new):
    nb, n_pages = page_table.shape
    past = n_pages * page
    n_sub = past // CMP_STRIDE
    n_slab = 2 * NSA_KV_HEADS
    n_win = cache_win.shape[0] // (n_slab * nb)
    assert past % SEL_BLOCK == 0 and t_new <= SEL_BLOCK and past // SEL_BLOCK + 2 <= LANES
    cover, expand = nsa_constants(n_sub, past + LANES)
    expand = expand * jnp.asarray(-MASKED, BF16)
    slopes = _alibi_slopes(NSA_HEADS)
    t_pos = past + np.arange(t_new)

    def alibi_bias(pos, ok):
        per_head = [np.where(ok, s * LOG2E * pos[None, :], MASKED) for s in slopes]
        return jnp.asarray(np.stack(per_head).reshape(NSA_KV_HEADS, NSA_GROUP * t_new, pos.shape[0]), F32)

    sel_pos = np.arange(past + LANES)
    sbias = alibi_bias(sel_pos, (sel_pos[None, :] <= t_pos[:, None]) & (sel_pos[None, :] < past + t_new))
    win_pos = past - n_win + np.arange(n_win + LANES)
    dist = t_pos[:, None] - win_pos[None, :]
    wbias = alibi_bias(win_pos, (dist >= 0) & (dist < WINDOW) & (win_pos[None, :] >= 0)
                       & (win_pos[None, :] < past + t_new))
    keep = min(WINDOW, n_win + t_new)
    assert keep >= t_new and (n_slab * (keep - t_new)) % 8 == 0
    bb = SAMPLE_STEP_BATCH if nb % SAMPLE_STEP_BATCH == 0 else 1
    n_chain = bb * NSA_KV_HEADS
    width = NSA_HEADS * HEAD_DIM
    const = lambda a: pl.BlockSpec(a.shape, lambda b, pt: (0,) * a.ndim)
    rows = lambda n: pl.BlockSpec((bb * n_slab * n, LANES), lambda b, pt: (b, 0))
    grid_spec = pltpu.PrefetchScalarGridSpec(
        num_scalar_prefetch=1,
        grid=(nb // bb,),
        in_specs=_page_specs(n_pages, n_slab * page, bb) + [
            pl.BlockSpec((bb * t_new, width), lambda b, pt: (b, 0)),
            pl.BlockSpec((bb * t_new, NSA_KV_HEADS * LANES), lambda b, pt: (b, 0)),
            pl.BlockSpec((bb, 2, 2, n_sub, HEAD_DIM), lambda b, pt: (b, 0, 0, 0, 0)),
            rows(t_new), rows(n_win), rows(t_new),
            const(cover), const(expand), const(sbias), const(wbias)],
        out_specs=[pl.BlockSpec((bb * t_new, width), lambda b, pt: (b, 0)), rows(keep)],
        scratch_shapes=[pltpu.VMEM((n_chain, past + LANES, HEAD_DIM), BF16),
                        pltpu.VMEM((n_chain, past + LANES, HEAD_DIM), BF16),
                        pltpu.VMEM((n_chain, n_win + LANES, HEAD_DIM), BF16),
                        pltpu.VMEM((n_chain, n_win + LANES, HEAD_DIM), BF16)],
    )
    return pl.pallas_call(
        functools.partial(_nsa_sample_kernel, n_pages=n_pages, page=page, t_new=t_new, n_win=n_win, bb=bb),
        grid_spec=grid_spec,
        out_shape=[jax.ShapeDtypeStruct((nb * t_new, width), F32),
                   jax.ShapeDtypeStruct((nb * keep * n_slab, LANES), F32)],
        compiler_params=_params(1, VMEM_LIMIT),
        name="nsa_sample",
    )(page_table.reshape(-1), *([sel_pool] * (bb * n_pages)), q_all, gates, kcvc, new_sel, cache_win, new_win, cover,
      expand, sbias, wbias)


def _slab_attend(q, kv, bias, shift):
    s = _dot_nt(q, kv) * (SCALE * LOG2E) + bias
    e = jnp.exp2(s - jnp.max(s, axis=-1, keepdims=True))
    d = jnp.sum(e, axis=-1, keepdims=True)
    return _dot(pltpu.roll(e, shift, axis=1).astype(BF16), kv) / jnp.where(d > 0, d, 1.0)


def _stack_heads(ref, row0, n_rows, n_heads):
    return jnp.concatenate([ref[row0:row0 + n_rows, h * HEAD_DIM:(h + 1) * HEAD_DIM] for h in range(n_heads)],
                           axis=0).astype(BF16)


def _slab_bias(head_slab, slopes, key_pos, key_ok, n_slab, t_new):
    n_heads, n_keys = len(head_slab), key_pos.shape[0]
    slab = np.arange(n_keys * n_slab) % n_slab
    out = np.full((n_heads, t_new, n_keys * n_slab), MASKED, np.float64)
    for h in range(n_heads):
        ok = np.repeat(key_ok, n_slab, axis=1) & (slab == head_slab[h])[None, :]
        out[h] = np.where(ok, slopes[h] * LOG2E * np.repeat(key_pos, n_slab)[None, :], MASKED)
    return jnp.asarray(out.reshape(n_heads * t_new, n_keys * n_slab), F32)


def _moba_sample_kernel(pt_ref, *refs, n_pages, page, t_new):
    page_refs = refs[:n_pages]
    q_ref, new_ref, bias_ref, o_ref, kv_scr, mean_scr = refs[n_pages:]
    n_slab = 2 * MOBA_HEADS
    rows_pp = page * n_slab
    past = n_pages * page
    pages_per_block = MOBA_BLOCK // page
    n_past = n_pages // pages_per_block
    block_w = MOBA_BLOCK * n_slab

    mean_scr[...] = jnp.zeros(mean_scr.shape, F32)
    for n in range(n_past):
        total = None
        for p in range(n * pages_per_block, (n + 1) * pages_per_block):
            x = page_refs[p][...]
            kv_scr[p * rows_pp:(p + 1) * rows_pp, :] = x.astype(BF16)
            part = jnp.sum(x.reshape(page, n_slab, LANES), axis=0)
            total = part if total is None else total + part
        mean_scr[n * n_slab:(n + 1) * n_slab, :] = total * (1.0 / MOBA_BLOCK)
    x_new = new_ref[...]
    kv_scr[past * n_slab:, :] = _padded_tile(x_new, LANES)
    mean_scr[n_past * n_slab:(n_past + 1) * n_slab, :] = (
        jnp.sum(x_new.reshape(t_new, n_slab, LANES), axis=0) * (1.0 / MOBA_BLOCK))

    tpos = past + lax.broadcasted_iota(jnp.int32, (t_new, 1), 0)
    cur = tpos // MOBA_BLOCK
    blk = lax.broadcasted_iota(jnp.int32, (t_new, LANES), 1)
    sel_bias = []
    for h in range(MOBA_HEADS):
        q = q_ref[:, h * HEAD_DIM:(h + 1) * HEAD_DIM].astype(BF16)
        kmean = mean_scr[pl.ds(h, LANES // n_slab, stride=n_slab), :]
        kmean = jnp.concatenate([kmean, jnp.zeros((LANES - kmean.shape[0], LANES), F32)], axis=0)
        gate = _dot_nt(q, kmean.astype(BF16))
        selected = jnp.where(blk == cur, 1.0, _moba_select(gate, cur, n_past))
        sel_bias.append((1.0 - selected) * MASKED)
    sel_bias = jnp.concatenate(sel_bias, axis=0)

    bias = jnp.concatenate(
        [bias_ref[:, n * block_w:(n + 1) * block_w] + sel_bias[:, n:n + 1] for n in range(n_past)]
        + [bias_ref[:, n_past * block_w:] + sel_bias[:, n_past:n_past + 1]], axis=1)
    o = _slab_attend(_stack_heads(q_ref, 0, t_new, MOBA_HEADS), kv_scr[...], bias, MOBA_HEADS)
    for h in range(MOBA_HEADS):
        o_ref[:, h * HEAD_DIM:(h + 1) * HEAD_DIM] = o[h * t_new:(h + 1) * t_new]


def moba_sample(q_all, q_col, moba_pool, page_table, page, new_kv, t_new):
    nb, n_pages = page_table.shape
    past = n_pages * page
    width = MOBA_HEADS * HEAD_DIM
    n_slab = 2 * MOBA_HEADS
    assert past % MOBA_BLOCK == 0 and t_new * n_slab <= LANES and past // MOBA_BLOCK < LANES // n_slab
    new_keys = LANES // n_slab
    key_pos = np.arange(past + new_keys)
    t_pos = past + np.arange(t_new)
    key_ok = (key_pos[None, :] <= t_pos[:, None]) & (key_pos[None, :] < past + t_new)
    bias = _slab_bias(list(range(MOBA_HEADS)), _alibi_slopes(MOBA_HEADS), key_pos, key_ok, n_slab, t_new)
    grid_spec = pltpu.PrefetchScalarGridSpec(
        num_scalar_prefetch=1,
        grid=(nb,),
        in_specs=_page_specs(n_pages, n_slab * page) + [
            pl.BlockSpec((t_new, width), lambda b, pt: (b, q_col)),
            pl.BlockSpec((t_new * n_slab, LANES), lambda b, pt: (b, 0)),
            pl.BlockSpec(bias.shape, lambda b, pt: (0, 0))],
        out_specs=pl.BlockSpec((t_new, width), lambda b, pt: (b, 0)),
        scratch_shapes=[pltpu.VMEM((past * n_slab + LANES, HEAD_DIM), BF16), pltpu.VMEM((LANES, LANES), F32)],
    )
    return pl.pallas_call(
        functools.partial(_moba_sample_kernel, n_pages=n_pages, page=page, t_new=t_new),
        grid_spec=grid_spec,
        out_shape=jax.ShapeDtypeStruct((nb * t_new, width), F32),
        compiler_params=_params(1, VMEM_LIMIT),
        name="moba_sample",
    )(page_table.reshape(-1), *([moba_pool] * n_pages), q_all, new_kv, bias)


def _cross_kernel(q_ref, kv_ref, o_ref, *, n_mem):
    n_slab = 2 * MEM_HEADS
    for h in range(MEM_HEADS):
        k = _kv_rows(kv_ref, 0, n_mem, h, n_slab).astype(BF16)
        v = _kv_rows(kv_ref, 0, n_mem, MEM_HEADS + h, n_slab).astype(BF16)
        s = _dot_nt(q_ref[:, h * HEAD_DIM:(h + 1) * HEAD_DIM], k) * SCALE
        e = jnp.exp(s - jnp.max(s, axis=-1, keepdims=True))
        p = e / jnp.sum(e, axis=-1, keepdims=True)
        o_ref[:, h * HEAD_DIM:(h + 1) * HEAD_DIM] = _dot(p.astype(BF16), v)


def cross_attend(q_all, q_col, mem_kv, batch, rows, tq):
    nq = rows // tq
    n_slab = 2 * MEM_HEADS
    width = MEM_HEADS * HEAD_DIM
    n_mem = mem_kv.shape[0] // (batch * n_slab)
    return pl.pallas_call(
        functools.partial(_cross_kernel, n_mem=n_mem),
        grid=(batch, nq),
        in_specs=[
            pl.BlockSpec((tq, width), lambda b, i: (b * nq + i, q_col)),
            pl.BlockSpec((n_mem * n_slab, LANES), lambda b, i: (b, 0)),
        ],
        out_specs=pl.BlockSpec((tq, width), lambda b, i: (b * nq + i, 0)),
        out_shape=jax.ShapeDtypeStruct((batch * rows, width), F32),
        compiler_params=_params(2, VMEM_LIMIT),
        name="cross_attend",
    )(q_all, mem_kv)


def _cross_sample_kernel(q_ref, kv_ref, bias_ref, o_ref, *, n_mem, t_new, bb):
    n_slab = 2 * MEM_HEADS
    rows = n_mem * n_slab
    for i in range(bb):
        q = _stack_heads(q_ref, i * t_new, t_new, MEM_HEADS)
        o = _slab_attend(q, kv_ref[i * rows:(i + 1) * rows, :].astype(BF16), bias_ref[...], MEM_HEADS)
        for h in range(MEM_HEADS):
            o_ref[i * t_new:(i + 1) * t_new, h * HEAD_DIM:(h + 1) * HEAD_DIM] = o[h * t_new:(h + 1) * t_new]


def cross_attend_sample(q_all, q_col, mem_kv, batch, t_new, bb):
    assert batch % bb == 0
    n_slab = 2 * MEM_HEADS
    width = MEM_HEADS * HEAD_DIM
    n_mem = mem_kv.shape[0] // (batch * n_slab)
    bias = _slab_bias(list(range(MEM_HEADS)), [0.0] * MEM_HEADS, np.arange(n_mem), np.ones((t_new, n_mem), bool),
                      n_slab, t_new)
    return pl.pallas_call(
        functools.partial(_cross_sample_kernel, n_mem=n_mem, t_new=t_new, bb=bb),
        grid=(batch // bb,),
        in_specs=[
            pl.BlockSpec((bb * t_new, width), lambda b: (b, q_col)),
            pl.BlockSpec((bb * n_mem * n_slab, LANES), lambda b: (b, 0)),
            pl.BlockSpec(bias.shape, lambda b: (0, 0)),
        ],
        out_specs=pl.BlockSpec((bb * t_new, width), lambda b: (b, 0)),
        out_shape=jax.ShapeDtypeStruct((batch * t_new, width), F32),
        compiler_params=_params(1, VMEM_LIMIT),
        name="cross_sample",
    )(q_all, mem_kv, bias)


def _merge_kernel(x_ref, on_ref, om_ref, oc_ref, z_ref, gl_ref, wn_ref, wm_ref, wc_ref, wo_ref, fg_ref, y_ref, mix_scr,
                  *, tn):
    d = x_ref.shape[1]
    wn_k, wm_k = wn_ref.shape[0], wm_ref.shape[0]
    z = z_ref[...]
    act = z * _sigmoid(z)
    un = (on_ref[...] * act[:, :wn_k]).astype(BF16)
    um = (om_ref[...] * act[:, wn_k:wn_k + wm_k]).astype(BF16)
    uc = (oc_ref[...] * act[:, wn_k + wm_k:]).astype(BF16)
    for c in range(d // tn):
        cols = slice(c * tn, (c + 1) * tn)
        mix = (_sigmoid(gl_ref[:, c * tn:(c + 1) * tn]) * _dot(un, wn_ref[:, cols])
               + _sigmoid(gl_ref[:, d + c * tn:d + (c + 1) * tn]) * _dot(um, wm_ref[:, cols])
               + _sigmoid(gl_ref[:, 2 * d + c * tn:2 * d + (c + 1) * tn]) * _dot(uc, wc_ref[:, cols]))
        mix_scr[:, cols] = mix.astype(BF16)
    y = x_ref[...] + _dot(mix_scr[...], wo_ref[...])
    inv = lax.rsqrt(jnp.mean(y * y, axis=-1, keepdims=True) + RMS_EPS)
    y_ref[...] = y * inv * fg_ref[...]


def merge_branches(x, o_nsa, o_moba, o_mem, z_all, gate_logits, w_nsa, w_moba, w_mem, w_out, final_g,
                   tm=MERGE_ROWS, tn=MERGE_COLS):
    m, d = x.shape
    tm = min(tm, m)
    row = lambda a: pl.BlockSpec((tm, a.shape[1]), lambda i: (i, 0))
    whole = lambda a: pl.BlockSpec(a.shape, lambda i: (0, 0), pipeline_mode=pl.Buffered(1))
    final_g = final_g.reshape(1, d)
    return pl.pallas_call(
        functools.partial(_merge_kernel, tn=tn),
        grid=(m // tm,),
        in_specs=[row(x), row(o_nsa), row(o_moba), row(o_mem), row(z_all), row(gate_logits),
                  whole(w_nsa), whole(w_moba), whole(w_mem), whole(w_out), whole(final_g)],
        out_specs=row(x),
        out_shape=jax.ShapeDtypeStruct((m, d), F32),
        scratch_shapes=[pltpu.VMEM((tm, d), BF16)],
        compiler_params=_params(1, VMEM_LIMIT),
        name="merge",
    )(x, o_nsa, o_moba, o_mem, z_all, gate_logits, w_nsa, w_moba, w_mem, w_out, final_g)


def _in_weights(w_in, d_model):
    splits = (NSA_HEADS * HEAD_DIM, 4 * HEAD_DIM, 4 * HEAD_DIM, 4 * HEAD_DIM, 3 * NSA_HEADS, NSA_HEADS * HEAD_DIM,
              MOBA_HEADS * HEAD_DIM, 2 * MOBA_HEADS * HEAD_DIM, MOBA_HEADS * HEAD_DIM, MEM_HEADS * HEAD_DIM,
              MEM_HEADS * HEAD_DIM, N_BRANCH * d_model)
    offs = np.cumsum((0,) + splits)
    assert offs[-1] == w_in.shape[1]
    (nsa_q, cmp_kv, sel_kv, win_kv, nsa_g, nsa_z, moba_q, moba_kv, moba_z, mem_q, mem_z, merge_g) = (
        w_in[:, offs[i]:offs[i + 1]] for i in range(len(splits)))
    per_group = 3 * NSA_GROUP
    gate_w = jnp.pad(nsa_g.reshape(-1, NSA_KV_HEADS, per_group), ((0, 0), (0, 0), (0, LANES - per_group)))
    cast = lambda w: w.astype(BF16)
    return dict(q=cast(jnp.concatenate([nsa_q, moba_q, mem_q], axis=1)),
                nsa_kv=cast(jnp.concatenate([cmp_kv, sel_kv, win_kv], axis=1)), moba=cast(moba_kv),
                gate=cast(gate_w.reshape(-1, NSA_KV_HEADS * LANES)),
                z=cast(jnp.concatenate([nsa_z, moba_z, mem_z], axis=1)), merge=cast(merge_g))


def _in_projection(x, norm_g, w, q_dtype):
    xn = rmsnorm_bf16(x, norm_g, NORM_ROWS)
    tm = min(PROJ_ROWS, x.shape[0] // 2)
    mm = lambda name, dtype: matmul(xn, w[name], dtype, tm, PROJ_COLS, "in_" + name)
    nsa_w = 4 * HEAD_DIM
    cmp_kv, sel_kv, win_kv = matmul_rows(xn, w["nsa_kv"], [nsa_w] * 3, min(KV_ROWS, x.shape[0] // 2), "in_nsa_kv")
    moba_kv, = matmul_rows(xn, w["moba"], [2 * MOBA_HEADS * HEAD_DIM], min(KV_ROWS, x.shape[0] // 2), "in_moba")
    return dict(q=mm("q", q_dtype), cmp=cmp_kv, sel=sel_kv, win=win_kv, moba=moba_kv,
                gate=mm("gate", F32), z=mm("z", F32), merge=mm("merge", F32))


def kernel(x_prompt, x_sample, cache_cmp_kv, cache_sel_kv, cache_moba_kv, cache_win_kv, cache_mem_kv, page_table,
           mem_prompt, norm_g, w_in, phi_pos, phi_w1, phi_w2, mem_norm_g, w_mem_kv, w_br_nsa, w_br_moba, w_br_mem,
           w_out, final_norm_g):
    batch, seq, d_model = x_prompt.shape
    nb, t_new, _ = x_sample.shape
    n_mem = mem_prompt.shape[1]
    page = cache_cmp_kv.shape[2]
    n_win = cache_win_kv.shape[2]
    assert norm_g.shape[0] == 1, "one layer"
    assert seq % MOBA_BLOCK == 0 and (page_table.shape[1] * page) % MOBA_BLOCK == 0 and MOBA_BLOCK % page == 0
    l = 0
    q_moba_col = NSA_HEADS * HEAD_DIM // (MOBA_HEADS * HEAD_DIM)
    q_mem_col = (NSA_HEADS + MOBA_HEADS) * HEAD_DIM // (MEM_HEADS * HEAD_DIM)
    lane_rows = lambda a: a.reshape(-1, LANES)

    w = _in_weights(w_in[l], d_model)
    w1cat, pos8, w2 = compress_weights(phi_pos[l], phi_w1[l], phi_w2[l])
    br = [a[l].astype(BF16) for a in (w_br_nsa, w_br_moba, w_br_mem, w_out)]

    xp = x_prompt.reshape(batch * seq, d_model)
    hp = _in_projection(xp, norm_g[l], w, BF16)
    mem_n = rmsnorm_bf16(mem_prompt.reshape(batch * n_mem, d_model), mem_norm_g[l], NORM_ROWS)
    mem_kv_p, = matmul_rows(mem_n, w_mem_kv[l].astype(BF16), [w_mem_kv.shape[2]], KV_ROWS, "mem_kv")
    pages_p = seq // page
    identity_table = jnp.arange(batch * pages_p, dtype=jnp.int32).reshape(batch, pages_p)
    kcvc_p = compress_tokens(hp["cmp"], identity_table, page, w1cat, pos8, w2)
    o_nsa_p = nsa_prompt(hp["q"], hp["gate"], kcvc_p, hp["sel"], hp["win"], batch, seq)
    o_moba_p = moba_prompt(hp["q"], q_moba_col, hp["moba"], batch, seq)
    o_mem_p = cross_attend(hp["q"], q_mem_col, mem_kv_p, batch, seq, min(seq, CROSS_Q_ROWS))
    y_prompt = merge_branches(xp, o_nsa_p, o_moba_p, o_mem_p, hp["z"], hp["merge"], *br, final_norm_g)

    xs = x_sample.reshape(nb * t_new, d_model)
    hs = _in_projection(xs, norm_g[l], w, F32)
    kcvc_s = compress_tokens(lane_rows(cache_cmp_kv[l]), page_table, page, w1cat, pos8, w2)
    o_nsa_s, win_s = nsa_sample(hs["q"], hs["gate"], kcvc_s, lane_rows(cache_sel_kv[l]), page_table, page, hs["sel"],
                                lane_rows(cache_win_kv[l]), hs["win"], t_new)
    o_moba_s = moba_sample(hs["q"], q_moba_col, lane_rows(cache_moba_kv[l]), page_table, page, hs["moba"], t_new)
    o_mem_s = cross_attend_sample(hs["q"], q_mem_col, lane_rows(cache_mem_kv[l]), nb, t_new,
                                  CROSS_STEP_BATCH if nb % CROSS_STEP_BATCH == 0 else 1)
    y_sample = merge_branches(xs, o_nsa_s, o_moba_s, o_mem_s, hs["z"], hs["merge"], *br, final_norm_g)

    nsa_shape = lambda a, b_, t_: a.reshape(1, b_, t_, 2, NSA_KV_HEADS, HEAD_DIM)
    moba_shape = lambda a, b_, t_: a.reshape(1, b_, t_, 2, MOBA_HEADS, HEAD_DIM)
    keep_p = min(WINDOW, seq)
    keep_s = min(WINDOW, n_win + t_new)
    return (y_prompt.reshape(batch, seq, d_model), y_sample.reshape(nb, t_new, d_model),
            nsa_shape(hp["cmp"], batch, seq), nsa_shape(hp["sel"], batch, seq), moba_shape(hp["moba"], batch, seq),
            nsa_shape(hp["win"], batch, seq)[:, :, seq - keep_p:],
            mem_kv_p.reshape(1, batch, n_mem, 2, MEM_HEADS, HEAD_DIM),
            nsa_shape(hs["cmp"], nb, t_new), nsa_shape(hs["sel"], nb, t_new), moba_shape(hs["moba"], nb, t_new),
            nsa_shape(win_s, nb, keep_s))
```
